```python
import math
import jax, jax.numpy as jnp
from jax import lax
import numpy as np

D_MODEL = 1024
BATCH = 2
SEQ = 8192
DEPTH = 4
DEC_BATCH = 32
DEC_SEQ = 1
PAST_LEN = 8192
PAGE_SIZE = 128

N_RG_LAYERS = DEPTH // 2
N_ATT_LAYERS = DEPTH - N_RG_LAYERS
LRU_WIDTH = D_MODEL
N_GATE_BLOCKS = 8
GATE_BLOCK = LRU_WIDTH // N_GATE_BLOCKS
CONV_WIDTH = 4
LRU_C = 8.0
N_HEADS = 8
HEAD_DIM = D_MODEL // N_HEADS
N_KV_HEADS = 4
KV_DIM = N_KV_HEADS * HEAD_DIM
ROT_DIM = HEAD_DIM // 4
ROPE_THETA = 500000.0
MOBA_BLOCK = 256
MOBA_TOPK = 3
Q_CHUNK = 32
D_FF = -(-8 * D_MODEL // (3 * 256)) * 256
EPS = 1e-6
NEG = -1e30

kernel_name = "yoco_griffin_moba_decoder_step"


def rmsnorm(x, g):
    xf = x.astype(jnp.float32)
    y = xf * lax.rsqrt(jnp.mean(xf * xf, axis=-1, keepdims=True) + EPS)
    return (y * g.astype(jnp.float32)).astype(x.dtype)


def partial_rope(x, pos):
    half = ROT_DIM // 2
    inv = ROPE_THETA ** (-jnp.arange(half, dtype=jnp.float32) / half)
    ang = pos.astype(jnp.float32)[:, None] * inv[None, :]
    cos = jnp.cos(ang)[:, None, :]
    sin = jnp.sin(ang)[:, None, :]
    xr = x[..., :ROT_DIM].astype(jnp.float32)
    x1, x2 = xr[..., :half], xr[..., half:]
    rot = jnp.concatenate([x1 * cos - x2 * sin, x2 * cos + x1 * sin], axis=-1).astype(x.dtype)
    return jnp.concatenate([rot, x[..., ROT_DIM:]], axis=-1)


def swiglu_ffn(x, g, w_in, w_out):
    h = rmsnorm(x, g) @ w_in
    gate, up = h[..., :D_FF], h[..., D_FF:]
    return x + (jax.nn.silu(gate) * up) @ w_out


def causal_conv(u, conv_state, w, b):
    T = u.shape[1]
    ext = jnp.concatenate([conv_state.astype(u.dtype), u], axis=1)
    out = b + ext[:, 0:T] * w[0]
    for j in range(1, CONV_WIDTH):
        out = out + ext[:, j:j + T] * w[j]
    return out, ext[:, T:]


def rglru(xc, h0, wa, ba, wi, bi, lam):
    B, T, C = xc.shape
    xb = xc.reshape(B, T, N_GATE_BLOCKS, GATE_BLOCK)
    r = jax.nn.sigmoid(jnp.einsum('btnc,ncd->btnd', xb, wa).reshape(B, T, C) + ba)
    i = jax.nn.sigmoid(jnp.einsum('btnc,ncd->btnd', xb, wi).reshape(B, T, C) + bi)
    log_a = (-LRU_C * r.astype(jnp.float32)) * jax.nn.softplus(-lam.astype(jnp.float32))
    a = jnp.exp(log_a)
    b = jnp.sqrt(-jnp.expm1(2.0 * log_a)) * (i * xc).astype(jnp.float32)
    b = b.at[:, 0].add(a[:, 0] * h0.astype(jnp.float32))

    def combine(left, right):
        return (left[0] * right[0], right[0] * left[1] + right[1])

    _, h = lax.associative_scan(combine, (a, b), axis=1)
    return h, h[:, -1]


def recurrent_block(x, conv_state, h0, g, w_in, conv_w, conv_b, wa, ba, wi, bi, lam, w_out):
    xn = rmsnorm(x, g)
    proj = xn @ w_in
    gate, u = proj[..., :LRU_WIDTH], proj[..., LRU_WIDTH:]
    uc, new_conv = causal_conv(u, conv_state, conv_w, conv_b)
    h, h_last = rglru(uc, h0, wa, ba, wi, bi, lam)
    y = (h.astype(x.dtype) * jax.nn.gelu(gate)) @ w_out
    return x + y, new_conv, h_last.astype(x.dtype)


def shared_kv(x, pos, g_kv, w_kv):
    B, T, _ = x.shape
    kv = rmsnorm(x, g_kv) @ w_kv
    k = kv[..., :KV_DIM].reshape(B, T, N_KV_HEADS, HEAD_DIM)
    v = kv[..., KV_DIM:].reshape(B, T, N_KV_HEADS, HEAD_DIM)
    return partial_rope(k, pos), v


def to_blocks(k, v):
    B, L = k.shape[:2]
    nb = -(-L // MOBA_BLOCK)
    pad = nb * MOBA_BLOCK - L
    widths = ((0, 0), (0, pad), (0, 0), (0, 0))
    kb = jnp.pad(k, widths).reshape(B, nb, MOBA_BLOCK, N_KV_HEADS, HEAD_DIM)
    vb = jnp.pad(v, widths).reshape(B, nb, MOBA_BLOCK, N_KV_HEADS, HEAD_DIM)
    km = jnp.mean(kb.astype(jnp.float32), axis=2)
    return kb, vb, km


def moba_chunk(q, qpos, kb, vb, km, topk):
    B, NB = kb.shape[:2]
    Tq = q.shape[1]
    grp = N_HEADS // N_KV_HEADS
    means_h = jnp.repeat(km, grp, axis=2)
    own = qpos // MOBA_BLOCK
    gate = jnp.einsum('bqhd,bnhd->bqhn', q.astype(jnp.float32), means_h)
    past = jnp.arange(NB)[None, :] < own[:, None]
    gate = jnp.where(past[None, :, None, :], gate, NEG)
    _, sel = lax.top_k(gate, topk)
    sel_ok = jnp.arange(topk)[None, :] < own[:, None]
    own_b = jnp.broadcast_to(own[None, :, None, None], (B, Tq, N_HEADS, 1)).astype(sel.dtype)
    blk = jnp.concatenate([sel, own_b], axis=-1)
    b_idx = jnp.arange(B)[:, None, None, None]
    g_idx = (jnp.arange(N_HEADS) // grp)[None, None, :, None]
    kg = kb[b_idx, blk, :, g_idx, :]
    vg = vb[b_idx, blk, :, g_idx, :]
    logits = jnp.einsum('bqhd,bqhnkd->bqhnk', q, kg,
                        preferred_element_type=jnp.float32) * (1.0 / math.sqrt(HEAD_DIM))
    kpos = blk[..., None] * MOBA_BLOCK + jnp.arange(MOBA_BLOCK)
    slot_ok = jnp.concatenate([sel_ok, jnp.ones((Tq, 1), dtype=bool)], axis=-1)
    mask = slot_ok[None, :, None, :, None] & (kpos <= qpos[None, :, None, None, None])
    logits = jnp.where(mask, logits, NEG)
    p = jax.nn.softmax(logits.reshape(B, Tq, N_HEADS, -1), axis=-1).reshape(logits.shape)
    out = jnp.einsum('bqhnk,bqhnkd->bqhd', p, vg.astype(jnp.float32))
    return out.astype(q.dtype)


def moba_layer(x, pos, kb, vb, km, g, w_q, w_o, chunked):
    B, T, _ = x.shape
    topk = min(MOBA_TOPK, kb.shape[1])
    q = partial_rope((rmsnorm(x, g) @ w_q).reshape(B, T, N_HEADS, HEAD_DIM), pos)
    if chunked:
        nc = T // Q_CHUNK
        qc = q.reshape(B, nc, Q_CHUNK, N_HEADS, HEAD_DIM).transpose(1, 0, 2, 3, 4)
        pc = pos.reshape(nc, Q_CHUNK)
        o = lax.map(lambda a: moba_chunk(a[0], a[1], kb, vb, km, topk), (qc, pc))
        o = o.transpose(1, 0, 2, 3, 4)
    else:
        o = moba_chunk(q, pos, kb, vb, km, topk)
    return x + o.reshape(B, T, N_HEADS * HEAD_DIM) @ w_o


def trunk(x, pos, conv_state, h_state, past_k, past_v, chunked,
          g_mix, g_ffn, w_rg_in, rg_conv_w, rg_conv_b, rg_gate_a_w, rg_gate_a_b,
          rg_gate_i_w, rg_gate_i_b, rg_lambda, w_rg_out, g_kv, w_kv, w_q, w_o,
          w_ffn_in, w_ffn_out, g_final):
    new_conv, new_h = [], []
    k_new = v_new = kb = vb = km = None
    for l in range(DEPTH):
        if l < N_RG_LAYERS:
            x, c, h = recurrent_block(x, conv_state[l], h_state[l], g_mix[l], w_rg_in[l],
                                      rg_conv_w[l], rg_conv_b[l], rg_gate_a_w[l], rg_gate_a_b[l],
                                      rg_gate_i_w[l], rg_gate_i_b[l], rg_lambda[l], w_rg_out[l])
            new_conv.append(c)
            new_h.append(h)
        else:
            if l == N_RG_LAYERS:
                k_new, v_new = shared_kv(x, pos, g_kv, w_kv)
                if past_k is None:
                    kb, vb, km = to_blocks(k_new, v_new)
                else:
                    kb, vb, km = to_blocks(jnp.concatenate([past_k, k_new], axis=1),
                                           jnp.concatenate([past_v, v_new], axis=1))
            a = l - N_RG_LAYERS
            x = moba_layer(x, pos, kb, vb, km, g_mix[l], w_q[a], w_o[a], chunked)
        x = swiglu_ffn(x, g_ffn[l], w_ffn_in[l], w_ffn_out[l])
    return rmsnorm(x, g_final), jnp.stack(new_conv), jnp.stack(new_h), k_new, v_new


def setup_inputs(seed: int = 0) -> dict:
    key = jax.random.key(seed)
    ks = jax.random.split(key, 32)
    f32 = jnp.float32
    n_pages = PAST_LEN // PAGE_SIZE
    n_used = DEC_BATCH * n_pages
    n_pool = n_used + max(1, n_used // 4)
    page_table = jax.random.permutation(ks[0], n_pool)[:n_used].astype(jnp.int32).reshape(DEC_BATCH, n_pages)

    def nrm(k, shape, scale):
        return jax.random.normal(k, shape, f32) * scale

    a_base = jax.random.uniform(ks[1], (N_RG_LAYERS, LRU_WIDTH), f32, 0.9, 0.999)
    s = a_base ** (1.0 / LRU_C)
    rg_lambda = jnp.log(s) - jnp.log1p(-s)
    return {
        "x_prompt": nrm(ks[2], (BATCH, SEQ, D_MODEL), 1.0),
        "x_sample": nrm(ks[3], (DEC_BATCH, DEC_SEQ, D_MODEL), 1.0),
        "cache_k": nrm(ks[4], (n_pool, PAGE_SIZE, N_KV_HEADS, HEAD_DIM), 1.0),
        "cache_v": nrm(ks[5], (n_pool, PAGE_SIZE, N_KV_HEADS, HEAD_DIM), 1.0),
        "state_conv": nrm(ks[6], (N_RG_LAYERS, DEC_BATCH, CONV_WIDTH - 1, LRU_WIDTH), 1.0),
        "state_h": nrm(ks[7], (N_RG_LAYERS, DEC_BATCH, LRU_WIDTH), 0.5),
        "page_table": page_table,
        "g_mix": 1.0 + nrm(ks[8], (DEPTH, D_MODEL), 0.02),
        "g_ffn": 1.0 + nrm(ks[9], (DEPTH, D_MODEL), 0.02),
        "w_rg_in": nrm(ks[10], (N_RG_LAYERS, D_MODEL, 2 * LRU_WIDTH), D_MODEL ** -0.5),
        "rg_conv_w": nrm(ks[11], (N_RG_LAYERS, CONV_WIDTH, LRU_WIDTH), CONV_WIDTH ** -0.5),
        "rg_conv_b": nrm(ks[12], (N_RG_LAYERS, LRU_WIDTH), 0.05),
        "rg_gate_a_w": nrm(ks[13], (N_RG_LAYERS, N_GATE_BLOCKS, GATE_BLOCK, GATE_BLOCK), GATE_BLOCK ** -0.5),
        "rg_gate_a_b": nrm(ks[14], (N_RG_LAYERS, LRU_WIDTH), 0.05),
        "rg_gate_i_w": nrm(ks[15], (N_RG_LAYERS, N_GATE_BLOCKS, GATE_BLOCK, GATE_BLOCK), GATE_BLOCK ** -0.5),
        "rg_gate_i_b": nrm(ks[16], (N_RG_LAYERS, LRU_WIDTH), 0.05),
        "rg_lambda": rg_lambda,
        "w_rg_out": nrm(ks[17], (N_RG_LAYERS, LRU_WIDTH, D_MODEL), LRU_WIDTH ** -0.5),
        "g_kv": 1.0 + nrm(ks[18], (D_MODEL,), 0.02),
        "w_kv": nrm(ks[19], (D_MODEL, 2 * KV_DIM), D_MODEL ** -0.5),
        "w_q": nrm(ks[20], (N_ATT_LAYERS, D_MODEL, N_HEADS * HEAD_DIM), D_MODEL ** -0.5),
        "w_o": nrm(ks[21], (N_ATT_LAYERS, N_HEADS * HEAD_DIM, D_MODEL), (N_HEADS * HEAD_DIM) ** -0.5),
        "w_ffn_in": nrm(ks[22], (DEPTH, D_MODEL, 2 * D_FF), D_MODEL ** -0.5),
        "w_ffn_out": nrm(ks[23], (DEPTH, D_FF, D_MODEL), D_FF ** -0.5),
        "g_final": 1.0 + nrm(ks[24], (D_MODEL,), 0.02),
    }


def reference(x_prompt, x_sample, cache_k, cache_v, state_conv, state_h, page_table,
              g_mix, g_ffn, w_rg_in, rg_conv_w, rg_conv_b, rg_gate_a_w, rg_gate_a_b,
              rg_gate_i_w, rg_gate_i_b, rg_lambda, w_rg_out, g_kv, w_kv, w_q, w_o,
              w_ffn_in, w_ffn_out, g_final):
    weights = (g_mix, g_ffn, w_rg_in, rg_conv_w, rg_conv_b, rg_gate_a_w, rg_gate_a_b,
               rg_gate_i_w, rg_gate_i_b, rg_lambda, w_rg_out, g_kv, w_kv, w_q, w_o,
               w_ffn_in, w_ffn_out, g_final)
    B, S, _ = x_prompt.shape
    pos_p = jnp.arange(S, dtype=jnp.int32)
    conv0 = jnp.zeros((N_RG_LAYERS, B, CONV_WIDTH - 1, LRU_WIDTH), x_prompt.dtype)
    h0 = jnp.zeros((N_RG_LAYERS, B, LRU_WIDTH), x_prompt.dtype)
    y_prompt, p_conv, p_h, p_k, p_v = trunk(x_prompt, pos_p, conv0, h0, None, None, True, *weights)
    DB, T, _ = x_sample.shape
    past_len = page_table.shape[1] * PAGE_SIZE
    pos_s = past_len + jnp.arange(T, dtype=jnp.int32)
    past_k = cache_k[page_table].reshape(DB, past_len, N_KV_HEADS, HEAD_DIM)
    past_v = cache_v[page_table].reshape(DB, past_len, N_KV_HEADS, HEAD_DIM)
    y_sample, s_conv, s_h, s_k, s_v = trunk(x_sample, pos_s, state_conv, state_h, past_k, past_v,
                                            False, *weights)
    return (y_prompt, y_sample, p_conv, p_h, p_k, p_v, s_conv, s_h, s_k, s_v)
```

```python
import functools
import math

import jax
import jax.numpy as jnp
from jax import lax
from jax.experimental import pallas as pl
from jax.experimental.pallas import tpu as pltpu

D_MODEL = 1024
LRU_WIDTH = D_MODEL
N_GATE_BLOCKS = 8
GATE_BLOCK = LRU_WIDTH // N_GATE_BLOCKS
CONV_WIDTH = 4
LRU_C = 8.0
N_HEADS = 8
HEAD_DIM = D_MODEL // N_HEADS
N_KV_HEADS = 4
KV_DIM = N_KV_HEADS * HEAD_DIM
GROUP = N_HEADS // N_KV_HEADS
ROT_DIM = HEAD_DIM // 4
ROPE_THETA = 500000.0
MOBA_BLOCK = 256
MOBA_TOPK = 3
PAGE_SIZE = 128
PAGES_PER_BLOCK = MOBA_BLOCK // PAGE_SIZE
D_FF = -(-8 * D_MODEL // (3 * 256)) * 256
EPS = 1e-6
NEG = -1e30
SM_SCALE = 1.0 / math.sqrt(HEAD_DIM)

F32 = jnp.float32
BF16 = jnp.bfloat16
HIGHEST = lax.Precision.HIGHEST

SUBLANES = 8
FF_CHUNK = 256
RG_TILE = 256
FFN_TILE = 512
KV_TILE = 512
VMEM_LIMIT = 56 * 1024 * 1024

NT_DIMS = (((1,), (1,)), ((), ()))


def _params(*sem):
    return pltpu.CompilerParams(dimension_semantics=sem, vmem_limit_bytes=VMEM_LIMIT)


def _resident(shape):
    zeros = (0,) * len(shape)
    return pl.BlockSpec(shape, lambda *_: zeros, pipeline_mode=pl.Buffered(1))


def _rms(x, g):
    ms = jnp.mean(x * x, axis=-1, keepdims=True)
    return x * lax.rsqrt(ms + EPS) * g


def _softplus(z):
    return jnp.maximum(z, 0.0) + jnp.log1p(jnp.exp(-jnp.abs(z)))


def _rope(x, c, s_lo, s_hi):
    n = x.shape[-1]
    half = ROT_DIM // 2
    return x * c + pltpu.roll(x, half, 1) * s_hi + pltpu.roll(x, n - half, 1) * s_lo


def _rope_tables(pos):
    half = ROT_DIM // 2
    inv = ROPE_THETA ** (-jnp.arange(half, dtype=F32) / half)
    ang = pos.astype(F32)[:, None] * inv[None, :]
    cos, sin = jnp.cos(ang), jnp.sin(ang)
    t = pos.shape[0]
    rest = HEAD_DIM - ROT_DIM
    c = jnp.concatenate([cos, cos, jnp.ones((t, rest), F32)], axis=1)
    s_hi = jnp.concatenate([jnp.zeros((t, half), F32), sin, jnp.zeros((t, rest), F32)], axis=1)
    s_lo = jnp.concatenate([-sin, jnp.zeros((t, half + rest), F32)], axis=1)
    return c, s_lo, s_hi


def _lru_coeffs(z_a, z_i, uc, sp):
    r = jax.nn.sigmoid(z_a)
    i = jax.nn.sigmoid(z_i)
    log_a = (-LRU_C * r) * sp
    a = jnp.exp(log_a)
    b = jnp.sqrt(-jnp.tanh(log_a) * (a * a + 1.0)) * (i * uc)
    return a, b


def _rg_prompt_kernel(x_ref, g_ref, win_ref, cw_ref, cb_ref, wg_ref, ba_ref, bi_ref, lam_ref,
                      wout_ref, xo_ref, conv_ref, hl_ref, ext_ref, a_ref, b_ref, gg_ref, hc_ref):
    tt = RG_TILE
    t = pl.program_id(1)

    @pl.when(t == 0)
    def _():
        ext_ref[0:SUBLANES, :] = jnp.zeros((SUBLANES, LRU_WIDTH), F32)
        hc_ref[...] = jnp.zeros_like(hc_ref)

    x = x_ref[0]
    xn = _rms(x, g_ref[...]).astype(BF16)
    proj = jnp.dot(xn, win_ref[...], preferred_element_type=F32)
    gg_ref[...] = jax.nn.gelu(proj[:, :LRU_WIDTH])
    ext_ref[SUBLANES:SUBLANES + tt, :] = proj[:, LRU_WIDTH:]

    base = SUBLANES - (CONV_WIDTH - 1)
    uc = cb_ref[...] + ext_ref[base:base + tt, :] * cw_ref[0:1, :]
    for j in range(1, CONV_WIDTH):
        uc = uc + ext_ref[base + j:base + j + tt, :] * cw_ref[j:j + 1, :]
    conv_ref[0] = ext_ref[tt + base:tt + SUBLANES, :]
    ext_ref[0:SUBLANES, :] = ext_ref[tt:tt + SUBLANES, :]

    sp = _softplus(-lam_ref[...])
    for n in range(N_GATE_BLOCKS):
        sl = slice(n * GATE_BLOCK, (n + 1) * GATE_BLOCK)
        ucn = uc[:, sl]
        z = jnp.dot(ucn.astype(BF16), wg_ref[n], preferred_element_type=F32)
        a, b = _lru_coeffs(z[:, :GATE_BLOCK] + ba_ref[:, sl], z[:, GATE_BLOCK:] + bi_ref[:, sl],
                           ucn, sp[:, sl])
        a_ref[:, sl] = a
        b_ref[:, sl] = b

    row = lax.broadcasted_iota(jnp.int32, (SUBLANES, LRU_WIDTH), 0)

    def group(gi, h_prev):
        r0 = pl.multiple_of(gi * SUBLANES, SUBLANES)
        a8 = a_ref[pl.ds(r0, SUBLANES), :]
        b8 = b_ref[pl.ds(r0, SUBLANES), :]
        s = 1
        while s < SUBLANES:
            keep = row >= s
            b8 = jnp.where(keep, a8 * pltpu.roll(b8, s, 0) + b8, b8)
            a8 = jnp.where(keep, a8 * pltpu.roll(a8, s, 0), a8)
            s *= 2
        h = a8 * h_prev + b8
        b_ref[pl.ds(r0, SUBLANES), :] = h
        return h[SUBLANES - 1:SUBLANES, :]

    h_last = lax.fori_loop(0, tt // SUBLANES, group, hc_ref[...])
    hc_ref[...] = h_last
    hl_ref[0] = h_last

    y = jnp.dot((b_ref[...] * gg_ref[...]).astype(BF16), wout_ref[...], preferred_element_type=F32)
    xo_ref[0] = x + y


def _rg_prompt(x, g, w_in, conv_w, conv_b, wg, ba, bi, lam, w_out):
    bsz, t, d = x.shape
    tt = RG_TILE
    vec = lambda n: _resident((1, n))
    return pl.pallas_call(
        _rg_prompt_kernel,
        name="rg_prompt",
        grid=(bsz, t // tt),
        in_specs=[
            pl.BlockSpec((1, tt, d), lambda b, i: (b, i, 0)),
            vec(d),
            _resident((d, 2 * LRU_WIDTH)),
            _resident((CONV_WIDTH, LRU_WIDTH)),
            vec(LRU_WIDTH),
            _resident((N_GATE_BLOCKS, GATE_BLOCK, 2 * GATE_BLOCK)),
            vec(LRU_WIDTH), vec(LRU_WIDTH), vec(LRU_WIDTH),
            _resident((LRU_WIDTH, d)),
        ],
        out_specs=[
            pl.BlockSpec((1, tt, d), lambda b, i: (b, i, 0)),
            pl.BlockSpec((1, CONV_WIDTH - 1, LRU_WIDTH), lambda b, i: (b, 0, 0)),
            pl.BlockSpec((1, 1, LRU_WIDTH), lambda b, i: (b, 0, 0)),
        ],
        out_shape=[
            jax.ShapeDtypeStruct((bsz, t, d), F32),
            jax.ShapeDtypeStruct((bsz, CONV_WIDTH - 1, LRU_WIDTH), F32),
            jax.ShapeDtypeStruct((bsz, 1, LRU_WIDTH), F32),
        ],
        scratch_shapes=[
            pltpu.VMEM((tt + SUBLANES, LRU_WIDTH), F32),
            pltpu.VMEM((tt, LRU_WIDTH), F32),
            pltpu.VMEM((tt, LRU_WIDTH), F32),
            pltpu.VMEM((tt, LRU_WIDTH), F32),
            pltpu.VMEM((1, LRU_WIDTH), F32),
        ],
        compiler_params=_params("arbitrary", "arbitrary"),
    )(x, g, w_in, conv_w, conv_b, wg, ba, bi, lam, w_out)


def _ffn_kernel(x_ref, g_ref, win_ref, wout_ref, gf_ref, o_ref, act_ref, *, final, precision):
    cdt = win_ref.dtype
    x = x_ref[...]
    xn = _rms(x, g_ref[...]).astype(cdt)
    for c in range(D_FF // FF_CHUNK):
        lo = c * FF_CHUNK
        gate = jnp.dot(xn, win_ref[:, lo:lo + FF_CHUNK], preferred_element_type=F32,
                       precision=precision)
        up = jnp.dot(xn, win_ref[:, D_FF + lo:D_FF + lo + FF_CHUNK], preferred_element_type=F32,
                     precision=precision)
        act_ref[:, lo:lo + FF_CHUNK] = (jax.nn.silu(gate) * up).astype(cdt)
    y = x + jnp.dot(act_ref[...], wout_ref[...], preferred_element_type=F32, precision=precision)
    if final:
        y = _rms(y, gf_ref[...])
    o_ref[...] = y


def _ffn(x, g, w_in, w_out, g_final, *, final, tile):
    n, d = x.shape
    precision = HIGHEST if w_in.dtype == F32 else None
    return pl.pallas_call(
        functools.partial(_ffn_kernel, final=final, precision=precision),
        name="ffn",
        grid=(n // tile,),
        in_specs=[
            pl.BlockSpec((tile, d), lambda i: (i, 0)),
            _resident((1, d)),
            _resident((d, 2 * D_FF)),
            _resident((D_FF, d)),
            _resident((1, d)),
        ],
        out_specs=pl.BlockSpec((tile, d), lambda i: (i, 0)),
        out_shape=jax.ShapeDtypeStruct((n, d), F32),
        scratch_shapes=[pltpu.VMEM((tile, D_FF), w_in.dtype)],
        compiler_params=_params("arbitrary"),
    )(x, g, w_in, w_out, g_final)


def _kv_prompt_kernel(x_ref, g_ref, w_ref, c_ref, slo_ref, shi_ref,
                      k_ref, v_ref, kb_ref, vt_ref, km_ref):
    xn = _rms(x_ref[0], g_ref[...]).astype(BF16)
    kv = jnp.dot(xn, w_ref[...], preferred_element_type=F32)
    tile4 = lambda r: jnp.concatenate([r[...]] * N_KV_HEADS, axis=1)
    k = _rope(kv[:, :KV_DIM], tile4(c_ref), tile4(slo_ref), tile4(shi_ref))
    v = kv[:, KV_DIM:]
    k_ref[0] = k
    v_ref[0] = v
    kb_ref[0] = k.astype(BF16)
    for bb in range(KV_TILE // MOBA_BLOCK):
        rows = slice(bb * MOBA_BLOCK, (bb + 1) * MOBA_BLOCK)
        vt_ref[0, bb] = v[rows, :].T.astype(BF16)
        km_ref[0, bb] = jnp.sum(k[rows, :], axis=0, keepdims=True) * (1.0 / MOBA_BLOCK)


def _kv_prompt(x, g, w_kv, c, s_lo, s_hi):
    bsz, t, d = x.shape
    tt = KV_TILE
    nbt = tt // MOBA_BLOCK
    nb = t // MOBA_BLOCK
    tab = pl.BlockSpec((tt, HEAD_DIM), lambda b, i: (i, 0))
    row_spec = pl.BlockSpec((1, tt, KV_DIM), lambda b, i: (b, i, 0))
    return pl.pallas_call(
        _kv_prompt_kernel,
        name="kv_prompt",
        grid=(bsz, t // tt),
        in_specs=[
            pl.BlockSpec((1, tt, d), lambda b, i: (b, i, 0)),
            _resident((1, d)),
            _resident((d, 2 * KV_DIM)),
            tab, tab, tab,
        ],
        out_specs=[
            row_spec, row_spec, row_spec,
            pl.BlockSpec((1, nbt, KV_DIM, MOBA_BLOCK), lambda b, i: (b, i, 0, 0)),
            pl.BlockSpec((1, nbt, 1, KV_DIM), lambda b, i: (b, i, 0, 0)),
        ],
        out_shape=[
            jax.ShapeDtypeStruct((bsz, t, KV_DIM), F32),
            jax.ShapeDtypeStruct((bsz, t, KV_DIM), F32),
            jax.ShapeDtypeStruct((bsz, t, KV_DIM), BF16),
            jax.ShapeDtypeStruct((bsz, nb, KV_DIM, MOBA_BLOCK), BF16),
            jax.ShapeDtypeStruct((bsz, nb, 1, KV_DIM), F32),
        ],
        compiler_params=_params("arbitrary", "arbitrary"),
    )(x, g, w_kv, c, s_lo, s_hi)


def _select_bias(gate, n_allowed, n_valid):
    nb, nq = gate.shape
    rowi = lax.broadcasted_iota(jnp.int32, (nb, nq), 0).astype(F32)
    g = jnp.where(rowi < n_allowed, gate, NEG)
    bias = jnp.full((nb, nq), NEG, F32)
    for r in range(MOBA_TOPK):
        mx = jnp.max(g, axis=0, keepdims=True)
        idx = jnp.min(jnp.where(g == mx, rowi, float(nb)), axis=0, keepdims=True)
        ok = (n_valid > r).astype(F32)
        idx = idx * ok - (1.0 - ok)
        pick = rowi == idx
        bias = jnp.where(pick, 0.0, bias)
        g = jnp.where(pick, NEG, g)
    return bias


def _attn_prompt_kernel(x_ref, g_ref, wq_ref, wo_ref, c_ref, slo_ref, shi_ref, k_ref, vt_ref,
                        km_ref, o_ref, bias_ref, acc_ref, attn_ref):
    tq = MOBA_BLOCK
    nq = GROUP * tq
    i = pl.program_id(1)
    x = x_ref[0]
    xn = _rms(x, g_ref[...]).astype(BF16)
    q = jnp.dot(xn, wq_ref[...], preferred_element_type=F32)
    c, s_lo, s_hi = c_ref[...], slo_ref[...], shi_ref[...]
    own = i.astype(F32)

    key_pos = lax.broadcasted_iota(jnp.int32, (tq, nq), 0)
    qry_pos = lax.broadcasted_iota(jnp.int32, (tq, nq), 1) % tq
    causal = key_pos <= qry_pos

    for kvh in range(N_KV_HEADS):
        lanes = slice(kvh * HEAD_DIM, (kvh + 1) * HEAD_DIM)
        heads = [kvh * GROUP + j for j in range(GROUP)]
        q2 = jnp.concatenate(
            [_rope(q[:, h * HEAD_DIM:(h + 1) * HEAD_DIM], c, s_lo, s_hi) for h in heads], axis=0)
        gate = lax.dot_general(km_ref[0, :, lanes], q2, NT_DIMS, precision=HIGHEST,
                               preferred_element_type=F32)
        bias_ref[...] = _select_bias(gate, own, own)
        q2s = (q2 * SM_SCALE).astype(BF16)

        s = lax.dot_general(k_ref[0, i, :, lanes], q2s, NT_DIMS, preferred_element_type=F32)
        s = jnp.where(causal, s, NEG)
        m = jnp.max(s, axis=0, keepdims=True)
        p = jnp.exp(s - m)
        l = jnp.sum(p, axis=0, keepdims=True)
        acc_ref[...] = jnp.dot(vt_ref[0, i, lanes, :], p.astype(BF16), preferred_element_type=F32)

        def past_block(j, carry):
            m, l = carry
            s = lax.dot_general(k_ref[0, j, :, lanes], q2s, NT_DIMS, preferred_element_type=F32)
            s = s + bias_ref[pl.ds(j, 1), :]
            m_new = jnp.maximum(m, jnp.max(s, axis=0, keepdims=True))
            alpha = jnp.exp(m - m_new)
            p = jnp.exp(s - m_new)
            l = alpha * l + jnp.sum(p, axis=0, keepdims=True)
            acc_ref[...] = alpha * acc_ref[...] + jnp.dot(
                vt_ref[0, j, lanes, :], p.astype(BF16), preferred_element_type=F32)
            return m_new, l

        m, l = lax.fori_loop(0, i, past_block, (m, l))
        o = (acc_ref[...] / l).T
        for j, h in enumerate(heads):
            attn_ref[:, h * HEAD_DIM:(h + 1) * HEAD_DIM] = o[j * tq:(j + 1) * tq, :].astype(BF16)

    o_ref[0] = x + jnp.dot(attn_ref[...], wo_ref[...], preferred_element_type=F32)


def _attn_prompt(x, g, w_q, w_o, c, s_lo, s_hi, kb, vt, km):
    bsz, t, d = x.shape
    tq = MOBA_BLOCK
    nb = t // tq
    tab = pl.BlockSpec((tq, HEAD_DIM), lambda b, i: (i, 0))
    per_seq = lambda shape: pl.BlockSpec((1,) + shape, lambda b, i: (b,) + (0,) * len(shape),
                                         pipeline_mode=pl.Buffered(1))
    return pl.pallas_call(
        _attn_prompt_kernel,
        name="attn_prompt",
        grid=(bsz, nb),
        in_specs=[
            pl.BlockSpec((1, tq, d), lambda b, i: (b, i, 0)),
            _resident((1, d)),
            _resident((d, d)),
            _resident((d, d)),
            tab, tab, tab,
            per_seq((nb, tq, KV_DIM)),
            per_seq((nb, KV_DIM, tq)),
            per_seq((nb, KV_DIM)),
        ],
        out_specs=pl.BlockSpec((1, tq, d), lambda b, i: (b, i, 0)),
        out_shape=jax.ShapeDtypeStruct((bsz, t, d), F32),
        scratch_shapes=[
            pltpu.VMEM((nb, GROUP * tq), F32),
            pltpu.VMEM((HEAD_DIM, GROUP * tq), F32),
            pltpu.VMEM((tq, d), BF16),
        ],
        compiler_params=_params("arbitrary", "arbitrary"),
    )(x, g, w_q, w_o, c, s_lo, s_hi, kb, vt, km)


def _dot_hi(a, b):
    return jnp.dot(a, b, preferred_element_type=F32, precision=HIGHEST)


def _rg_sample_kernel(x_ref, cs_ref, h0_ref, g_ref, win_ref, cw_ref, cb_ref, wa_ref, wi_ref,
                      ba_ref, bi_ref, lam_ref, wout_ref, xo_ref, cso_ref, ho_ref):
    x = x_ref[...]
    proj = _dot_hi(_rms(x, g_ref[...]), win_ref[...])
    gate, u = proj[:, :LRU_WIDTH], proj[:, LRU_WIDTH:]
    uc = cb_ref[...] + cs_ref[0] * cw_ref[0:1, :]
    for j in range(1, CONV_WIDTH - 1):
        uc = uc + cs_ref[j] * cw_ref[j:j + 1, :]
    uc = uc + u * cw_ref[CONV_WIDTH - 1:CONV_WIDTH, :]
    for j in range(CONV_WIDTH - 2):
        cso_ref[j] = cs_ref[j + 1]
    cso_ref[CONV_WIDTH - 2] = u

    sp = _softplus(-lam_ref[...])
    h0 = h0_ref[...]
    hs = []
    for n in range(N_GATE_BLOCKS):
        sl = slice(n * GATE_BLOCK, (n + 1) * GATE_BLOCK)
        ucn = uc[:, sl]
        a, b = _lru_coeffs(_dot_hi(ucn, wa_ref[n]) + ba_ref[:, sl],
                           _dot_hi(ucn, wi_ref[n]) + bi_ref[:, sl], ucn, sp[:, sl])
        hs.append(b + a * h0[:, sl])
    h = jnp.concatenate(hs, axis=1)
    ho_ref[...] = h
    xo_ref[...] = x + _dot_hi(h * jax.nn.gelu(gate), wout_ref[...])


def _rg_sample(x, cs, h0, g, w_in, conv_w, conv_b, wa, wi, ba, bi, lam, w_out):
    n, d = x.shape
    return pl.pallas_call(
        _rg_sample_kernel,
        name="rg_sample",
        out_shape=[
            jax.ShapeDtypeStruct((n, d), F32),
            jax.ShapeDtypeStruct((CONV_WIDTH - 1, n, LRU_WIDTH), F32),
            jax.ShapeDtypeStruct((n, LRU_WIDTH), F32),
        ],
        compiler_params=pltpu.CompilerParams(vmem_limit_bytes=VMEM_LIMIT),
    )(x, cs, h0, g, w_in, conv_w, conv_b, wa, wi, ba, bi, lam, w_out)


def _proj_sample_kernel(x_ref, g_ref, w_ref, c_ref, slo_ref, shi_ref, o_ref, *, n_rope_heads):
    y = _dot_hi(_rms(x_ref[...], g_ref[...]), w_ref[...])
    if n_rope_heads:
        tile = lambda r: jnp.concatenate([r[...]] * n_rope_heads, axis=1)
        width = n_rope_heads * HEAD_DIM
        rot = _rope(y[:, :width], tile(c_ref), tile(slo_ref), tile(shi_ref))
        y = jnp.concatenate([rot, y[:, width:]], axis=1) if width < y.shape[1] else rot
    o_ref[...] = y


def _proj_sample(x, g, w, c, s_lo, s_hi, n_rope_heads):
    return pl.pallas_call(
        functools.partial(_proj_sample_kernel, n_rope_heads=n_rope_heads),
        name="proj_sample",
        out_shape=jax.ShapeDtypeStruct((x.shape[0], w.shape[1]), F32),
        compiler_params=pltpu.CompilerParams(vmem_limit_bytes=VMEM_LIMIT),
    )(x, g, w, c, s_lo, s_hi)


def _kmeans_sample_kernel(pt_ref, k0_ref, k1_ref, o_ref):
    n = pl.program_id(1)
    s = jnp.sum(k0_ref[0], axis=0, keepdims=True) + jnp.sum(k1_ref[0], axis=0, keepdims=True)
    o_ref[0, pl.ds(n, 1), :] = s * (1.0 / MOBA_BLOCK)


def _kmeans_sample(page_table_flat, cache_k, n_seq, n_pages):
    nb = n_pages // PAGES_PER_BLOCK

    def page(half):
        return pl.BlockSpec(
            (1, PAGE_SIZE, KV_DIM),
            lambda b, n, pt: (pt[b * n_pages + PAGES_PER_BLOCK * n + half], 0, 0))

    return pl.pallas_call(
        _kmeans_sample_kernel,
        name="kmeans_sample",
        grid_spec=pltpu.PrefetchScalarGridSpec(
            num_scalar_prefetch=1,
            grid=(n_seq, nb),
            in_specs=[page(0), page(1)],
            out_specs=pl.BlockSpec((1, nb, KV_DIM), lambda b, n, pt: (b, 0, 0)),
        ),
        out_shape=jax.ShapeDtypeStruct((n_seq, nb, KV_DIM), F32),
        compiler_params=_params("arbitrary", "arbitrary"),
    )(page_table_flat, cache_k, cache_k)


def _gate_sample_kernel(q_ref, km_ref, sel_ref):
    km = km_ref[...]
    n_seq, nb, _ = km.shape
    blk = lax.broadcasted_iota(jnp.int32, (n_seq, nb, 1), 1).astype(F32)
    lane = lax.broadcasted_iota(jnp.int32, (n_seq, 1, HEAD_DIM), 2)
    for h in range(N_HEADS):
        kvh = h // GROUP
        qh = q_ref[:, :, h * HEAD_DIM:(h + 1) * HEAD_DIM]
        g = jnp.sum(km[:, :, kvh * HEAD_DIM:(kvh + 1) * HEAD_DIM] * qh, axis=-1, keepdims=True)
        out = jnp.zeros((n_seq, 1, HEAD_DIM), F32)
        for r in range(MOBA_TOPK):
            mx = jnp.max(g, axis=1, keepdims=True)
            idx = jnp.min(jnp.where(g == mx, blk, float(nb)), axis=1, keepdims=True)
            g = jnp.where(blk == idx, NEG, g)
            out = jnp.where(lane == r, idx, out)
        sel_ref[h] = out.astype(jnp.int32)


def _gate_sample(q3, km):
    n_seq = q3.shape[0]
    return pl.pallas_call(
        _gate_sample_kernel,
        name="gate_sample",
        out_shape=jax.ShapeDtypeStruct((N_HEADS, n_seq, 1, HEAD_DIM), jnp.int32),
        compiler_params=pltpu.CompilerParams(vmem_limit_bytes=VMEM_LIMIT),
    )(q3, km)


def _attn_sample_kernel(pt_ref, sel_ref, q_ref, kn_ref, vn_ref, ck_ref, cv_ref, o_ref,
                        kbuf, vbuf, sem, *, n_pages):
    b = pl.program_id(0)
    n_slots = MOBA_TOPK * PAGES_PER_BLOCK

    def copies(h):
        kvh = h // GROUP
        out = []
        for r in range(MOBA_TOPK):
            blk = sel_ref[(b * N_HEADS + h) * MOBA_TOPK + r]
            for half in range(PAGES_PER_BLOCK):
                page = pt_ref[b * n_pages + PAGES_PER_BLOCK * blk + half]
                slot = r * PAGES_PER_BLOCK + half
                lanes = pl.ds(kvh * HEAD_DIM, HEAD_DIM)
                out.append(pltpu.make_async_copy(ck_ref.at[page, :, lanes], kbuf.at[h, slot], sem.at[0]))
                out.append(pltpu.make_async_copy(cv_ref.at[page, :, lanes], vbuf.at[h, slot], sem.at[1]))
        return out

    all_copies = [cp for h in range(N_HEADS) for cp in copies(h)]
    for cp in all_copies:
        cp.start()
    for cp in all_copies:
        cp.wait()

    for h in range(N_HEADS):
        kvh = h // GROUP
        qh = q_ref[0, :, h * HEAD_DIM:(h + 1) * HEAD_DIM] * SM_SCALE
        kn = kn_ref[0, :, kvh * HEAD_DIM:(kvh + 1) * HEAD_DIM]
        vn = vn_ref[0, :, kvh * HEAD_DIM:(kvh + 1) * HEAD_DIM]
        kk = kbuf[h].reshape(n_slots * PAGE_SIZE, HEAD_DIM)
        vv = vbuf[h].reshape(n_slots * PAGE_SIZE, HEAD_DIM)
        s = jnp.sum(kk * qh, axis=-1, keepdims=True)
        s_new = jnp.sum(kn * qh, axis=-1, keepdims=True)
        m = jnp.maximum(jnp.max(s, axis=0, keepdims=True), s_new)
        p = jnp.exp(s - m)
        p_new = jnp.exp(s_new - m)
        l = jnp.sum(p, axis=0, keepdims=True) + p_new
        acc = jnp.sum(p * vv, axis=0, keepdims=True) + p_new * vn
        o_ref[0, :, h * HEAD_DIM:(h + 1) * HEAD_DIM] = acc / l


def _attn_sample(page_table_flat, sel_flat, q3, k_new3, v_new3, cache_k, cache_v, n_pages):
    n_seq = q3.shape[0]
    n_slots = MOBA_TOPK * PAGES_PER_BLOCK
    row = lambda w: pl.BlockSpec((1, 1, w), lambda b, pt, sel: (b, 0, 0))
    return pl.pallas_call(
        functools.partial(_attn_sample_kernel, n_pages=n_pages),
        name="attn_sample",
        grid_spec=pltpu.PrefetchScalarGridSpec(
            num_scalar_prefetch=2,
            grid=(n_seq,),
            in_specs=[row(D_MODEL), row(KV_DIM), row(KV_DIM),
                      pl.BlockSpec(memory_space=pl.ANY), pl.BlockSpec(memory_space=pl.ANY)],
            out_specs=row(D_MODEL),
            scratch_shapes=[
                pltpu.VMEM((N_HEADS, n_slots, PAGE_SIZE, HEAD_DIM), F32),
                pltpu.VMEM((N_HEADS, n_slots, PAGE_SIZE, HEAD_DIM), F32),
                pltpu.SemaphoreType.DMA((2,)),
            ],
        ),
        out_shape=jax.ShapeDtypeStruct((n_seq, 1, D_MODEL), F32),
        compiler_params=_params("arbitrary"),
    )(page_table_flat, sel_flat, q3, k_new3, v_new3, cache_k, cache_v)


def _oproj_sample_kernel(x_ref, a_ref, w_ref, o_ref):
    o_ref[...] = x_ref[...] + _dot_hi(a_ref[...], w_ref[...])


def _oproj_sample(x, attn, w_o):
    return pl.pallas_call(
        _oproj_sample_kernel,
        name="oproj_sample",
        out_shape=jax.ShapeDtypeStruct(x.shape, F32),
        compiler_params=pltpu.CompilerParams(vmem_limit_bytes=VMEM_LIMIT),
    )(x, attn, w_o)


def _prompt_trunk(x, w):
    bsz, t, d = x.shape
    c, s_lo, s_hi = _rope_tables(jnp.arange(t, dtype=jnp.int32))
    row = lambda a: a.reshape(1, -1)
    bf = lambda a: a.astype(BF16)
    convs, hs = [], []
    n_rg = w["w_rg_in"].shape[0]
    depth = w["g_mix"].shape[0]

    def ffn(x, l):
        y = _ffn(x.reshape(bsz * t, d), row(w["g_ffn"][l]), bf(w["w_ffn_in"][l]), bf(w["w_ffn_out"][l]),
                 row(w["g_final"]), final=(l == depth - 1), tile=FFN_TILE)
        return y.reshape(bsz, t, d)

    for l in range(n_rg):
        wg = jnp.concatenate([w["rg_gate_a_w"][l], w["rg_gate_i_w"][l]], axis=-1)
        x, conv, h = _rg_prompt(
            x, row(w["g_mix"][l]), bf(w["w_rg_in"][l]), w["rg_conv_w"][l], row(w["rg_conv_b"][l]),
            bf(wg), row(w["rg_gate_a_b"][l]), row(w["rg_gate_i_b"][l]), row(w["rg_lambda"][l]),
            bf(w["w_rg_out"][l]))
        convs.append(conv)
        hs.append(h[:, 0, :])
        x = ffn(x, l)

    k, v, kb, vt, km = _kv_prompt(x, row(w["g_kv"]), bf(w["w_kv"]), c, s_lo, s_hi)
    nb = t // MOBA_BLOCK
    kb = kb.reshape(bsz, nb, MOBA_BLOCK, KV_DIM)
    km = km.reshape(bsz, nb, KV_DIM)
    for a in range(depth - n_rg):
        l = n_rg + a
        x = _attn_prompt(x, row(w["g_mix"][l]), bf(w["w_q"][a]), bf(w["w_o"][a]), c, s_lo, s_hi, kb, vt, km)
        x = ffn(x, l)
    shape4 = (bsz, t, N_KV_HEADS, HEAD_DIM)
    return x, jnp.stack(convs), jnp.stack(hs), k.reshape(shape4), v.reshape(shape4)


def _sample_trunk(x3, state_conv, state_h, cache_k, cache_v, page_table, w):
    n_seq, _, d = x3.shape
    n_pages = page_table.shape[1]
    past_len = n_pages * PAGE_SIZE
    c, s_lo, s_hi = _rope_tables(jnp.full((1,), past_len, jnp.int32))
    row = lambda a: a.reshape(1, -1)
    n_rg = w["w_rg_in"].shape[0]
    depth = w["g_mix"].shape[0]
    x = x3.reshape(n_seq, d)
    pt_flat = page_table.reshape(-1)
    ck = cache_k.reshape(cache_k.shape[0], PAGE_SIZE, KV_DIM)
    cv = cache_v.reshape(cache_v.shape[0], PAGE_SIZE, KV_DIM)
    convs, hs = [], []

    def ffn(x, l):
        return _ffn(x, row(w["g_ffn"][l]), w["w_ffn_in"][l], w["w_ffn_out"][l], row(w["g_final"]),
                    final=(l == depth - 1), tile=n_seq)

    for l in range(n_rg):
        x, conv, h = _rg_sample(
            x, jnp.swapaxes(state_conv[l], 0, 1), state_h[l], row(w["g_mix"][l]), w["w_rg_in"][l],
            w["rg_conv_w"][l], row(w["rg_conv_b"][l]), w["rg_gate_a_w"][l], w["rg_gate_i_w"][l],
            row(w["rg_gate_a_b"][l]), row(w["rg_gate_i_b"][l]), row(w["rg_lambda"][l]), w["w_rg_out"][l])
        convs.append(jnp.swapaxes(conv, 0, 1))
        hs.append(h)
        x = ffn(x, l)

    kv = _proj_sample(x, row(w["g_kv"]), w["w_kv"], c, s_lo, s_hi, N_KV_HEADS)
    k_new, v_new = kv[:, :KV_DIM], kv[:, KV_DIM:]
    km = _kmeans_sample(pt_flat, ck, n_seq, n_pages)
    for a in range(depth - n_rg):
        l = n_rg + a
        q = _proj_sample(x, row(w["g_mix"][l]), w["w_q"][a], c, s_lo, s_hi, N_HEADS)
        q3 = q.reshape(n_seq, 1, d)
        sel = _gate_sample(q3, km)[:, :, 0, :MOBA_TOPK]
        sel_flat = jnp.transpose(sel, (1, 0, 2)).reshape(-1)
        attn = _attn_sample(pt_flat, sel_flat, q3, k_new.reshape(n_seq, 1, KV_DIM),
                            v_new.reshape(n_seq, 1, KV_DIM), ck, cv, n_pages)
        x = _oproj_sample(x, attn.reshape(n_seq, d), w["w_o"][a])
        x = ffn(x, l)
    shape4 = (n_seq, 1, N_KV_HEADS, HEAD_DIM)
    return (x.reshape(n_seq, 1, d), jnp.stack(convs), jnp.stack(hs),
            k_new.reshape(shape4), v_new.reshape(shape4))


def kernel(x_prompt, x_sample, cache_k, cache_v, state_conv, state_h, page_table,
           g_mix, g_ffn, w_rg_in, rg_conv_w, rg_conv_b, rg_gate_a_w, rg_gate_a_b,
           rg_gate_i_w, rg_gate_i_b, rg_lambda, w_rg_out, g_kv, w_kv, w_q, w_o,
           w_ffn_in, w_ffn_out, g_final):
    assert x_sample.shape[1] == 1 and x_prompt.shape[1] % MOBA_BLOCK == 0
    assert MOBA_TOPK <= page_table.shape[1] // PAGES_PER_BLOCK
    w = dict(g_mix=g_mix, g_ffn=g_ffn, w_rg_in=w_rg_in, rg_conv_w=rg_conv_w, rg_conv_b=rg_conv_b,
             rg_gate_a_w=rg_gate_a_w, rg_gate_a_b=rg_gate_a_b, rg_gate_i_w=rg_gate_i_w,
             rg_gate_i_b=rg_gate_i_b, rg_lambda=rg_lambda, w_rg_out=w_rg_out, g_kv=g_kv, w_kv=w_kv,
             w_q=w_q, w_o=w_o, w_ffn_in=w_ffn_in, w_ffn_out=w_ffn_out, g_final=g_final)
    y_p, p_conv, p_h, p_k, p_v = _prompt_trunk(x_prompt, w)
    y_s, s_conv, s_h, s_k, s_v = _sample_trunk(x_sample, state_conv, state_h, cache_k, cache_v,
                                               page_table, w)
    return (y_p, y_s, p_conv, p_h, p_k, p_v, s_conv, s_h, s_k, s_v)
```

```python
import functools
import math

import jax
import jax.numpy as jnp
from jax import lax
from jax.experimental import pallas as pl
from jax.experimental.pallas import tpu as pltpu

D_MODEL = 1024
LRU_WIDTH = D_MODEL
N_GATE_BLOCKS = 8
GATE_BLOCK = LRU_WIDTH // N_GATE_BLOCKS
CONV_WIDTH = 4
LRU_C = 8.0
N_HEADS = 8
HEAD_DIM = D_MODEL // N_HEADS
N_KV_HEADS = 4
KV_DIM = N_KV_HEADS * HEAD_DIM
GROUP = N_HEADS // N_KV_HEADS
ROT_DIM = HEAD_DIM // 4
ROPE_THETA = 500000.0
MOBA_BLOCK = 256
MOBA_TOPK = 3
PAGE_SIZE = 128
PAGES_PER_BLOCK = MOBA_BLOCK // PAGE_SIZE
D_FF = -(-8 * D_MODEL // (3 * 256)) * 256
EPS = 1e-6
NEG = -1e30
SM_SCALE = 1.0 / math.sqrt(HEAD_DIM)
LOG2E = math.log2(math.e)

F32 = jnp.float32
BF16 = jnp.bfloat16
HIGHEST = lax.Precision.HIGHEST

SUBLANES = 8
FF_CHUNK = 256
RG_TILE = 256
FFN_TILE = 512
KV_TILE = 512
KMEANS_PAGES = 16
ROW_PARITIES = SUBLANES // N_KV_HEADS
PAGE_VIEW = (PAGE_SIZE // ROW_PARITIES, SUBLANES, HEAD_DIM)
VMEM_LIMIT = 56 * 1024 * 1024

NT_DIMS = (((1,), (1,)), ((), ()))


def _params(*sem):
    return pltpu.CompilerParams(dimension_semantics=sem, vmem_limit_bytes=VMEM_LIMIT)


def _resident(shape):
    zeros = (0,) * len(shape)
    return pl.BlockSpec(shape, lambda *_: zeros, pipeline_mode=pl.Buffered(1))


def _whole(shape):
    zeros = (0,) * len(shape)
    return pl.BlockSpec(shape, lambda *_: zeros)


def _layer(layer, shape):
    zeros = (0,) * len(shape)
    return pl.BlockSpec((None,) + shape, lambda *_: (layer,) + zeros, pipeline_mode=pl.Buffered(1))


def _rms(x, g):
    ms = jnp.mean(x * x, axis=-1, keepdims=True)
    return x * lax.rsqrt(ms + EPS) * g


def _softplus(z):
    return jnp.maximum(z, 0.0) + jnp.log1p(jnp.exp(-jnp.abs(z)))


def _rope(x, c, s_lo, s_hi):
    n = x.shape[-1]
    half = ROT_DIM // 2
    return x * c + pltpu.roll(x, half, 1) * s_hi + pltpu.roll(x, n - half, 1) * s_lo


def _rope_tables(pos):
    half = ROT_DIM // 2
    inv = ROPE_THETA ** (-jnp.arange(half, dtype=F32) / half)
    ang = pos.astype(F32)[:, None] * inv[None, :]
    cos, sin = jnp.cos(ang), jnp.sin(ang)
    t = pos.shape[0]
    rest = HEAD_DIM - ROT_DIM
    c = jnp.concatenate([cos, cos, jnp.ones((t, rest), F32)], axis=1)
    s_hi = jnp.concatenate([jnp.zeros((t, half), F32), sin, jnp.zeros((t, rest), F32)], axis=1)
    s_lo = jnp.concatenate([-sin, jnp.zeros((t, half + rest), F32)], axis=1)
    return c, s_lo, s_hi


def _lru_coeffs(z_a, z_i, uc, sp):
    r = jax.nn.sigmoid(z_a)
    i = jax.nn.sigmoid(z_i)
    log_a = (-LRU_C * r) * sp
    a = jnp.exp(log_a)
    b = jnp.sqrt(-jnp.tanh(log_a) * (a * a + 1.0)) * (i * uc)
    return a, b


def _rg_prompt_kernel(x_ref, g_ref, win_ref, cw_ref, cb_ref, wg_ref, ba_ref, bi_ref, lam_ref,
                      wout_ref, xo_ref, conv_ref, hl_ref, ext_ref, a_ref, b_ref, gg_ref, hc_ref):
    tt = RG_TILE
    t = pl.program_id(1)

    @pl.when(t == 0)
    def _():
        ext_ref[0:SUBLANES, :] = jnp.zeros((SUBLANES, LRU_WIDTH), F32)
        hc_ref[...] = jnp.zeros_like(hc_ref)

    x = x_ref[0]
    xn = _rms(x, g_ref[...]).astype(BF16)
    proj = jnp.dot(xn, win_ref[...], preferred_element_type=F32)
    gg_ref[...] = jax.nn.gelu(proj[:, :LRU_WIDTH])
    ext_ref[SUBLANES:SUBLANES + tt, :] = proj[:, LRU_WIDTH:]

    base = SUBLANES - (CONV_WIDTH - 1)
    uc = cb_ref[...] + ext_ref[base:base + tt, :] * cw_ref[0:1, :]
    for j in range(1, CONV_WIDTH):
        uc = uc + ext_ref[base + j:base + j + tt, :] * cw_ref[j:j + 1, :]
    conv_ref[0] = ext_ref[tt + base:tt + SUBLANES, :]
    ext_ref[0:SUBLANES, :] = ext_ref[tt:tt + SUBLANES, :]

    sp = _softplus(-lam_ref[...])
    for n in range(N_GATE_BLOCKS):
        sl = slice(n * GATE_BLOCK, (n + 1) * GATE_BLOCK)
        ucn = uc[:, sl]
        z = jnp.dot(ucn.astype(BF16), wg_ref[n], preferred_element_type=F32)
        a, b = _lru_coeffs(z[:, :GATE_BLOCK] + ba_ref[:, sl], z[:, GATE_BLOCK:] + bi_ref[:, sl],
                           ucn, sp[:, sl])
        a_ref[:, sl] = a
        b_ref[:, sl] = b

    row = lax.broadcasted_iota(jnp.int32, (SUBLANES, LRU_WIDTH), 0)

    def group(gi, h_prev):
        r0 = pl.multiple_of(gi * SUBLANES, SUBLANES)
        a8 = a_ref[pl.ds(r0, SUBLANES), :]
        b8 = b_ref[pl.ds(r0, SUBLANES), :]
        s = 1
        while s < SUBLANES:
            keep = row >= s
            b8 = jnp.where(keep, a8 * pltpu.roll(b8, s, 0) + b8, b8)
            a8 = jnp.where(keep, a8 * pltpu.roll(a8, s, 0), a8)
            s *= 2
        h = a8 * h_prev + b8
        b_ref[pl.ds(r0, SUBLANES), :] = h
        return h[SUBLANES - 1:SUBLANES, :]

    h_last = lax.fori_loop(0, tt // SUBLANES, group, hc_ref[...])
    hc_ref[...] = h_last
    hl_ref[0] = h_last

    y = jnp.dot((b_ref[...] * gg_ref[...]).astype(BF16), wout_ref[...], preferred_element_type=F32)
    xo_ref[0] = x + y


def _rg_prompt(x, g, w_in, conv_w, conv_b, wg, ba, bi, lam, w_out):
    bsz, t, d = x.shape
    tt = RG_TILE
    vec = lambda n: _resident((1, n))
    return pl.pallas_call(
        _rg_prompt_kernel,
        name="rg_prompt",
        grid=(bsz, t // tt),
        in_specs=[
            pl.BlockSpec((1, tt, d), lambda b, i: (b, i, 0)),
            vec(d),
            _resident((d, 2 * LRU_WIDTH)),
            _resident((CONV_WIDTH, LRU_WIDTH)),
            vec(LRU_WIDTH),
            _resident((N_GATE_BLOCKS, GATE_BLOCK, 2 * GATE_BLOCK)),
            vec(LRU_WIDTH), vec(LRU_WIDTH), vec(LRU_WIDTH),
            _resident((LRU_WIDTH, d)),
        ],
        out_specs=[
            pl.BlockSpec((1, tt, d), lambda b, i: (b, i, 0)),
            pl.BlockSpec((1, CONV_WIDTH - 1, LRU_WIDTH), lambda b, i: (b, 0, 0)),
            pl.BlockSpec((1, 1, LRU_WIDTH), lambda b, i: (b, 0, 0)),
        ],
        out_shape=[
            jax.ShapeDtypeStruct((bsz, t, d), F32),
            jax.ShapeDtypeStruct((bsz, CONV_WIDTH - 1, LRU_WIDTH), F32),
            jax.ShapeDtypeStruct((bsz, 1, LRU_WIDTH), F32),
        ],
        scratch_shapes=[
            pltpu.VMEM((tt + SUBLANES, LRU_WIDTH), F32),
            pltpu.VMEM((tt, LRU_WIDTH), F32),
            pltpu.VMEM((tt, LRU_WIDTH), F32),
            pltpu.VMEM((tt, LRU_WIDTH), F32),
            pltpu.VMEM((1, LRU_WIDTH), F32),
        ],
        compiler_params=_params("arbitrary", "arbitrary"),
    )(x, g, w_in, conv_w, conv_b, wg, ba, bi, lam, w_out)


def _ffn_kernel(x_ref, g_ref, win_ref, wout_ref, gf_ref, o_ref, act_ref, *, final, precision):
    cdt = win_ref.dtype
    x = x_ref[...]
    xn = _rms(x, g_ref[...]).astype(cdt)
    for c in range(D_FF // FF_CHUNK):
        lo = c * FF_CHUNK
        gate = jnp.dot(xn, win_ref[:, lo:lo + FF_CHUNK], preferred_element_type=F32,
                       precision=precision)
        up = jnp.dot(xn, win_ref[:, D_FF + lo:D_FF + lo + FF_CHUNK], preferred_element_type=F32,
                     precision=precision)
        act_ref[:, lo:lo + FF_CHUNK] = (jax.nn.silu(gate) * up).astype(cdt)
    y = x + jnp.dot(act_ref[...], wout_ref[...], preferred_element_type=F32, precision=precision)
    if final:
        y = _rms(y, gf_ref[...])
    o_ref[...] = y


def _ffn(x, g, w_in, w_out, g_final, layer, *, final, tile):
    n, d = x.shape
    precision = HIGHEST if w_in.dtype == F32 else None
    return pl.pallas_call(
        functools.partial(_ffn_kernel, final=final, precision=precision),
        name="ffn",
        grid=(n // tile,),
        in_specs=[
            pl.BlockSpec((tile, d), lambda i: (i, 0)),
            _resident((1, d)),
            _layer(layer, (d, 2 * D_FF)),
            _layer(layer, (D_FF, d)),
            _resident((1, d)),
        ],
        out_specs=pl.BlockSpec((tile, d), lambda i: (i, 0)),
        out_shape=jax.ShapeDtypeStruct((n, d), F32),
        scratch_shapes=[pltpu.VMEM((tile, D_FF), w_in.dtype)],
        compiler_params=_params("arbitrary"),
    )(x, g, w_in, w_out, g_final)


def _kv_prompt_kernel(x_ref, g_ref, w_ref, c_ref, slo_ref, shi_ref,
                      k_ref, v_ref, kb_ref, vt_ref, km_ref):
    xn = _rms(x_ref[0], g_ref[...]).astype(BF16)
    kv = jnp.dot(xn, w_ref[...], preferred_element_type=F32)
    tile4 = lambda r: jnp.concatenate([r[...]] * N_KV_HEADS, axis=1)
    k = _rope(kv[:, :KV_DIM], tile4(c_ref), tile4(slo_ref), tile4(shi_ref))
    v = kv[:, KV_DIM:]
    k_ref[0] = k
    v_ref[0] = v
    kb_ref[0] = k.astype(BF16)
    for bb in range(KV_TILE // MOBA_BLOCK):
        rows = slice(bb * MOBA_BLOCK, (bb + 1) * MOBA_BLOCK)
        vt_ref[0, bb] = v[rows, :].T.astype(BF16)
        km_ref[0, bb] = jnp.sum(k[rows, :], axis=0, keepdims=True) * (1.0 / MOBA_BLOCK)


def _kv_prompt(x, g, w_kv, c, s_lo, s_hi):
    bsz, t, d = x.shape
    tt = KV_TILE
    nbt = tt // MOBA_BLOCK
    nb = t // MOBA_BLOCK
    tab = pl.BlockSpec((tt, HEAD_DIM), lambda b, i: (i, 0))
    row_spec = pl.BlockSpec((1, tt, KV_DIM), lambda b, i: (b, i, 0))
    return pl.pallas_call(
        _kv_prompt_kernel,
        name="kv_prompt",
        grid=(bsz, t // tt),
        in_specs=[
            pl.BlockSpec((1, tt, d), lambda b, i: (b, i, 0)),
            _resident((1, d)),
            _resident((d, 2 * KV_DIM)),
            tab, tab, tab,
        ],
        out_specs=[
            row_spec, row_spec, row_spec,
            pl.BlockSpec((1, nbt, KV_DIM, MOBA_BLOCK), lambda b, i: (b, i, 0, 0)),
            pl.BlockSpec((1, nbt, 1, KV_DIM), lambda b, i: (b, i, 0, 0)),
        ],
        out_shape=[
            jax.ShapeDtypeStruct((bsz, t, KV_DIM), F32),
            jax.ShapeDtypeStruct((bsz, t, KV_DIM), F32),
            jax.ShapeDtypeStruct((bsz, t, KV_DIM), BF16),
            jax.ShapeDtypeStruct((bsz, nb, KV_DIM, MOBA_BLOCK), BF16),
            jax.ShapeDtypeStruct((bsz, nb, 1, KV_DIM), F32),
        ],
        compiler_params=_params("arbitrary", "arbitrary"),
    )(x, g, w_kv, c, s_lo, s_hi)


def _select_bias(gate, n_allowed, n_valid):
    nb, nq = gate.shape
    rowi = lax.broadcasted_iota(jnp.int32, (nb, nq), 0).astype(F32)
    g = jnp.where(rowi < n_allowed, gate, NEG)
    bias = jnp.full((nb, nq), NEG, F32)
    for r in range(MOBA_TOPK):
        mx = jnp.max(g, axis=0, keepdims=True)
        idx = jnp.min(jnp.where(g == mx, rowi, float(nb)), axis=0, keepdims=True)
        ok = (n_valid > r).astype(F32)
        idx = idx * ok - (1.0 - ok)
        pick = rowi == idx
        bias = jnp.where(pick, 0.0, bias)
        g = jnp.where(pick, NEG, g)
    return bias


def _attn_prompt_kernel(x_ref, g_ref, wq_ref, wo_ref, c_ref, slo_ref, shi_ref, k_ref, vt_ref,
                        km_ref, o_ref, q2s_ref, bias_ref, m_ref, l_ref, acc_ref, attn_ref):
    tq = MOBA_BLOCK
    nq = GROUP * tq
    i = pl.program_id(1)
    x = x_ref[0]
    xn = _rms(x, g_ref[...]).astype(BF16)
    q = jnp.dot(xn, wq_ref[...], preferred_element_type=F32)
    c, s_lo, s_hi = c_ref[...], slo_ref[...], shi_ref[...]
    own = i.astype(F32)

    key_pos = lax.broadcasted_iota(jnp.int32, (tq, nq), 0)
    qry_pos = lax.broadcasted_iota(jnp.int32, (tq, nq), 1) % tq
    causal = key_pos <= qry_pos
    kv_lanes = [slice(kvh * HEAD_DIM, (kvh + 1) * HEAD_DIM) for kvh in range(N_KV_HEADS)]

    for kvh, lanes in enumerate(kv_lanes):
        q2 = jnp.concatenate(
            [_rope(q[:, h * HEAD_DIM:(h + 1) * HEAD_DIM], c, s_lo, s_hi)
             for h in range(kvh * GROUP, (kvh + 1) * GROUP)], axis=0)
        gate = lax.dot_general(km_ref[0, :, lanes], q2, NT_DIMS, precision=HIGHEST,
                               preferred_element_type=F32)
        bias_ref[kvh] = _select_bias(gate, own, own)
        q2s = (q2 * (SM_SCALE * LOG2E)).astype(BF16)
        q2s_ref[kvh] = q2s

        s = lax.dot_general(k_ref[0, i, :, lanes], q2s, NT_DIMS, preferred_element_type=F32)
        s = jnp.where(causal, s, NEG)
        m = jnp.max(s, axis=0, keepdims=True)
        p = jnp.exp2(s - m)
        m_ref[kvh] = m
        l_ref[kvh] = jnp.sum(p, axis=0, keepdims=True)
        acc_ref[kvh] = jnp.dot(vt_ref[0, i, lanes, :], p.astype(BF16), preferred_element_type=F32)

    def past_block(j, carry):
        for kvh, lanes in enumerate(kv_lanes):
            s = lax.dot_general(k_ref[0, j, :, lanes], q2s_ref[kvh], NT_DIMS,
                                preferred_element_type=F32)
            s = s + bias_ref[kvh, pl.ds(j, 1), :]
            m = m_ref[kvh]
            m_new = jnp.maximum(m, jnp.max(s, axis=0, keepdims=True))
            alpha = jnp.exp2(m - m_new)
            p = jnp.exp2(s - m_new)
            m_ref[kvh] = m_new
            l_ref[kvh] = alpha * l_ref[kvh] + jnp.sum(p, axis=0, keepdims=True)
            acc_ref[kvh] = alpha * acc_ref[kvh] + jnp.dot(
                vt_ref[0, j, lanes, :], p.astype(BF16), preferred_element_type=F32)
        return carry

    lax.fori_loop(0, i, past_block, 0)

    for kvh in range(N_KV_HEADS):
        o = (acc_ref[kvh] / l_ref[kvh]).T
        for j in range(GROUP):
            h = kvh * GROUP + j
            attn_ref[:, h * HEAD_DIM:(h + 1) * HEAD_DIM] = o[j * tq:(j + 1) * tq, :].astype(BF16)

    o_ref[0] = x + jnp.dot(attn_ref[...], wo_ref[...], preferred_element_type=F32)


def _attn_prompt(x, g, w_q, w_o, c, s_lo, s_hi, kb, vt, km):
    bsz, t, d = x.shape
    tq = MOBA_BLOCK
    nb = t // tq
    tab = pl.BlockSpec((tq, HEAD_DIM), lambda b, i: (i, 0))
    per_seq = lambda shape: pl.BlockSpec((1,) + shape, lambda b, i: (b,) + (0,) * len(shape),
                                         pipeline_mode=pl.Buffered(1))
    return pl.pallas_call(
        _attn_prompt_kernel,
        name="attn_prompt",
        grid=(bsz, nb),
        in_specs=[
            pl.BlockSpec((1, tq, d), lambda b, i: (b, i, 0)),
            _resident((1, d)),
            _resident((d, d)),
            _resident((d, d)),
            tab, tab, tab,
            per_seq((nb, tq, KV_DIM)),
            per_seq((nb, KV_DIM, tq)),
            per_seq((nb, KV_DIM)),
        ],
        out_specs=pl.BlockSpec((1, tq, d), lambda b, i: (b, i, 0)),
        out_shape=jax.ShapeDtypeStruct((bsz, t, d), F32),
        scratch_shapes=[
            pltpu.VMEM((N_KV_HEADS, GROUP * tq, HEAD_DIM), BF16),
            pltpu.VMEM((N_KV_HEADS, nb, GROUP * tq), F32),
            pltpu.VMEM((N_KV_HEADS, 1, GROUP * tq), F32),
            pltpu.VMEM((N_KV_HEADS, 1, GROUP * tq), F32),
            pltpu.VMEM((N_KV_HEADS, HEAD_DIM, GROUP * tq), F32),
            pltpu.VMEM((tq, d), BF16),
        ],
        compiler_params=_params("arbitrary", "arbitrary"),
    )(x, g, w_q, w_o, c, s_lo, s_hi, kb, vt, km)


def _dot_hi(a, b):
    return jnp.dot(a, b, preferred_element_type=F32, precision=HIGHEST)


def _rg_sample_kernel(x_ref, cs_ref, h0_ref, g_ref, win_ref, cw_ref, cb_ref, wa_ref, wi_ref,
                      ba_ref, bi_ref, lam_ref, wout_ref, xo_ref, cso_ref, ho_ref):
    x = x_ref[...]
    proj = _dot_hi(_rms(x, g_ref[...]), win_ref[...])
    gate, u = proj[:, :LRU_WIDTH], proj[:, LRU_WIDTH:]
    uc = cb_ref[...] + cs_ref[0] * cw_ref[0:1, :]
    for j in range(1, CONV_WIDTH - 1):
        uc = uc + cs_ref[j] * cw_ref[j:j + 1, :]
    uc = uc + u * cw_ref[CONV_WIDTH - 1:CONV_WIDTH, :]
    for j in range(CONV_WIDTH - 2):
        cso_ref[j] = cs_ref[j + 1]
    cso_ref[CONV_WIDTH - 2] = u

    sp = _softplus(-lam_ref[...])
    h0 = h0_ref[...]
    hs = []
    for n in range(N_GATE_BLOCKS):
        sl = slice(n * GATE_BLOCK, (n + 1) * GATE_BLOCK)
        ucn = uc[:, sl]
        a, b = _lru_coeffs(_dot_hi(ucn, wa_ref[n]) + ba_ref[:, sl],
                           _dot_hi(ucn, wi_ref[n]) + bi_ref[:, sl], ucn, sp[:, sl])
        hs.append(b + a * h0[:, sl])
    h = jnp.concatenate(hs, axis=1)
    ho_ref[...] = h
    xo_ref[...] = x + _dot_hi(h * jax.nn.gelu(gate), wout_ref[...])


def _rg_sample(x, cs, h0, g, w_in, conv_w, conv_b, wa, wi, ba, bi, lam, w_out, layer):
    n, d = x.shape
    vec = _resident((1, LRU_WIDTH))
    gates = _layer(layer, (N_GATE_BLOCKS, GATE_BLOCK, GATE_BLOCK))
    return pl.pallas_call(
        _rg_sample_kernel,
        name="rg_sample",
        grid=(1,),
        in_specs=[
            _resident((n, d)), _resident((CONV_WIDTH - 1, n, LRU_WIDTH)), _resident((n, LRU_WIDTH)),
            _resident((1, d)), _layer(layer, (d, 2 * LRU_WIDTH)), _resident((CONV_WIDTH, LRU_WIDTH)),
            vec, gates, gates, vec, vec, vec, _layer(layer, (LRU_WIDTH, d)),
        ],
        out_specs=[_whole((n, d)), _whole((CONV_WIDTH - 1, n, LRU_WIDTH)), _whole((n, LRU_WIDTH))],
        out_shape=[
            jax.ShapeDtypeStruct((n, d), F32),
            jax.ShapeDtypeStruct((CONV_WIDTH - 1, n, LRU_WIDTH), F32),
            jax.ShapeDtypeStruct((n, LRU_WIDTH), F32),
        ],
        compiler_params=_params("arbitrary"),
    )(x, cs, h0, g, w_in, conv_w, conv_b, wa, wi, ba, bi, lam, w_out)


def _proj_sample_kernel(x_ref, g_ref, w_ref, c_ref, slo_ref, shi_ref, o_ref, *, n_rope_heads):
    y = _dot_hi(_rms(x_ref[...], g_ref[...]), w_ref[...])
    if n_rope_heads:
        tile = lambda r: jnp.concatenate([r[...]] * n_rope_heads, axis=1)
        width = n_rope_heads * HEAD_DIM
        rot = _rope(y[:, :width], tile(c_ref), tile(slo_ref), tile(shi_ref))
        y = jnp.concatenate([rot, y[:, width:]], axis=1) if width < y.shape[1] else rot
    o_ref[...] = y


def _proj_sample(x, g, w, layer, c, s_lo, s_hi, n_rope_heads):
    n, d = x.shape
    width = w.shape[2]
    tab = _resident((1, HEAD_DIM))
    return pl.pallas_call(
        functools.partial(_proj_sample_kernel, n_rope_heads=n_rope_heads),
        name="proj_sample",
        grid=(1,),
        in_specs=[_resident((n, d)), _resident((1, d)), _layer(layer, (d, width)), tab, tab, tab],
        out_specs=_whole((n, width)),
        out_shape=jax.ShapeDtypeStruct((n, width), F32),
        compiler_params=_params("arbitrary"),
    )(x, g, w, c, s_lo, s_hi)


def _kmeans_sample_kernel(pt_ref, *refs):
    page_refs, o_ref = refs[:-1], refs[-1]
    step = pl.program_id(1)
    blocks = len(page_refs) // PAGES_PER_BLOCK
    means = []
    for blk in range(blocks):
        pages = page_refs[blk * PAGES_PER_BLOCK:(blk + 1) * PAGES_PER_BLOCK]
        s = sum(jnp.sum(p[0], axis=0) for p in pages)
        s = sum(s[par * N_KV_HEADS:(par + 1) * N_KV_HEADS] for par in range(ROW_PARITIES))
        means.append(jnp.concatenate([s[kvh:kvh + 1, :] for kvh in range(N_KV_HEADS)], axis=1))
    first = pl.multiple_of(step * blocks, blocks)
    o_ref[0, pl.ds(first, blocks), :] = jnp.concatenate(means, axis=0) * (1.0 / MOBA_BLOCK)


def _kmeans_sample(page_table_flat, cache_k, n_seq, n_pages):
    nb = n_pages // PAGES_PER_BLOCK

    def page(k):
        return pl.BlockSpec(
            (1,) + PAGE_VIEW,
            lambda b, n, pt: (pt[b * n_pages + KMEANS_PAGES * n + k], 0, 0, 0))

    return pl.pallas_call(
        _kmeans_sample_kernel,
        name="kmeans_sample",
        grid_spec=pltpu.PrefetchScalarGridSpec(
            num_scalar_prefetch=1,
            grid=(n_seq, n_pages // KMEANS_PAGES),
            in_specs=[page(k) for k in range(KMEANS_PAGES)],
            out_specs=pl.BlockSpec((1, nb, KV_DIM), lambda b, n, pt: (b, 0, 0)),
        ),
        out_shape=jax.ShapeDtypeStruct((n_seq, nb, KV_DIM), F32),
        compiler_params=_params("arbitrary", "arbitrary"),
    )(page_table_flat, *([cache_k] * KMEANS_PAGES))


def _gate_sample_kernel(q_ref, km_ref, sel_ref):
    km = km_ref[...]
    n_seq, nb, _ = km.shape
    blk = lax.broadcasted_iota(jnp.int32, (n_seq, nb, 1), 1).astype(F32)
    lane = lax.broadcasted_iota(jnp.int32, (n_seq, 1, HEAD_DIM), 2)
    for h in range(N_HEADS):
        kvh = h // GROUP
        qh = q_ref[:, :, h * HEAD_DIM:(h + 1) * HEAD_DIM]
        g = jnp.sum(km[:, :, kvh * HEAD_DIM:(kvh + 1) * HEAD_DIM] * qh, axis=-1, keepdims=True)
        out = jnp.zeros((n_seq, 1, HEAD_DIM), F32)
        for r in range(MOBA_TOPK):
            mx = jnp.max(g, axis=1, keepdims=True)
            idx = jnp.min(jnp.where(g == mx, blk, float(nb)), axis=1, keepdims=True)
            g = jnp.where(blk == idx, NEG, g)
            out = jnp.where(lane == r, idx, out)
        sel_ref[h] = out.astype(jnp.int32)


def _gate_sample(q3, km):
    n_seq = q3.shape[0]
    return pl.pallas_call(
        _gate_sample_kernel,
        name="gate_sample",
        out_shape=jax.ShapeDtypeStruct((N_HEADS, n_seq, 1, HEAD_DIM), jnp.int32),
        compiler_params=pltpu.CompilerParams(vmem_limit_bytes=VMEM_LIMIT),
    )(q3, km)


def _attn_sample_kernel(pt_ref, sel_ref, q_ref, kn_ref, vn_ref, ck_ref, cv_ref, o_ref,
                        kbuf, vbuf, sem, *, n_pages):
    b = pl.program_id(0)
    n_slots = MOBA_TOPK * PAGES_PER_BLOCK
    rows = PAGE_SIZE // ROW_PARITIES

    def copies(seq, buf):
        out = []
        for h in range(N_HEADS):
            kvh = h // GROUP
            for r in range(MOBA_TOPK):
                blk = sel_ref[(seq * N_HEADS + h) * MOBA_TOPK + r]
                for half in range(PAGES_PER_BLOCK):
                    page = pt_ref[seq * n_pages + PAGES_PER_BLOCK * blk + half]
                    slot = r * PAGES_PER_BLOCK + half
                    for par in range(ROW_PARITIES):
                        sub = par * N_KV_HEADS + kvh
                        dst = (buf, h, slot, pl.ds(par * rows, rows))
                        out.append(pltpu.make_async_copy(ck_ref.at[page, :, sub], kbuf.at[dst], sem.at[0, buf]))
                        out.append(pltpu.make_async_copy(cv_ref.at[page, :, sub], vbuf.at[dst], sem.at[1, buf]))
        return out

    cur = b % 2

    @pl.when(b == 0)
    def _():
        for cp in copies(b, cur):
            cp.start()

    @pl.when(b + 1 < pl.num_programs(0))
    def _():
        for cp in copies(b + 1, 1 - cur):
            cp.start()

    for cp in copies(b, cur):
        cp.wait()

    for h in range(N_HEADS):
        kvh = h // GROUP
        qh = q_ref[0, :, h * HEAD_DIM:(h + 1) * HEAD_DIM] * SM_SCALE
        kn = kn_ref[0, :, kvh * HEAD_DIM:(kvh + 1) * HEAD_DIM]
        vn = vn_ref[0, :, kvh * HEAD_DIM:(kvh + 1) * HEAD_DIM]
        kk = kbuf[cur, h].reshape(n_slots * PAGE_SIZE, HEAD_DIM)
        vv = vbuf[cur, h].reshape(n_slots * PAGE_SIZE, HEAD_DIM)
        s = jnp.sum(kk * qh, axis=-1, keepdims=True)
        s_new = jnp.sum(kn * qh, axis=-1, keepdims=True)
        m = jnp.maximum(jnp.max(s, axis=0, keepdims=True), s_new)
        p = jnp.exp(s - m)
        p_new = jnp.exp(s_new - m)
        l = jnp.sum(p, axis=0, keepdims=True) + p_new
        acc = jnp.sum(p * vv, axis=0, keepdims=True) + p_new * vn
        o_ref[0, :, h * HEAD_DIM:(h + 1) * HEAD_DIM] = acc / l


def _attn_sample(page_table_flat, sel_flat, q3, k_new3, v_new3, cache_k, cache_v, n_pages):
    n_seq = q3.shape[0]
    n_slots = MOBA_TOPK * PAGES_PER_BLOCK
    row = lambda w: pl.BlockSpec((1, 1, w), lambda b, pt, sel: (b, 0, 0))
    return pl.pallas_call(
        functools.partial(_attn_sample_kernel, n_pages=n_pages),
        name="attn_sample",
        grid_spec=pltpu.PrefetchScalarGridSpec(
            num_scalar_prefetch=2,
            grid=(n_seq,),
            in_specs=[row(D_MODEL), row(KV_DIM), row(KV_DIM),
                      pl.BlockSpec(memory_space=pl.ANY), pl.BlockSpec(memory_space=pl.ANY)],
            out_specs=row(D_MODEL),
            scratch_shapes=[
                pltpu.VMEM((2, N_HEADS, n_slots, PAGE_SIZE, HEAD_DIM), F32),
                pltpu.VMEM((2, N_HEADS, n_slots, PAGE_SIZE, HEAD_DIM), F32),
                pltpu.SemaphoreType.DMA((2, 2)),
            ],
        ),
        out_shape=jax.ShapeDtypeStruct((n_seq, 1, D_MODEL), F32),
        compiler_params=_params("arbitrary"),
    )(page_table_flat, sel_flat, q3, k_new3, v_new3, cache_k, cache_v)


def _oproj_sample_kernel(x_ref, a_ref, w_ref, o_ref):
    o_ref[...] = x_ref[...] + _dot_hi(a_ref[...], w_ref[...])


def _oproj_sample(x, attn, w_o, layer):
    n, d = x.shape
    return pl.pallas_call(
        _oproj_sample_kernel,
        name="oproj_sample",
        grid=(1,),
        in_specs=[_resident((n, d)), _resident((n, d)), _layer(layer, (d, d))],
        out_specs=_whole((n, d)),
        out_shape=jax.ShapeDtypeStruct((n, d), F32),
        compiler_params=_params("arbitrary"),
    )(x, attn, w_o)


def _prompt_trunk(x, w):
    bsz, t, d = x.shape
    c, s_lo, s_hi = _rope_tables(jnp.arange(t, dtype=jnp.int32))
    row = lambda a: a.reshape(1, -1)
    bf = lambda a: a.astype(BF16)
    convs, hs = [], []
    n_rg = w["w_rg_in"].shape[0]
    depth = w["g_mix"].shape[0]

    w_ffn_in, w_ffn_out = bf(w["w_ffn_in"]), bf(w["w_ffn_out"])

    def ffn(x, l):
        y = _ffn(x.reshape(bsz * t, d), row(w["g_ffn"][l]), w_ffn_in, w_ffn_out,
                 row(w["g_final"]), l, final=(l == depth - 1), tile=FFN_TILE)
        return y.reshape(bsz, t, d)

    for l in range(n_rg):
        wg = jnp.concatenate([w["rg_gate_a_w"][l], w["rg_gate_i_w"][l]], axis=-1)
        x, conv, h = _rg_prompt(
            x, row(w["g_mix"][l]), bf(w["w_rg_in"][l]), w["rg_conv_w"][l], row(w["rg_conv_b"][l]),
            bf(wg), row(w["rg_gate_a_b"][l]), row(w["rg_gate_i_b"][l]), row(w["rg_lambda"][l]),
            bf(w["w_rg_out"][l]))
        convs.append(conv)
        hs.append(h[:, 0, :])
        x = ffn(x, l)

    k, v, kb, vt, km = _kv_prompt(x, row(w["g_kv"]), bf(w["w_kv"]), c, s_lo, s_hi)
    nb = t // MOBA_BLOCK
    kb = kb.reshape(bsz, nb, MOBA_BLOCK, KV_DIM)
    km = km.reshape(bsz, nb, KV_DIM)
    for a in range(depth - n_rg):
        l = n_rg + a
        x = _attn_prompt(x, row(w["g_mix"][l]), bf(w["w_q"][a]), bf(w["w_o"][a]), c, s_lo, s_hi, kb, vt, km)
        x = ffn(x, l)
    shape4 = (bsz, t, N_KV_HEADS, HEAD_DIM)
    return x, jnp.stack(convs), jnp.stack(hs), k.reshape(shape4), v.reshape(shape4)


def _sample_trunk(x3, state_conv, state_h, cache_k, cache_v, page_table, w):
    n_seq, _, d = x3.shape
    n_pages = page_table.shape[1]
    past_len = n_pages * PAGE_SIZE
    c, s_lo, s_hi = _rope_tables(jnp.full((1,), past_len, jnp.int32))
    row = lambda a: a.reshape(1, -1)
    n_rg = w["w_rg_in"].shape[0]
    depth = w["g_mix"].shape[0]
    x = x3.reshape(n_seq, d)
    pt_flat = page_table.reshape(-1)
    ck = cache_k.reshape((cache_k.shape[0],) + PAGE_VIEW)
    cv = cache_v.reshape((cache_v.shape[0],) + PAGE_VIEW)
    convs, hs = [], []

    def ffn(x, l):
        return _ffn(x, row(w["g_ffn"][l]), w["w_ffn_in"], w["w_ffn_out"], row(w["g_final"]), l,
                    final=(l == depth - 1), tile=n_seq)

    for l in range(n_rg):
        x, conv, h = _rg_sample(
            x, jnp.swapaxes(state_conv[l], 0, 1), state_h[l], row(w["g_mix"][l]), w["w_rg_in"],
            w["rg_conv_w"][l], row(w["rg_conv_b"][l]), w["rg_gate_a_w"], w["rg_gate_i_w"],
            row(w["rg_gate_a_b"][l]), row(w["rg_gate_i_b"][l]), row(w["rg_lambda"][l]), w["w_rg_out"], l)
        convs.append(jnp.swapaxes(conv, 0, 1))
        hs.append(h)
        x = ffn(x, l)

    kv = _proj_sample(x, row(w["g_kv"]), w["w_kv"][None], 0, c, s_lo, s_hi, N_KV_HEADS)
    k_new, v_new = kv[:, :KV_DIM], kv[:, KV_DIM:]
    km = _kmeans_sample(pt_flat, ck, n_seq, n_pages)
    for a in range(depth - n_rg):
        l = n_rg + a
        q = _proj_sample(x, row(w["g_mix"][l]), w["w_q"], a, c, s_lo, s_hi, N_HEADS)
        q3 = q.reshape(n_seq, 1, d)
        sel = _gate_sample(q3, km)[:, :, 0, :MOBA_TOPK]
        sel_flat = jnp.transpose(sel, (1, 0, 2)).reshape(-1)
        attn = _attn_sample(pt_flat, sel_flat, q3, k_new.reshape(n_seq, 1, KV_DIM),
                            v_new.reshape(n_seq, 1, KV_DIM), ck, cv, n_pages)
        x = _oproj_sample(x, attn.reshape(n_seq, d), w["w_o"], a)
        x = ffn(x, l)
    shape4 = (n_seq, 1, N_KV_HEADS, HEAD_DIM)
    return (x.reshape(n_seq, 1, d), jnp.stack(convs), jnp.stack(hs),
            k_new.reshape(shape4), v_new.reshape(shape4))


def kernel(x_prompt, x_sample, cache_k, cache_v, state_conv, state_h, page_table,
           g_mix, g_ffn, w_rg_in, rg_conv_w, rg_conv_b, rg_gate_a_w, rg_gate_a_b,
           rg_gate_i_w, rg_gate_i_b, rg_lambda, w_rg_out, g_kv, w_kv, w_q, w_o,
           w_ffn_in, w_ffn_out, g_final):
    assert x_sample.shape[1] == 1 and x_prompt.shape[1] % MOBA_BLOCK == 0
    assert MOBA_TOPK <= page_table.shape[1] // PAGES_PER_BLOCK
    w = dict(g_mix=g_mix, g_ffn=g_ffn, w_rg_in=w_rg_in, rg_conv_w=rg_conv_w, rg_conv_b=rg_conv_b,
             rg_gate_a_w=rg_gate_a_w, rg_gate_a_b=rg_gate_a_b, rg_gate_i_w=rg_gate_i_w,
             rg_gate_i_b=rg_gate_i_b, rg_lambda=rg_lambda, w_rg_out=w_rg_out, g_kv=g_kv, w_kv=w_kv,
             w_q=w_q, w_o=w_o, w_ffn_in=w_ffn_in, w_ffn_out=w_ffn_out, g_final=g_final)
    y_p, p_conv, p_h, p_k, p_v = _prompt_trunk(x_prompt, w)
    y_s, s_conv, s_h, s_k, s_v = _sample_trunk(x_sample, state_conv, state_h, cache_k, cache_v,
                                               page_table, w)
    return (y_p, y_s, p_conv, p_h, p_k, p_v, s_conv, s_h, s_k, s_v)
```

```python
import functools
import math

import jax
import jax.numpy as jnp
from jax import lax
from jax.experimental import pallas as pl
from jax.experimental.pallas import tpu as pltpu

D_MODEL = 1024
LRU_WIDTH = D_MODEL
N_GATE_BLOCKS = 8
GATE_BLOCK = LRU_WIDTH // N_GATE_BLOCKS
CONV_WIDTH = 4
LRU_C = 8.0
N_HEADS = 8
HEAD_DIM = D_MODEL // N_HEADS
N_KV_HEADS = 4
KV_DIM = N_KV_HEADS * HEAD_DIM
GROUP = N_HEADS // N_KV_HEADS
ROT_DIM = HEAD_DIM // 4
ROPE_THETA = 500000.0
MOBA_BLOCK = 256
MOBA_TOPK = 3
PAGE_SIZE = 128
PAGES_PER_BLOCK = MOBA_BLOCK // PAGE_SIZE
D_FF = -(-8 * D_MODEL // (3 * 256)) * 256
EPS = 1e-6
NEG = -1e30
SM_SCALE = 1.0 / math.sqrt(HEAD_DIM)
LOG2E = math.log2(math.e)

F32 = jnp.float32
BF16 = jnp.bfloat16
HIGHEST = lax.Precision.HIGHEST

SUBLANES = 8
BF16_ROWS = 16
FF_CHUNK = 256
RG_TILE = 256
FFN_TILE = 512
KV_TILE = 512
KMEANS_PAGES = 16
ROW_PARITIES = SUBLANES // N_KV_HEADS
PAGE_VIEW = (PAGE_SIZE // ROW_PARITIES, SUBLANES, HEAD_DIM)
VMEM_LIMIT = 56 * 1024 * 1024

NT_DIMS = (((1,), (1,)), ((), ()))


def _params(*sem):
    return pltpu.CompilerParams(dimension_semantics=sem, vmem_limit_bytes=VMEM_LIMIT)


def _resident(shape):
    zeros = (0,) * len(shape)
    return pl.BlockSpec(shape, lambda *_: zeros, pipeline_mode=pl.Buffered(1))


def _whole(shape):
    zeros = (0,) * len(shape)
    return pl.BlockSpec(shape, lambda *_: zeros)


def _layer(layer, shape):
    zeros = (0,) * len(shape)
    return pl.BlockSpec((None,) + shape, lambda *_: (layer,) + zeros, pipeline_mode=pl.Buffered(1))


def _rms(x, g):
    ms = jnp.mean(x * x, axis=-1, keepdims=True)
    return x * lax.rsqrt(ms + EPS) * g


def _softplus(z):
    return jnp.maximum(z, 0.0) + jnp.log1p(jnp.exp(-jnp.abs(z)))


def _rope(x, c, s_lo, s_hi):
    n = x.shape[-1]
    half = ROT_DIM // 2
    return x * c + pltpu.roll(x, half, 1) * s_hi + pltpu.roll(x, n - half, 1) * s_lo


def _rope_tables(pos):
    half = ROT_DIM // 2
    inv = ROPE_THETA ** (-jnp.arange(half, dtype=F32) / half)
    ang = pos.astype(F32)[:, None] * inv[None, :]
    cos, sin = jnp.cos(ang), jnp.sin(ang)
    t = pos.shape[0]
    rest = HEAD_DIM - ROT_DIM
    c = jnp.concatenate([cos, cos, jnp.ones((t, rest), F32)], axis=1)
    s_hi = jnp.concatenate([jnp.zeros((t, half), F32), sin, jnp.zeros((t, rest), F32)], axis=1)
    s_lo = jnp.concatenate([-sin, jnp.zeros((t, half + rest), F32)], axis=1)
    return c, s_lo, s_hi


def _lru_coeffs(z_a, z_i, uc, sp):
    r = jax.nn.sigmoid(z_a)
    i = jax.nn.sigmoid(z_i)
    log_a = (-LRU_C * r) * sp
    a = jnp.exp(log_a)
    b = jnp.sqrt(-jnp.tanh(log_a) * (a * a + 1.0)) * (i * uc)
    return a, b


def _rg_prompt_kernel(x_ref, g_ref, win_ref, cw_ref, cb_ref, wg_ref, ba_ref, bi_ref, lam_ref,
                      wout_ref, xo_ref, conv_ref, hl_ref, ext_ref, a_ref, b_ref, gg_ref, hc_ref):
    tt = RG_TILE
    t = pl.program_id(1)

    @pl.when(t == 0)
    def _():
        ext_ref[0:SUBLANES, :] = jnp.zeros((SUBLANES, LRU_WIDTH), F32)
        hc_ref[...] = jnp.zeros_like(hc_ref)

    x = x_ref[0]
    xn = _rms(x, g_ref[...]).astype(BF16)
    proj = jnp.dot(xn, win_ref[...], preferred_element_type=F32)
    gg_ref[...] = jax.nn.gelu(proj[:, :LRU_WIDTH])
    ext_ref[SUBLANES:SUBLANES + tt, :] = proj[:, LRU_WIDTH:]

    base = SUBLANES - (CONV_WIDTH - 1)
    uc = cb_ref[...] + ext_ref[base:base + tt, :] * cw_ref[0:1, :]
    for j in range(1, CONV_WIDTH):
        uc = uc + ext_ref[base + j:base + j + tt, :] * cw_ref[j:j + 1, :]
    conv_ref[0] = ext_ref[tt + base:tt + SUBLANES, :]
    ext_ref[0:SUBLANES, :] = ext_ref[tt:tt + SUBLANES, :]

    sp = _softplus(-lam_ref[...])
    for n in range(N_GATE_BLOCKS):
        sl = slice(n * GATE_BLOCK, (n + 1) * GATE_BLOCK)
        ucn = uc[:, sl]
        z = jnp.dot(ucn.astype(BF16), wg_ref[n], preferred_element_type=F32)
        a, b = _lru_coeffs(z[:, :GATE_BLOCK] + ba_ref[:, sl], z[:, GATE_BLOCK:] + bi_ref[:, sl],
                           ucn, sp[:, sl])
        a_ref[:, sl] = a
        b_ref[:, sl] = b

    row = lax.broadcasted_iota(jnp.int32, (SUBLANES, LRU_WIDTH), 0)

    def group(gi, h_prev):
        r0 = pl.multiple_of(gi * SUBLANES, SUBLANES)
        a8 = a_ref[pl.ds(r0, SUBLANES), :]
        b8 = b_ref[pl.ds(r0, SUBLANES), :]
        s = 1
        while s < SUBLANES:
            keep = row >= s
            b8 = jnp.where(keep, a8 * pltpu.roll(b8, s, 0) + b8, b8)
            a8 = jnp.where(keep, a8 * pltpu.roll(a8, s, 0), a8)
            s *= 2
        h = a8 * h_prev + b8
        b_ref[pl.ds(r0, SUBLANES), :] = h
        return h[SUBLANES - 1:SUBLANES, :]

    h_last = lax.fori_loop(0, tt // SUBLANES, group, hc_ref[...])
    hc_ref[...] = h_last
    hl_ref[0] = h_last

    y = jnp.dot((b_ref[...] * gg_ref[...]).astype(BF16), wout_ref[...], preferred_element_type=F32)
    xo_ref[0] = x + y


def _rg_prompt(x, g, w_in, conv_w, conv_b, wg, ba, bi, lam, w_out):
    bsz, t, d = x.shape
    tt = RG_TILE
    vec = lambda n: _resident((1, n))
    return pl.pallas_call(
        _rg_prompt_kernel,
        name="rg_prompt",
        grid=(bsz, t // tt),
        in_specs=[
            pl.BlockSpec((1, tt, d), lambda b, i: (b, i, 0)),
            vec(d),
            _resident((d, 2 * LRU_WIDTH)),
            _resident((CONV_WIDTH, LRU_WIDTH)),
            vec(LRU_WIDTH),
            _resident((N_GATE_BLOCKS, GATE_BLOCK, 2 * GATE_BLOCK)),
            vec(LRU_WIDTH), vec(LRU_WIDTH), vec(LRU_WIDTH),
            _resident((LRU_WIDTH, d)),
        ],
        out_specs=[
            pl.BlockSpec((1, tt, d), lambda b, i: (b, i, 0)),
            pl.BlockSpec((1, CONV_WIDTH - 1, LRU_WIDTH), lambda b, i: (b, 0, 0)),
            pl.BlockSpec((1, 1, LRU_WIDTH), lambda b, i: (b, 0, 0)),
        ],
        out_shape=[
            jax.ShapeDtypeStruct((bsz, t, d), F32),
            jax.ShapeDtypeStruct((bsz, CONV_WIDTH - 1, LRU_WIDTH), F32),
            jax.ShapeDtypeStruct((bsz, 1, LRU_WIDTH), F32),
        ],
        scratch_shapes=[
            pltpu.VMEM((tt + SUBLANES, LRU_WIDTH), F32),
            pltpu.VMEM((tt, LRU_WIDTH), F32),
            pltpu.VMEM((tt, LRU_WIDTH), F32),
            pltpu.VMEM((tt, LRU_WIDTH), F32),
            pltpu.VMEM((1, LRU_WIDTH), F32),
        ],
        compiler_params=_params("arbitrary", "arbitrary"),
    )(x, g, w_in, conv_w, conv_b, wg, ba, bi, lam, w_out)


def _ffn_kernel(x_ref, g_ref, win_ref, wout_ref, gf_ref, o_ref, act_ref, *, final, precision):
    cdt = win_ref.dtype
    x = x_ref[...]
    xn = _rms(x, g_ref[...]).astype(cdt)
    for c in range(D_FF // FF_CHUNK):
        lo = c * FF_CHUNK
        gate = jnp.dot(xn, win_ref[:, lo:lo + FF_CHUNK], preferred_element_type=F32,
                       precision=precision)
        up = jnp.dot(xn, win_ref[:, D_FF + lo:D_FF + lo + FF_CHUNK], preferred_element_type=F32,
                     precision=precision)
        act_ref[:, lo:lo + FF_CHUNK] = (jax.nn.silu(gate) * up).astype(cdt)
    y = x + jnp.dot(act_ref[...], wout_ref[...], preferred_element_type=F32, precision=precision)
    if final:
        y = _rms(y, gf_ref[...])
    o_ref[...] = y


def _ffn(x, g, w_in, w_out, g_final, layer, *, final, tile):
    n, d = x.shape
    precision = HIGHEST if w_in.dtype == F32 else None
    return pl.pallas_call(
        functools.partial(_ffn_kernel, final=final, precision=precision),
        name="ffn",
        grid=(n // tile,),
        in_specs=[
            pl.BlockSpec((tile, d), lambda i: (i, 0)),
            _resident((1, d)),
            _layer(layer, (d, 2 * D_FF)),
            _layer(layer, (D_FF, d)),
            _resident((1, d)),
        ],
        out_specs=pl.BlockSpec((tile, d), lambda i: (i, 0)),
        out_shape=jax.ShapeDtypeStruct((n, d), F32),
        scratch_shapes=[pltpu.VMEM((tile, D_FF), w_in.dtype)],
        compiler_params=_params("arbitrary"),
    )(x, g, w_in, w_out, g_final)


def _kv_prompt_kernel(x_ref, g_ref, w_ref, c_ref, slo_ref, shi_ref,
                      k_ref, v_ref, kb_ref, vt_ref, km_ref):
    xn = _rms(x_ref[0], g_ref[...]).astype(BF16)
    kv = jnp.dot(xn, w_ref[...], preferred_element_type=F32)
    tile4 = lambda r: jnp.concatenate([r[...]] * N_KV_HEADS, axis=1)
    k = _rope(kv[:, :KV_DIM], tile4(c_ref), tile4(slo_ref), tile4(shi_ref))
    v = kv[:, KV_DIM:]
    k_ref[0] = k
    v_ref[0] = v
    kb_ref[0] = k.astype(BF16)
    for bb in range(KV_TILE // MOBA_BLOCK):
        rows = slice(bb * MOBA_BLOCK, (bb + 1) * MOBA_BLOCK)
        vt_ref[0, bb] = v[rows, :].T.astype(BF16)
        km_ref[0, bb] = jnp.sum(k[rows, :], axis=0, keepdims=True) * (1.0 / MOBA_BLOCK)


def _kv_prompt(x, g, w_kv, c, s_lo, s_hi):
    bsz, t, d = x.shape
    tt = KV_TILE
    nbt = tt // MOBA_BLOCK
    nb = t // MOBA_BLOCK
    tab = pl.BlockSpec((tt, HEAD_DIM), lambda b, i: (i, 0))
    row_spec = pl.BlockSpec((1, tt, KV_DIM), lambda b, i: (b, i, 0))
    return pl.pallas_call(
        _kv_prompt_kernel,
        name="kv_prompt",
        grid=(bsz, t // tt),
        in_specs=[
            pl.BlockSpec((1, tt, d), lambda b, i: (b, i, 0)),
            _resident((1, d)),
            _resident((d, 2 * KV_DIM)),
            tab, tab, tab,
        ],
        out_specs=[
            row_spec, row_spec, row_spec,
            pl.BlockSpec((1, nbt, KV_DIM, MOBA_BLOCK), lambda b, i: (b, i, 0, 0)),
            pl.BlockSpec((1, nbt, 1, KV_DIM), lambda b, i: (b, i, 0, 0)),
        ],
        out_shape=[
            jax.ShapeDtypeStruct((bsz, t, KV_DIM), F32),
            jax.ShapeDtypeStruct((bsz, t, KV_DIM), F32),
            jax.ShapeDtypeStruct((bsz, t, KV_DIM), BF16),
            jax.ShapeDtypeStruct((bsz, nb, KV_DIM, MOBA_BLOCK), BF16),
            jax.ShapeDtypeStruct((bsz, nb, 1, KV_DIM), F32),
        ],
        compiler_params=_params("arbitrary", "arbitrary"),
    )(x, g, w_kv, c, s_lo, s_hi)


def _select_bias(gate, n_allowed, n_valid):
    nb, nq = gate.shape
    rowi = lax.broadcasted_iota(jnp.int32, (nb, nq), 0).astype(F32)
    g = jnp.where(rowi < n_allowed, gate, NEG)
    bias = jnp.full((nb, nq), NEG, F32)
    for r in range(MOBA_TOPK):
        mx = jnp.max(g, axis=0, keepdims=True)
        idx = jnp.min(jnp.where(g == mx, rowi, float(nb)), axis=0, keepdims=True)
        ok = (n_valid > r).astype(F32)
        idx = idx * ok - (1.0 - ok)
        pick = rowi == idx
        bias = jnp.where(pick, 0.0, bias)
        g = jnp.where(pick, NEG, g)
    return bias


def _attn_prompt_kernel(x_ref, g_ref, wq_ref, wo_ref, c_ref, slo_ref, shi_ref, k_ref, vt_ref,
                        km_ref, o_ref, q2s_ref, s0_ref, s1_ref, m_ref, acc_ref, attn_ref):
    tq = MOBA_BLOCK
    nq = GROUP * tq
    i = pl.program_id(1)
    x = x_ref[0]
    xn = _rms(x, g_ref[...]).astype(BF16)
    q = jnp.dot(xn, wq_ref[...], preferred_element_type=F32)
    c, s_lo, s_hi = c_ref[...], slo_ref[...], shi_ref[...]
    own = i.astype(F32)

    key_pos = lax.broadcasted_iota(jnp.int32, (tq, nq), 0)
    qry_pos = lax.broadcasted_iota(jnp.int32, (tq, nq), 1) % tq
    causal = key_pos <= qry_pos
    kv_lanes = [slice(kvh * HEAD_DIM, (kvh + 1) * HEAD_DIM) for kvh in range(N_KV_HEADS)]

    ones_rows = jnp.where(lax.broadcasted_iota(jnp.int32, (BF16_ROWS, tq), 0) == 0, 1.0, 0.0
                          ).astype(BF16)

    def vt_ones(j, lanes):
        return jnp.concatenate([vt_ref[0, j, lanes, :], ones_rows], axis=0)

    for kvh, lanes in enumerate(kv_lanes):
        q2 = jnp.concatenate(
            [_rope(q[:, h * HEAD_DIM:(h + 1) * HEAD_DIM], c, s_lo, s_hi)
             for h in range(kvh * GROUP, (kvh + 1) * GROUP)], axis=0)
        gate = lax.dot_general(km_ref[0, :, lanes], q2, NT_DIMS, precision=HIGHEST,
                               preferred_element_type=F32)
        bias = _select_bias(gate, own, own)
        bias_t = jnp.concatenate(
            [bias, jnp.zeros((HEAD_DIM - bias.shape[0], nq), F32)], axis=0).T
        q2s = (q2 * (SM_SCALE * LOG2E)).astype(BF16)
        q2s_ref[kvh] = jnp.concatenate([q2s, bias_t.astype(BF16)], axis=1)

        s = lax.dot_general(k_ref[0, i, :, lanes], q2s, NT_DIMS, preferred_element_type=F32)
        s = jnp.where(causal, s, NEG)
        m = jnp.max(s, axis=0, keepdims=True)
        p = jnp.exp2(s - m)
        m_ref[kvh] = m
        acc_ref[kvh] = jnp.dot(vt_ones(i, lanes), p.astype(BF16), preferred_element_type=F32)

    block_lane = lax.broadcasted_iota(jnp.int32, (1, HEAD_DIM), 1)

    def qk(j, s_out):
        one_hot = jnp.broadcast_to(jnp.where(block_lane == j, 1.0, 0.0).astype(BF16),
                                   (tq, HEAD_DIM))
        for kvh, lanes in enumerate(kv_lanes):
            k_aug = jnp.concatenate([k_ref[0, j, :, lanes], one_hot], axis=1)
            s_out[kvh] = lax.dot_general(k_aug, q2s_ref[kvh], NT_DIMS,
                                         preferred_element_type=F32)

    def past_block(j, s_cur, s_nxt):
        qk(j + 1, s_nxt)
        for kvh, lanes in enumerate(kv_lanes):
            s = s_cur[kvh]
            m = m_ref[kvh]
            m_new = jnp.maximum(m, jnp.max(s, axis=0, keepdims=True))
            alpha = jnp.exp2(m - m_new)
            p = jnp.exp2(s - m_new)
            m_ref[kvh] = m_new
            acc_ref[kvh] = alpha * acc_ref[kvh] + jnp.dot(
                vt_ones(j, lanes), p.astype(BF16), preferred_element_type=F32)

    qk(0, s0_ref)

    def two_blocks(t, carry):
        past_block(2 * t, s0_ref, s1_ref)
        past_block(2 * t + 1, s1_ref, s0_ref)
        return carry

    lax.fori_loop(0, i // 2, two_blocks, 0)

    @pl.when(i % 2 == 1)
    def _():
        past_block(i - 1, s0_ref, s1_ref)

    for kvh in range(N_KV_HEADS):
        o = (acc_ref[kvh, :HEAD_DIM, :] / acc_ref[kvh, HEAD_DIM:HEAD_DIM + 1, :]).T
        for j in range(GROUP):
            h = kvh * GROUP + j
            attn_ref[:, h * HEAD_DIM:(h + 1) * HEAD_DIM] = o[j * tq:(j + 1) * tq, :].astype(BF16)

    o_ref[0] = x + jnp.dot(attn_ref[...], wo_ref[...], preferred_element_type=F32)


def _attn_prompt(x, g, w_q, w_o, c, s_lo, s_hi, kb, vt, km):
    bsz, t, d = x.shape
    tq = MOBA_BLOCK
    nb = t // tq
    tab = pl.BlockSpec((tq, HEAD_DIM), lambda b, i: (i, 0))
    per_seq = lambda shape: pl.BlockSpec((1,) + shape, lambda b, i: (b,) + (0,) * len(shape),
                                         pipeline_mode=pl.Buffered(1))
    return pl.pallas_call(
        _attn_prompt_kernel,
        name="attn_prompt",
        grid=(bsz, nb),
        in_specs=[
            pl.BlockSpec((1, tq, d), lambda b, i: (b, i, 0)),
            _resident((1, d)),
            _resident((d, d)),
            _resident((d, d)),
            tab, tab, tab,
            per_seq((nb, tq, KV_DIM)),
            per_seq((nb, KV_DIM, tq)),
            per_seq((nb, KV_DIM)),
        ],
        out_specs=pl.BlockSpec((1, tq, d), lambda b, i: (b, i, 0)),
        out_shape=jax.ShapeDtypeStruct((bsz, t, d), F32),
        scratch_shapes=[
            pltpu.VMEM((N_KV_HEADS, GROUP * tq, 2 * HEAD_DIM), BF16),
            pltpu.VMEM((N_KV_HEADS, tq, GROUP * tq), F32),
            pltpu.VMEM((N_KV_HEADS, tq, GROUP * tq), F32),
            pltpu.VMEM((N_KV_HEADS, 1, GROUP * tq), F32),
            pltpu.VMEM((N_KV_HEADS, HEAD_DIM + BF16_ROWS, GROUP * tq), F32),
            pltpu.VMEM((tq, d), BF16),
        ],
        compiler_params=_params("arbitrary", "arbitrary"),
    )(x, g, w_q, w_o, c, s_lo, s_hi, kb, vt, km)


def _dot_hi(a, b):
    return jnp.dot(a, b, preferred_element_type=F32, precision=HIGHEST)


def _rg_sample_kernel(x_ref, cs_ref, h0_ref, g_ref, win_ref, cw_ref, cb_ref, wa_ref, wi_ref,
                      ba_ref, bi_ref, lam_ref, wout_ref, xo_ref, cso_ref, ho_ref):
    x = x_ref[...]
    proj = _dot_hi(_rms(x, g_ref[...]), win_ref[...])
    gate, u = proj[:, :LRU_WIDTH], proj[:, LRU_WIDTH:]
    uc = cb_ref[...] + cs_ref[0] * cw_ref[0:1, :]
    for j in range(1, CONV_WIDTH - 1):
        uc = uc + cs_ref[j] * cw_ref[j:j + 1, :]
    uc = uc + u * cw_ref[CONV_WIDTH - 1:CONV_WIDTH, :]
    for j in range(CONV_WIDTH - 2):
        cso_ref[j] = cs_ref[j + 1]
    cso_ref[CONV_WIDTH - 2] = u

    sp = _softplus(-lam_ref[...])
    h0 = h0_ref[...]
    hs = []
    for n in range(N_GATE_BLOCKS):
        sl = slice(n * GATE_BLOCK, (n + 1) * GATE_BLOCK)
        ucn = uc[:, sl]
        a, b = _lru_coeffs(_dot_hi(ucn, wa_ref[n]) + ba_ref[:, sl],
                           _dot_hi(ucn, wi_ref[n]) + bi_ref[:, sl], ucn, sp[:, sl])
        hs.append(b + a * h0[:, sl])
    h = jnp.concatenate(hs, axis=1)
    ho_ref[...] = h
    xo_ref[...] = x + _dot_hi(h * jax.nn.gelu(gate), wout_ref[...])


def _rg_sample(x, cs, h0, g, w_in, conv_w, conv_b, wa, wi, ba, bi, lam, w_out, layer):
    n, d = x.shape
    vec = _resident((1, LRU_WIDTH))
    gates = _layer(layer, (N_GATE_BLOCKS, GATE_BLOCK, GATE_BLOCK))
    return pl.pallas_call(
        _rg_sample_kernel,
        name="rg_sample",
        grid=(1,),
        in_specs=[
            _resident((n, d)), _resident((CONV_WIDTH - 1, n, LRU_WIDTH)), _resident((n, LRU_WIDTH)),
            _resident((1, d)), _layer(layer, (d, 2 * LRU_WIDTH)), _resident((CONV_WIDTH, LRU_WIDTH)),
            vec, gates, gates, vec, vec, vec, _layer(layer, (LRU_WIDTH, d)),
        ],
        out_specs=[_whole((n, d)), _whole((CONV_WIDTH - 1, n, LRU_WIDTH)), _whole((n, LRU_WIDTH))],
        out_shape=[
            jax.ShapeDtypeStruct((n, d), F32),
            jax.ShapeDtypeStruct((CONV_WIDTH - 1, n, LRU_WIDTH), F32),
            jax.ShapeDtypeStruct((n, LRU_WIDTH), F32),
        ],
        compiler_params=_params("arbitrary"),
    )(x, cs, h0, g, w_in, conv_w, conv_b, wa, wi, ba, bi, lam, w_out)


def _proj_sample_kernel(x_ref, g_ref, w_ref, c_ref, slo_ref, shi_ref, o_ref, *, n_rope_heads):
    y = _dot_hi(_rms(x_ref[...], g_ref[...]), w_ref[...])
    if n_rope_heads:
        tile = lambda r: jnp.concatenate([r[...]] * n_rope_heads, axis=1)
        width = n_rope_heads * HEAD_DIM
        rot = _rope(y[:, :width], tile(c_ref), tile(slo_ref), tile(shi_ref))
        y = jnp.concatenate([rot, y[:, width:]], axis=1) if width < y.shape[1] else rot
    o_ref[...] = y


def _proj_sample(x, g, w, layer, c, s_lo, s_hi, n_rope_heads):
    n, d = x.shape
    width = w.shape[2]
    tab = _resident((1, HEAD_DIM))
    return pl.pallas_call(
        functools.partial(_proj_sample_kernel, n_rope_heads=n_rope_heads),
        name="proj_sample",
        grid=(1,),
        in_specs=[_resident((n, d)), _resident((1, d)), _layer(layer, (d, width)), tab, tab, tab],
        out_specs=_whole((n, width)),
        out_shape=jax.ShapeDtypeStruct((n, width), F32),
        compiler_params=_params("arbitrary"),
    )(x, g, w, c, s_lo, s_hi)


def _kmeans_sample_kernel(pt_ref, *refs):
    page_refs, o_ref = refs[:-1], refs[-1]
    step = pl.program_id(1)
    blocks = len(page_refs) // PAGES_PER_BLOCK
    means = []
    for blk in range(blocks):
        pages = page_refs[blk * PAGES_PER_BLOCK:(blk + 1) * PAGES_PER_BLOCK]
        s = sum(jnp.sum(p[0], axis=0) for p in pages)
        s = sum(s[par * N_KV_HEADS:(par + 1) * N_KV_HEADS] for par in range(ROW_PARITIES))
        means.append(jnp.concatenate([s[kvh:kvh + 1, :] for kvh in range(N_KV_HEADS)], axis=1))
    first = pl.multiple_of(step * blocks, blocks)
    o_ref[0, pl.ds(first, blocks), :] = jnp.concatenate(means, axis=0) * (1.0 / MOBA_BLOCK)


def _kmeans_sample(page_table_flat, cache_k, n_seq, n_pages):
    nb = n_pages // PAGES_PER_BLOCK

    def page(k):
        return pl.BlockSpec(
            (1,) + PAGE_VIEW,
            lambda b, n, pt: (pt[b * n_pages + KMEANS_PAGES * n + k], 0, 0, 0))

    return pl.pallas_call(
        _kmeans_sample_kernel,
        name="kmeans_sample",
        grid_spec=pltpu.PrefetchScalarGridSpec(
            num_scalar_prefetch=1,
            grid=(n_seq, n_pages // KMEANS_PAGES),
            in_specs=[page(k) for k in range(KMEANS_PAGES)],
            out_specs=pl.BlockSpec((1, nb, KV_DIM), lambda b, n, pt: (b, 0, 0)),
        ),
        out_shape=jax.ShapeDtypeStruct((n_seq, nb, KV_DIM), F32),
        compiler_params=_params("arbitrary", "arbitrary"),
    )(page_table_flat, *([cache_k] * KMEANS_PAGES))


def _gate_sample_kernel(q_ref, km_ref, sel_ref):
    km = km_ref[...]
    n_seq, nb, _ = km.shape
    blk = lax.broadcasted_iota(jnp.int32, (n_seq, nb, 1), 1).astype(F32)
    lane = lax.broadcasted_iota(jnp.int32, (n_seq, 1, HEAD_DIM), 2)
    for h in range(N_HEADS):
        kvh = h // GROUP
        qh = q_ref[:, :, h * HEAD_DIM:(h + 1) * HEAD_DIM]
        g = jnp.sum(km[:, :, kvh * HEAD_DIM:(kvh + 1) * HEAD_DIM] * qh, axis=-1, keepdims=True)
        out = jnp.zeros((n_seq, 1, HEAD_DIM), F32)
        for r in range(MOBA_TOPK):
            mx = jnp.max(g, axis=1, keepdims=True)
            idx = jnp.min(jnp.where(g == mx, blk, float(nb)), axis=1, keepdims=True)
            g = jnp.where(blk == idx, NEG, g)
            out = jnp.where(lane == r, idx, out)
        sel_ref[h] = out.astype(jnp.int32)


def _gate_sample(q3, km):
    n_seq = q3.shape[0]
    return pl.pallas_call(
        _gate_sample_kernel,
        name="gate_sample",
        out_shape=jax.ShapeDtypeStruct((N_HEADS, n_seq, 1, HEAD_DIM), jnp.int32),
        compiler_params=pltpu.CompilerParams(vmem_limit_bytes=VMEM_LIMIT),
    )(q3, km)


def _attn_sample_kernel(pt_ref, sel_ref, q_ref, kn_ref, vn_ref, ck_ref, cv_ref, o_ref,
                        kbuf, vbuf, sem, *, n_pages):
    b = pl.program_id(0)
    n_slots = MOBA_TOPK * PAGES_PER_BLOCK
    rows = PAGE_SIZE // ROW_PARITIES

    def copies(seq, buf):
        out = []
        for h in range(N_HEADS):
            kvh = h // GROUP
            for r in range(MOBA_TOPK):
                blk = sel_ref[(seq * N_HEADS + h) * MOBA_TOPK + r]
                for half in range(PAGES_PER_BLOCK):
                    page = pt_ref[seq * n_pages + PAGES_PER_BLOCK * blk + half]
                    slot = r * PAGES_PER_BLOCK + half
                    for par in range(ROW_PARITIES):
                        sub = par * N_KV_HEADS + kvh
                        dst = (buf, h, slot, pl.ds(par * rows, rows))
                        out.append(pltpu.make_async_copy(ck_ref.at[page, :, sub], kbuf.at[dst], sem.at[0, buf]))
                        out.append(pltpu.make_async_copy(cv_ref.at[page, :, sub], vbuf.at[dst], sem.at[1, buf]))
        return out

    cur = b % 2

    @pl.when(b == 0)
    def _():
        for cp in copies(b, cur):
            cp.start()

    @pl.when(b + 1 < pl.num_programs(0))
    def _():
        for cp in copies(b + 1, 1 - cur):
            cp.start()

    for cp in copies(b, cur):
        cp.wait()

    for h in range(N_HEADS):
        kvh = h // GROUP
        qh = q_ref[0, :, h * HEAD_DIM:(h + 1) * HEAD_DIM] * SM_SCALE
        kn = kn_ref[0, :, kvh * HEAD_DIM:(kvh + 1) * HEAD_DIM]
        vn = vn_ref[0, :, kvh * HEAD_DIM:(kvh + 1) * HEAD_DIM]
        kk = kbuf[cur, h].reshape(n_slots * PAGE_SIZE, HEAD_DIM)
        vv = vbuf[cur, h].reshape(n_slots * PAGE_SIZE, HEAD_DIM)
        s = jnp.sum(kk * qh, axis=-1, keepdims=True)
        s_new = jnp.sum(kn * qh, axis=-1, keepdims=True)
        m = jnp.maximum(jnp.max(s, axis=0, keepdims=True), s_new)
        p = jnp.exp(s - m)
        p_new = jnp.exp(s_new - m)
        l = jnp.sum(p, axis=0, keepdims=True) + p_new
        acc = jnp.sum(p * vv, axis=0, keepdims=True) + p_new * vn
        o_ref[0, :, h * HEAD_DIM:(h + 1) * HEAD_DIM] = acc / l


def _attn_sample(page_table_flat, sel_flat, q3, k_new3, v_new3, cache_k, cache_v, n_pages):
    n_seq = q3.shape[0]
    n_slots = MOBA_TOPK * PAGES_PER_BLOCK
    row = lambda w: pl.BlockSpec((1, 1, w), lambda b, pt, sel: (b, 0, 0))
    return pl.pallas_call(
        functools.partial(_attn_sample_kernel, n_pages=n_pages),
        name="attn_sample",
        grid_spec=pltpu.PrefetchScalarGridSpec(
            num_scalar_prefetch=2,
            grid=(n_seq,),
            in_specs=[row(D_MODEL), row(KV_DIM), row(KV_DIM),
                      pl.BlockSpec(memory_space=pl.ANY), pl.BlockSpec(memory_space=pl.ANY)],
            out_specs=row(D_MODEL),
            scratch_shapes=[
                pltpu.VMEM((2, N_HEADS, n_slots, PAGE_SIZE, HEAD_DIM), F32),
                pltpu.VMEM((2, N_HEADS, n_slots, PAGE_SIZE, HEAD_DIM), F32),
                pltpu.SemaphoreType.DMA((2, 2)),
            ],
        ),
        out_shape=jax.ShapeDtypeStruct((n_seq, 1, D_MODEL), F32),
        compiler_params=_params("arbitrary"),
    )(page_table_flat, sel_flat, q3, k_new3, v_new3, cache_k, cache_v)


def _oproj_sample_kernel(x_ref, a_ref, w_ref, o_ref):
    o_ref[...] = x_ref[...] + _dot_hi(a_ref[...], w_ref[...])


def _oproj_sample(x, attn, w_o, layer):
    n, d = x.shape
    return pl.pallas_call(
        _oproj_sample_kernel,
        name="oproj_sample",
        grid=(1,),
        in_specs=[_resident((n, d)), _resident((n, d)), _layer(layer, (d, d))],
        out_specs=_whole((n, d)),
        out_shape=jax.ShapeDtypeStruct((n, d), F32),
        compiler_params=_params("arbitrary"),
    )(x, attn, w_o)


def _prompt_trunk(x, w):
    bsz, t, d = x.shape
    c, s_lo, s_hi = _rope_tables(jnp.arange(t, dtype=jnp.int32))
    row = lambda a: a.reshape(1, -1)
    bf = lambda a: a.astype(BF16)
    convs, hs = [], []
    n_rg = w["w_rg_in"].shape[0]
    depth = w["g_mix"].shape[0]

    w_ffn_in, w_ffn_out = bf(w["w_ffn_in"]), bf(w["w_ffn_out"])

    def ffn(x, l):
        y = _ffn(x.reshape(bsz * t, d), row(w["g_ffn"][l]), w_ffn_in, w_ffn_out,
                 row(w["g_final"]), l, final=(l == depth - 1), tile=FFN_TILE)
        return y.reshape(bsz, t, d)

    for l in range(n_rg):
        wg = jnp.concatenate([w["rg_gate_a_w"][l], w["rg_gate_i_w"][l]], axis=-1)
        x, conv, h = _rg_prompt(
            x, row(w["g_mix"][l]), bf(w["w_rg_in"][l]), w["rg_conv_w"][l], row(w["rg_conv_b"][l]),
            bf(wg), row(w["rg_gate_a_b"][l]), row(w["rg_gate_i_b"][l]), row(w["rg_lambda"][l]),
            bf(w["w_rg_out"][l]))
        convs.append(conv)
        hs.append(h[:, 0, :])
        x = ffn(x, l)

    k, v, kb, vt, km = _kv_prompt(x, row(w["g_kv"]), bf(w["w_kv"]), c, s_lo, s_hi)
    nb = t // MOBA_BLOCK
    kb = kb.reshape(bsz, nb, MOBA_BLOCK, KV_DIM)
    km = km.reshape(bsz, nb, KV_DIM)
    for a in range(depth - n_rg):
        l = n_rg + a
        x = _attn_prompt(x, row(w["g_mix"][l]), bf(w["w_q"][a]), bf(w["w_o"][a]), c, s_lo, s_hi, kb, vt, km)
        x = ffn(x, l)
    shape4 = (bsz, t, N_KV_HEADS, HEAD_DIM)
    return x, jnp.stack(convs), jnp.stack(hs), k.reshape(shape4), v.reshape(shape4)


def _sample_trunk(x3, state_conv, state_h, cache_k, cache_v, page_table, w):
    n_seq, _, d = x3.shape
    n_pages = page_table.shape[1]
    past_len = n_pages * PAGE_SIZE
    c, s_lo, s_hi = _rope_tables(jnp.full((1,), past_len, jnp.int32))
    row = lambda a: a.reshape(1, -1)
    n_rg = w["w_rg_in"].shape[0]
    depth = w["g_mix"].shape[0]
    x = x3.reshape(n_seq, d)
    pt_flat = page_table.reshape(-1)
    ck = cache_k.reshape((cache_k.shape[0],) + PAGE_VIEW)
    cv = cache_v.reshape((cache_v.shape[0],) + PAGE_VIEW)
    convs, hs = [], []

    def ffn(x, l):
        return _ffn(x, row(w["g_ffn"][l]), w["w_ffn_in"], w["w_ffn_out"], row(w["g_final"]), l,
                    final=(l == depth - 1), tile=n_seq)

    for l in range(n_rg):
        x, conv, h = _rg_sample(
            x, jnp.swapaxes(state_conv[l], 0, 1), state_h[l], row(w["g_mix"][l]), w["w_rg_in"],
            w["rg_conv_w"][l], row(w["rg_conv_b"][l]), w["rg_gate_a_w"], w["rg_gate_i_w"],
            row(w["rg_gate_a_b"][l]), row(w["rg_gate_i_b"][l]), row(w["rg_lambda"][l]), w["w_rg_out"], l)
        convs.append(jnp.swapaxes(conv, 0, 1))
        hs.append(h)
        x = ffn(x, l)

    kv = _proj_sample(x, row(w["g_kv"]), w["w_kv"][None], 0, c, s_lo, s_hi, N_KV_HEADS)
    k_new, v_new = kv[:, :KV_DIM], kv[:, KV_DIM:]
    km = _kmeans_sample(pt_flat, ck, n_seq, n_pages)
    for a in range(depth - n_rg):
        l = n_rg + a
        q = _proj_sample(x, row(w["g_mix"][l]), w["w_q"], a, c, s_lo, s_hi, N_HEADS)
        q3 = q.reshape(n_seq, 1, d)
        sel = _gate_sample(q3, km)[:, :, 0, :MOBA_TOPK]
        sel_flat = jnp.transpose(sel, (1, 0, 2)).reshape(-1)
        attn = _attn_sample(pt_flat, sel_flat, q3, k_new.reshape(n_seq, 1, KV_DIM),
                            v_new.reshape(n_seq, 1, KV_DIM), ck, cv, n_pages)
        x = _oproj_sample(x, attn.reshape(n_seq, d), w["w_o"], a)
        x = ffn(x, l)
    shape4 = (n_seq, 1, N_KV_HEADS, HEAD_DIM)
    return (x.reshape(n_seq, 1, d), jnp.stack(convs), jnp.stack(hs),
            k_new.reshape(shape4), v_new.reshape(shape4))


def kernel(x_prompt, x_sample, cache_k, cache_v, state_conv, state_h, page_table,
           g_mix, g_ffn, w_rg_in, rg_conv_w, rg_conv_b, rg_gate_a_w, rg_gate_a_b,
           rg_gate_i_w, rg_gate_i_b, rg_lambda, w_rg_out, g_kv, w_kv, w_q, w_o,
           w_ffn_in, w_ffn_out, g_final):
    assert x_sample.shape[1] == 1 and x_prompt.shape[1] % MOBA_BLOCK == 0
    assert MOBA_TOPK <= page_table.shape[1] // PAGES_PER_BLOCK
    w = dict(g_mix=g_mix, g_ffn=g_ffn, w_rg_in=w_rg_in, rg_conv_w=rg_conv_w, rg_conv_b=rg_conv_b,
             rg_gate_a_w=rg_gate_a_w, rg_gate_a_b=rg_gate_a_b, rg_gate_i_w=rg_gate_i_w,
             rg_gate_i_b=rg_gate_i_b, rg_lambda=rg_lambda, w_rg_out=w_rg_out, g_kv=g_kv, w_kv=w_kv,
             w_q=w_q, w_o=w_o, w_ffn_in=w_ffn_in, w_ffn_out=w_ffn_out, g_final=g_final)
    y_p, p_conv, p_h, p_k, p_v = _prompt_trunk(x_prompt, w)
    y_s, s_conv, s_h, s_k, s_v = _sample_trunk(x_sample, state_conv, state_h, cache_k, cache_v,
                                               page_table, w)
    return (y_p, y_s, p_conv, p_h, p_k, p_v, s_conv, s_h, s_k, s_v)
```

```python
import functools
import math

import jax
import jax.numpy as jnp
from jax import lax
from jax.experimental import pallas as pl
from jax.experimental.pallas import tpu as pltpu

D_MODEL = 1024
LRU_WIDTH = D_MODEL
N_GATE_BLOCKS = 8
GATE_BLOCK = LRU_WIDTH // N_GATE_BLOCKS
CONV_WIDTH = 4
LRU_C = 8.0
N_HEADS = 8
HEAD_DIM = D_MODEL // N_HEADS
N_KV_HEADS = 4
KV_DIM = N_KV_HEADS * HEAD_DIM
GROUP = N_HEADS // N_KV_HEADS
ROT_DIM = HEAD_DIM // 4
ROPE_THETA = 500000.0
MOBA_BLOCK = 256
MOBA_TOPK = 3
PAGE_SIZE = 128
PAGES_PER_BLOCK = MOBA_BLOCK // PAGE_SIZE
D_FF = -(-8 * D_MODEL // (3 * 256)) * 256
EPS = 1e-6
NEG = -1e30
SM_SCALE = 1.0 / math.sqrt(HEAD_DIM)
LOG2E = math.log2(math.e)

F32 = jnp.float32
BF16 = jnp.bfloat16
HIGHEST = lax.Precision.HIGHEST

SUBLANES = 8
BF16_ROWS = 16
FF_CHUNK = 256
RG_TILE = 256
FFN_TILE = 512
KV_TILE = 512
KMEANS_PAGES = 16
ROW_PARITIES = SUBLANES // N_KV_HEADS
PAGE_VIEW = (PAGE_SIZE // ROW_PARITIES, SUBLANES, HEAD_DIM)
VMEM_LIMIT = 56 * 1024 * 1024

NT_DIMS = (((1,), (1,)), ((), ()))


def _params(*sem):
    return pltpu.CompilerParams(dimension_semantics=sem, vmem_limit_bytes=VMEM_LIMIT)


def _resident(shape):
    zeros = (0,) * len(shape)
    return pl.BlockSpec(shape, lambda *_: zeros, pipeline_mode=pl.Buffered(1))


def _whole(shape):
    zeros = (0,) * len(shape)
    return pl.BlockSpec(shape, lambda *_: zeros)


def _layer(layer, shape):
    zeros = (0,) * len(shape)
    return pl.BlockSpec((None,) + shape, lambda *_: (layer,) + zeros, pipeline_mode=pl.Buffered(1))


def _rms(x, g):
    ms = jnp.mean(x * x, axis=-1, keepdims=True)
    return x * lax.rsqrt(ms + EPS) * g


def _softplus(z):
    return jnp.maximum(z, 0.0) + jnp.log1p(jnp.exp(-jnp.abs(z)))


def _rope(x, c, s_lo, s_hi):
    n = x.shape[-1]
    half = ROT_DIM // 2
    return x * c + pltpu.roll(x, half, 1) * s_hi + pltpu.roll(x, n - half, 1) * s_lo


def _rope_tables(pos):
    half = ROT_DIM // 2
    inv = ROPE_THETA ** (-jnp.arange(half, dtype=F32) / half)
    ang = pos.astype(F32)[:, None] * inv[None, :]
    cos, sin = jnp.cos(ang), jnp.sin(ang)
    t = pos.shape[0]
    rest = HEAD_DIM - ROT_DIM
    c = jnp.concatenate([cos, cos, jnp.ones((t, rest), F32)], axis=1)
    s_hi = jnp.concatenate([jnp.zeros((t, half), F32), sin, jnp.zeros((t, rest), F32)], axis=1)
    s_lo = jnp.concatenate([-sin, jnp.zeros((t, half + rest), F32)], axis=1)
    return c, s_lo, s_hi


def _lru_coeffs(z_a, z_i, uc, sp):
    r = jax.nn.sigmoid(z_a)
    i = jax.nn.sigmoid(z_i)
    log_a = (-LRU_C * r) * sp
    a = jnp.exp(log_a)
    b = jnp.sqrt(-jnp.tanh(log_a) * (a * a + 1.0)) * (i * uc)
    return a, b


def _rg_prompt_kernel(x_ref, g_ref, win_ref, cw_ref, cb_ref, wg_ref, ba_ref, bi_ref, lam_ref,
                      wout_ref, xo_ref, conv_ref, hl_ref, ext_ref, a_ref, b_ref, gg_ref, hc_ref):
    tt = RG_TILE
    t = pl.program_id(1)

    @pl.when(t == 0)
    def _():
        ext_ref[0:SUBLANES, :] = jnp.zeros((SUBLANES, LRU_WIDTH), F32)
        hc_ref[...] = jnp.zeros_like(hc_ref)

    x = x_ref[0]
    xn = _rms(x, g_ref[...]).astype(BF16)
    proj = jnp.dot(xn, win_ref[...], preferred_element_type=F32)
    gg_ref[...] = jax.nn.gelu(proj[:, :LRU_WIDTH])
    ext_ref[SUBLANES:SUBLANES + tt, :] = proj[:, LRU_WIDTH:]

    base = SUBLANES - (CONV_WIDTH - 1)
    uc = cb_ref[...] + ext_ref[base:base + tt, :] * cw_ref[0:1, :]
    for j in range(1, CONV_WIDTH):
        uc = uc + ext_ref[base + j:base + j + tt, :] * cw_ref[j:j + 1, :]
    conv_ref[0] = ext_ref[tt + base:tt + SUBLANES, :]
    ext_ref[0:SUBLANES, :] = ext_ref[tt:tt + SUBLANES, :]

    sp = _softplus(-lam_ref[...])
    for n in range(N_GATE_BLOCKS):
        sl = slice(n * GATE_BLOCK, (n + 1) * GATE_BLOCK)
        ucn = uc[:, sl]
        z = jnp.dot(ucn.astype(BF16), wg_ref[n], preferred_element_type=F32)
        a, b = _lru_coeffs(z[:, :GATE_BLOCK] + ba_ref[:, sl], z[:, GATE_BLOCK:] + bi_ref[:, sl],
                           ucn, sp[:, sl])
        a_ref[:, sl] = a
        b_ref[:, sl] = b

    row = lax.broadcasted_iota(jnp.int32, (SUBLANES, LRU_WIDTH), 0)

    def group(gi, h_prev):
        r0 = pl.multiple_of(gi * SUBLANES, SUBLANES)
        a8 = a_ref[pl.ds(r0, SUBLANES), :]
        b8 = b_ref[pl.ds(r0, SUBLANES), :]
        s = 1
        while s < SUBLANES:
            keep = row >= s
            b8 = jnp.where(keep, a8 * pltpu.roll(b8, s, 0) + b8, b8)
            a8 = jnp.where(keep, a8 * pltpu.roll(a8, s, 0), a8)
            s *= 2
        h = a8 * h_prev + b8
        b_ref[pl.ds(r0, SUBLANES), :] = h
        return h[SUBLANES - 1:SUBLANES, :]

    h_last = lax.fori_loop(0, tt // SUBLANES, group, hc_ref[...])
    hc_ref[...] = h_last
    hl_ref[0] = h_last

    y = jnp.dot((b_ref[...] * gg_ref[...]).astype(BF16), wout_ref[...], preferred_element_type=F32)
    xo_ref[0] = x + y


def _rg_prompt(x, g, w_in, conv_w, conv_b, wg, ba, bi, lam, w_out):
    bsz, t, d = x.shape
    tt = RG_TILE
    vec = lambda n: _resident((1, n))
    return pl.pallas_call(
        _rg_prompt_kernel,
        name="rg_prompt",
        grid=(bsz, t // tt),
        in_specs=[
            pl.BlockSpec((1, tt, d), lambda b, i: (b, i, 0)),
            vec(d),
            _resident((d, 2 * LRU_WIDTH)),
            _resident((CONV_WIDTH, LRU_WIDTH)),
            vec(LRU_WIDTH),
            _resident((N_GATE_BLOCKS, GATE_BLOCK, 2 * GATE_BLOCK)),
            vec(LRU_WIDTH), vec(LRU_WIDTH), vec(LRU_WIDTH),
            _resident((LRU_WIDTH, d)),
        ],
        out_specs=[
            pl.BlockSpec((1, tt, d), lambda b, i: (b, i, 0)),
            pl.BlockSpec((1, CONV_WIDTH - 1, LRU_WIDTH), lambda b, i: (b, 0, 0)),
            pl.BlockSpec((1, 1, LRU_WIDTH), lambda b, i: (b, 0, 0)),
        ],
        out_shape=[
            jax.ShapeDtypeStruct((bsz, t, d), F32),
            jax.ShapeDtypeStruct((bsz, CONV_WIDTH - 1, LRU_WIDTH), F32),
            jax.ShapeDtypeStruct((bsz, 1, LRU_WIDTH), F32),
        ],
        scratch_shapes=[
            pltpu.VMEM((tt + SUBLANES, LRU_WIDTH), F32),
            pltpu.VMEM((tt, LRU_WIDTH), F32),
            pltpu.VMEM((tt, LRU_WIDTH), F32),
            pltpu.VMEM((tt, LRU_WIDTH), F32),
            pltpu.VMEM((1, LRU_WIDTH), F32),
        ],
        compiler_params=_params("arbitrary", "arbitrary"),
    )(x, g, w_in, conv_w, conv_b, wg, ba, bi, lam, w_out)


def _ffn_kernel(x_ref, g_ref, win_ref, wout_ref, gf_ref, o_ref, act_ref, *, final, precision):
    cdt = win_ref.dtype
    x = x_ref[...]
    xn = _rms(x, g_ref[...]).astype(cdt)
    for c in range(D_FF // FF_CHUNK):
        lo = c * FF_CHUNK
        gate = jnp.dot(xn, win_ref[:, lo:lo + FF_CHUNK], preferred_element_type=F32,
                       precision=precision)
        up = jnp.dot(xn, win_ref[:, D_FF + lo:D_FF + lo + FF_CHUNK], preferred_element_type=F32,
                     precision=precision)
        act_ref[:, lo:lo + FF_CHUNK] = (jax.nn.silu(gate) * up).astype(cdt)
    y = x + jnp.dot(act_ref[...], wout_ref[...], preferred_element_type=F32, precision=precision)
    if final:
        y = _rms(y, gf_ref[...])
    o_ref[...] = y


def _ffn(x, g, w_in, w_out, g_final, layer, *, final, tile):
    n, d = x.shape
    precision = HIGHEST if w_in.dtype == F32 else None
    return pl.pallas_call(
        functools.partial(_ffn_kernel, final=final, precision=precision),
        name="ffn",
        grid=(n // tile,),
        in_specs=[
            pl.BlockSpec((tile, d), lambda i: (i, 0)),
            _resident((1, d)),
            _layer(layer, (d, 2 * D_FF)),
            _layer(layer, (D_FF, d)),
            _resident((1, d)),
        ],
        out_specs=pl.BlockSpec((tile, d), lambda i: (i, 0)),
        out_shape=jax.ShapeDtypeStruct((n, d), F32),
        scratch_shapes=[pltpu.VMEM((tile, D_FF), w_in.dtype)],
        compiler_params=_params("arbitrary"),
    )(x, g, w_in, w_out, g_final)


def _kv_prompt_kernel(x_ref, g_ref, w_ref, c_ref, slo_ref, shi_ref,
                      k_ref, v_ref, kb_ref, vt_ref, km_ref):
    xn = _rms(x_ref[0], g_ref[...]).astype(BF16)
    kv = jnp.dot(xn, w_ref[...], preferred_element_type=F32)
    tile4 = lambda r: jnp.concatenate([r[...]] * N_KV_HEADS, axis=1)
    k = _rope(kv[:, :KV_DIM], tile4(c_ref), tile4(slo_ref), tile4(shi_ref))
    v = kv[:, KV_DIM:]
    k_ref[0] = k
    v_ref[0] = v
    kb_ref[0] = k.astype(BF16)
    for bb in range(KV_TILE // MOBA_BLOCK):
        rows = slice(bb * MOBA_BLOCK, (bb + 1) * MOBA_BLOCK)
        vt_ref[0, bb] = v[rows, :].T.astype(BF16)
        km_ref[0, bb] = jnp.sum(k[rows, :], axis=0, keepdims=True) * (1.0 / MOBA_BLOCK)


def _kv_prompt(x, g, w_kv, c, s_lo, s_hi):
    bsz, t, d = x.shape
    tt = KV_TILE
    nbt = tt // MOBA_BLOCK
    nb = t // MOBA_BLOCK
    tab = pl.BlockSpec((tt, HEAD_DIM), lambda b, i: (i, 0))
    row_spec = pl.BlockSpec((1, tt, KV_DIM), lambda b, i: (b, i, 0))
    return pl.pallas_call(
        _kv_prompt_kernel,
        name="kv_prompt",
        grid=(bsz, t // tt),
        in_specs=[
            pl.BlockSpec((1, tt, d), lambda b, i: (b, i, 0)),
            _resident((1, d)),
            _resident((d, 2 * KV_DIM)),
            tab, tab, tab,
        ],
        out_specs=[
            row_spec, row_spec, row_spec,
            pl.BlockSpec((1, nbt, KV_DIM, MOBA_BLOCK), lambda b, i: (b, i, 0, 0)),
            pl.BlockSpec((1, nbt, 1, KV_DIM), lambda b, i: (b, i, 0, 0)),
        ],
        out_shape=[
            jax.ShapeDtypeStruct((bsz, t, KV_DIM), F32),
            jax.ShapeDtypeStruct((bsz, t, KV_DIM), F32),
            jax.ShapeDtypeStruct((bsz, t, KV_DIM), BF16),
            jax.ShapeDtypeStruct((bsz, nb, KV_DIM, MOBA_BLOCK), BF16),
            jax.ShapeDtypeStruct((bsz, nb, 1, KV_DIM), F32),
        ],
        compiler_params=_params("arbitrary", "arbitrary"),
    )(x, g, w_kv, c, s_lo, s_hi)


def _select_bias(gate, n_allowed, n_valid):
    nb, nq = gate.shape
    rowi = lax.broadcasted_iota(jnp.int32, (nb, nq), 0).astype(F32)
    g = jnp.where(rowi < n_allowed, gate, NEG)
    bias = jnp.where(rowi == n_allowed, 0.0, NEG)
    for r in range(MOBA_TOPK):
        mx = jnp.max(g, axis=0, keepdims=True)
        idx = jnp.min(jnp.where(g == mx, rowi, float(nb)), axis=0, keepdims=True)
        ok = (n_valid > r).astype(F32)
        idx = idx * ok - (1.0 - ok)
        pick = rowi == idx
        bias = jnp.where(pick, 0.0, bias)
        g = jnp.where(pick, NEG, g)
    return bias


def _attn_prompt_kernel(x_ref, g_ref, wq_ref, wo_ref, c_ref, slo_ref, shi_ref, k_ref, vt_ref,
                        km_ref, o_ref, q2s_ref, s0_ref, s1_ref, m_ref, acc_ref, attn_ref):
    tq = MOBA_BLOCK
    nq = GROUP * tq
    i = pl.program_id(1)
    x = x_ref[0]
    xn = _rms(x, g_ref[...]).astype(BF16)
    q = jnp.dot(xn, wq_ref[...], preferred_element_type=F32)
    c, s_lo, s_hi = c_ref[...], slo_ref[...], shi_ref[...]
    own = i.astype(F32)

    key_pos = lax.broadcasted_iota(jnp.int32, (tq, nq), 0)
    qry_pos = lax.broadcasted_iota(jnp.int32, (tq, nq), 1) % tq
    causal = key_pos <= qry_pos
    kv_lanes = [slice(kvh * HEAD_DIM, (kvh + 1) * HEAD_DIM) for kvh in range(N_KV_HEADS)]

    ones_rows = jnp.where(lax.broadcasted_iota(jnp.int32, (BF16_ROWS, tq), 0) == 0, 1.0, 0.0
                          ).astype(BF16)

    def vt_ones(j, lanes):
        return jnp.concatenate([vt_ref[0, j, lanes, :], ones_rows], axis=0)

    for kvh, lanes in enumerate(kv_lanes):
        q2 = jnp.concatenate(
            [_rope(q[:, h * HEAD_DIM:(h + 1) * HEAD_DIM], c, s_lo, s_hi)
             for h in range(kvh * GROUP, (kvh + 1) * GROUP)], axis=0)
        gate = lax.dot_general(km_ref[0, :, lanes], q2, NT_DIMS, precision=HIGHEST,
                               preferred_element_type=F32)
        bias = _select_bias(gate, own, own)
        bias_t = jnp.concatenate(
            [bias, jnp.zeros((HEAD_DIM - bias.shape[0], nq), F32)], axis=0).T
        q2s = (q2 * (SM_SCALE * LOG2E)).astype(BF16)
        q2s_ref[kvh] = jnp.concatenate([q2s, bias_t.astype(BF16)], axis=1)
        m_ref[kvh] = jnp.full((1, nq), NEG, F32)
        acc_ref[kvh] = jnp.zeros(acc_ref.shape[1:], F32)

    block_lane = lax.broadcasted_iota(jnp.int32, (1, HEAD_DIM), 1)

    def qk(j, s_out):
        one_hot = jnp.broadcast_to(jnp.where(block_lane == j, 1.0, 0.0).astype(BF16),
                                   (tq, HEAD_DIM))
        for kvh, lanes in enumerate(kv_lanes):
            k_aug = jnp.concatenate([k_ref[0, j, :, lanes], one_hot], axis=1)
            s_out[kvh] = lax.dot_general(k_aug, q2s_ref[kvh], NT_DIMS,
                                         preferred_element_type=F32)

    def block(j, s_cur, s_nxt):
        if s_nxt is not None:
            qk(j + 1, s_nxt)
        for kvh, lanes in enumerate(kv_lanes):
            s = s_cur[kvh]
            if s_nxt is None:
                s = jnp.where(causal, s, NEG)
            m = m_ref[kvh]
            m_new = jnp.maximum(m, jnp.max(s, axis=0, keepdims=True))
            alpha = jnp.exp2(m - m_new)
            p = jnp.exp2(s - m_new)
            m_ref[kvh] = m_new
            acc_ref[kvh] = alpha * acc_ref[kvh] + jnp.dot(
                vt_ones(j, lanes), p.astype(BF16), preferred_element_type=F32)

    qk(0, s0_ref)

    def two_blocks(t, carry):
        block(2 * t, s0_ref, s1_ref)
        block(2 * t + 1, s1_ref, s0_ref)
        return carry

    lax.fori_loop(0, i // 2, two_blocks, 0)

    @pl.when(i % 2 == 1)
    def _():
        block(i - 1, s0_ref, s1_ref)
        block(i, s1_ref, None)

    @pl.when(i % 2 == 0)
    def _():
        block(i, s0_ref, None)

    for kvh in range(N_KV_HEADS):
        o = (acc_ref[kvh, :HEAD_DIM, :] / acc_ref[kvh, HEAD_DIM:HEAD_DIM + 1, :]).T
        for j in range(GROUP):
            h = kvh * GROUP + j
            attn_ref[:, h * HEAD_DIM:(h + 1) * HEAD_DIM] = o[j * tq:(j + 1) * tq, :].astype(BF16)

    o_ref[0] = x + jnp.dot(attn_ref[...], wo_ref[...], preferred_element_type=F32)


def _attn_prompt(x, g, w_q, w_o, c, s_lo, s_hi, kb, vt, km):
    bsz, t, d = x.shape
    tq = MOBA_BLOCK
    nb = t // tq
    tab = pl.BlockSpec((tq, HEAD_DIM), lambda b, i: (i, 0))
    per_seq = lambda shape: pl.BlockSpec((1,) + shape, lambda b, i: (b,) + (0,) * len(shape),
                                         pipeline_mode=pl.Buffered(1))
    return pl.pallas_call(
        _attn_prompt_kernel,
        name="attn_prompt",
        grid=(bsz, nb),
        in_specs=[
            pl.BlockSpec((1, tq, d), lambda b, i: (b, i, 0)),
            _resident((1, d)),
            _resident((d, d)),
            _resident((d, d)),
            tab, tab, tab,
            per_seq((nb, tq, KV_DIM)),
            per_seq((nb, KV_DIM, tq)),
            per_seq((nb, KV_DIM)),
        ],
        out_specs=pl.BlockSpec((1, tq, d), lambda b, i: (b, i, 0)),
        out_shape=jax.ShapeDtypeStruct((bsz, t, d), F32),
        scratch_shapes=[
            pltpu.VMEM((N_KV_HEADS, GROUP * tq, 2 * HEAD_DIM), BF16),
            pltpu.VMEM((N_KV_HEADS, tq, GROUP * tq), F32),
            pltpu.VMEM((N_KV_HEADS, tq, GROUP * tq), F32),
            pltpu.VMEM((N_KV_HEADS, 1, GROUP * tq), F32),
            pltpu.VMEM((N_KV_HEADS, HEAD_DIM + BF16_ROWS, GROUP * tq), F32),
            pltpu.VMEM((tq, d), BF16),
        ],
        compiler_params=_params("arbitrary", "arbitrary"),
    )(x, g, w_q, w_o, c, s_lo, s_hi, kb, vt, km)


def _dot_hi(a, b):
    return jnp.dot(a, b, preferred_element_type=F32, precision=HIGHEST)


def _rg_sample_kernel(x_ref, cs_ref, h0_ref, g_ref, win_ref, cw_ref, cb_ref, wa_ref, wi_ref,
                      ba_ref, bi_ref, lam_ref, wout_ref, xo_ref, cso_ref, ho_ref):
    x = x_ref[...]
    proj = _dot_hi(_rms(x, g_ref[...]), win_ref[...])
    gate, u = proj[:, :LRU_WIDTH], proj[:, LRU_WIDTH:]
    uc = cb_ref[...] + cs_ref[0] * cw_ref[0:1, :]
    for j in range(1, CONV_WIDTH - 1):
        uc = uc + cs_ref[j] * cw_ref[j:j + 1, :]
    uc = uc + u * cw_ref[CONV_WIDTH - 1:CONV_WIDTH, :]
    for j in range(CONV_WIDTH - 2):
        cso_ref[j] = cs_ref[j + 1]
    cso_ref[CONV_WIDTH - 2] = u

    sp = _softplus(-lam_ref[...])
    h0 = h0_ref[...]
    hs = []
    for n in range(N_GATE_BLOCKS):
        sl = slice(n * GATE_BLOCK, (n + 1) * GATE_BLOCK)
        ucn = uc[:, sl]
        a, b = _lru_coeffs(_dot_hi(ucn, wa_ref[n]) + ba_ref[:, sl],
                           _dot_hi(ucn, wi_ref[n]) + bi_ref[:, sl], ucn, sp[:, sl])
        hs.append(b + a * h0[:, sl])
    h = jnp.concatenate(hs, axis=1)
    ho_ref[...] = h
    xo_ref[...] = x + _dot_hi(h * jax.nn.gelu(gate), wout_ref[...])


def _rg_sample(x, cs, h0, g, w_in, conv_w, conv_b, wa, wi, ba, bi, lam, w_out, layer):
    n, d = x.shape
    vec = _resident((1, LRU_WIDTH))
    gates = _layer(layer, (N_GATE_BLOCKS, GATE_BLOCK, GATE_BLOCK))
    return pl.pallas_call(
        _rg_sample_kernel,
        name="rg_sample",
        grid=(1,),
        in_specs=[
            _resident((n, d)), _resident((CONV_WIDTH - 1, n, LRU_WIDTH)), _resident((n, LRU_WIDTH)),
            _resident((1, d)), _layer(layer, (d, 2 * LRU_WIDTH)), _resident((CONV_WIDTH, LRU_WIDTH)),
            vec, gates, gates, vec, vec, vec, _layer(layer, (LRU_WIDTH, d)),
        ],
        out_specs=[_whole((n, d)), _whole((CONV_WIDTH - 1, n, LRU_WIDTH)), _whole((n, LRU_WIDTH))],
        out_shape=[
            jax.ShapeDtypeStruct((n, d), F32),
            jax.ShapeDtypeStruct((CONV_WIDTH - 1, n, LRU_WIDTH), F32),
            jax.ShapeDtypeStruct((n, LRU_WIDTH), F32),
        ],
        compiler_params=_params("arbitrary"),
    )(x, cs, h0, g, w_in, conv_w, conv_b, wa, wi, ba, bi, lam, w_out)


def _proj_sample_kernel(x_ref, g_ref, w_ref, c_ref, slo_ref, shi_ref, o_ref, *, n_rope_heads):
    y = _dot_hi(_rms(x_ref[...], g_ref[...]), w_ref[...])
    if n_rope_heads:
        tile = lambda r: jnp.concatenate([r[...]] * n_rope_heads, axis=1)
        width = n_rope_heads * HEAD_DIM
        rot = _rope(y[:, :width], tile(c_ref), tile(slo_ref), tile(shi_ref))
        y = jnp.concatenate([rot, y[:, width:]], axis=1) if width < y.shape[1] else rot
    o_ref[...] = y


def _proj_sample(x, g, w, layer, c, s_lo, s_hi, n_rope_heads):
    n, d = x.shape
    width = w.shape[2]
    tab = _resident((1, HEAD_DIM))
    return pl.pallas_call(
        functools.partial(_proj_sample_kernel, n_rope_heads=n_rope_heads),
        name="proj_sample",
        grid=(1,),
        in_specs=[_resident((n, d)), _resident((1, d)), _layer(layer, (d, width)), tab, tab, tab],
        out_specs=_whole((n, width)),
        out_shape=jax.ShapeDtypeStruct((n, width), F32),
        compiler_params=_params("arbitrary"),
    )(x, g, w, c, s_lo, s_hi)


def _kmeans_sample_kernel(pt_ref, *refs):
    page_refs, o_ref = refs[:-1], refs[-1]
    step = pl.program_id(1)
    blocks = len(page_refs) // PAGES_PER_BLOCK
    means = []
    for blk in range(blocks):
        pages = page_refs[blk * PAGES_PER_BLOCK:(blk + 1) * PAGES_PER_BLOCK]
        s = sum(jnp.sum(p[0], axis=0) for p in pages)
        s = sum(s[par * N_KV_HEADS:(par + 1) * N_KV_HEADS] for par in range(ROW_PARITIES))
        means.append(jnp.concatenate([s[kvh:kvh + 1, :] for kvh in range(N_KV_HEADS)], axis=1))
    first = pl.multiple_of(step * blocks, blocks)
    o_ref[0, pl.ds(first, blocks), :] = jnp.concatenate(means, axis=0) * (1.0 / MOBA_BLOCK)


def _kmeans_sample(page_table_flat, cache_k, n_seq, n_pages):
    nb = n_pages // PAGES_PER_BLOCK

    def page(k):
        return pl.BlockSpec(
            (1,) + PAGE_VIEW,
            lambda b, n, pt: (pt[b * n_pages + KMEANS_PAGES * n + k], 0, 0, 0))

    return pl.pallas_call(
        _kmeans_sample_kernel,
        name="kmeans_sample",
        grid_spec=pltpu.PrefetchScalarGridSpec(
            num_scalar_prefetch=1,
            grid=(n_seq, n_pages // KMEANS_PAGES),
            in_specs=[page(k) for k in range(KMEANS_PAGES)],
            out_specs=pl.BlockSpec((1, nb, KV_DIM), lambda b, n, pt: (b, 0, 0)),
        ),
        out_shape=jax.ShapeDtypeStruct((n_seq, nb, KV_DIM), F32),
        compiler_params=_params("arbitrary", "arbitrary"),
    )(page_table_flat, *([cache_k] * KMEANS_PAGES))


def _gate_sample_kernel(q_ref, km_ref, sel_ref):
    km = km_ref[...]
    n_seq, nb, _ = km.shape
    blk = lax.broadcasted_iota(jnp.int32, (n_seq, nb, 1), 1).astype(F32)
    lane = lax.broadcasted_iota(jnp.int32, (n_seq, 1, HEAD_DIM), 2)
    for h in range(N_HEADS):
        kvh = h // GROUP
        qh = q_ref[:, :, h * HEAD_DIM:(h + 1) * HEAD_DIM]
        g = jnp.sum(km[:, :, kvh * HEAD_DIM:(kvh + 1) * HEAD_DIM] * qh, axis=-1, keepdims=True)
        out = jnp.zeros((n_seq, 1, HEAD_DIM), F32)
        for r in range(MOBA_TOPK):
            mx = jnp.max(g, axis=1, keepdims=True)
            idx = jnp.min(jnp.where(g == mx, blk, float(nb)), axis=1, keepdims=True)
            g = jnp.where(blk == idx, NEG, g)
            out = jnp.where(lane == r, idx, out)
        sel_ref[h] = out.astype(jnp.int32)


def _gate_sample(q3, km):
    n_seq = q3.shape[0]
    return pl.pallas_call(
        _gate_sample_kernel,
        name="gate_sample",
        out_shape=jax.ShapeDtypeStruct((N_HEADS, n_seq, 1, HEAD_DIM), jnp.int32),
        compiler_params=pltpu.CompilerParams(vmem_limit_bytes=VMEM_LIMIT),
    )(q3, km)


def _attn_sample_kernel(pt_ref, sel_ref, q_ref, kn_ref, vn_ref, ck_ref, cv_ref, o_ref,
                        kbuf, vbuf, sem, *, n_pages):
    b = pl.program_id(0)
    n_slots = MOBA_TOPK * PAGES_PER_BLOCK
    rows = PAGE_SIZE // ROW_PARITIES

    def copies(seq, buf):
        out = []
        for h in range(N_HEADS):
            kvh = h // GROUP
            for r in range(MOBA_TOPK):
                blk = sel_ref[(seq * N_HEADS + h) * MOBA_TOPK + r]
                for half in range(PAGES_PER_BLOCK):
                    page = pt_ref[seq * n_pages + PAGES_PER_BLOCK * blk + half]
                    slot = r * PAGES_PER_BLOCK + half
                    for par in range(ROW_PARITIES):
                        sub = par * N_KV_HEADS + kvh
                        dst = (buf, h, slot, pl.ds(par * rows, rows))
                        out.append(pltpu.make_async_copy(ck_ref.at[page, :, sub], kbuf.at[dst], sem.at[0, buf]))
                        out.append(pltpu.make_async_copy(cv_ref.at[page, :, sub], vbuf.at[dst], sem.at[1, buf]))
        return out

    cur = b % 2

    @pl.when(b == 0)
    def _():
        for cp in copies(b, cur):
            cp.start()

    @pl.when(b + 1 < pl.num_programs(0))
    def _():
        for cp in copies(b + 1, 1 - cur):
            cp.start()

    for cp in copies(b, cur):
        cp.wait()

    for h in range(N_HEADS):
        kvh = h // GROUP
        qh = q_ref[0, :, h * HEAD_DIM:(h + 1) * HEAD_DIM] * SM_SCALE
        kn = kn_ref[0, :, kvh * HEAD_DIM:(kvh + 1) * HEAD_DIM]
        vn = vn_ref[0, :, kvh * HEAD_DIM:(kvh + 1) * HEAD_DIM]
        kk = kbuf[cur, h].reshape(n_slots * PAGE_SIZE, HEAD_DIM)
        vv = vbuf[cur, h].reshape(n_slots * PAGE_SIZE, HEAD_DIM)
        s = jnp.sum(kk * qh, axis=-1, keepdims=True)
        s_new = jnp.sum(kn * qh, axis=-1, keepdims=True)
        m = jnp.maximum(jnp.max(s, axis=0, keepdims=True), s_new)
        p = jnp.exp(s - m)
        p_new = jnp.exp(s_new - m)
        l = jnp.sum(p, axis=0, keepdims=True) + p_new
        acc = jnp.sum(p * vv, axis=0, keepdims=True) + p_new * vn
        o_ref[0, :, h * HEAD_DIM:(h + 1) * HEAD_DIM] = acc / l


def _attn_sample(page_table_flat, sel_flat, q3, k_new3, v_new3, cache_k, cache_v, n_pages):
    n_seq = q3.shape[0]
    n_slots = MOBA_TOPK * PAGES_PER_BLOCK
    row = lambda w: pl.BlockSpec((1, 1, w), lambda b, pt, sel: (b, 0, 0))
    return pl.pallas_call(
        functools.partial(_attn_sample_kernel, n_pages=n_pages),
        name="attn_sample",
        grid_spec=pltpu.PrefetchScalarGridSpec(
            num_scalar_prefetch=2,
            grid=(n_seq,),
            in_specs=[row(D_MODEL), row(KV_DIM), row(KV_DIM),
                      pl.BlockSpec(memory_space=pl.ANY), pl.BlockSpec(memory_space=pl.ANY)],
            out_specs=row(D_MODEL),
            scratch_shapes=[
                pltpu.VMEM((2, N_HEADS, n_slots, PAGE_SIZE, HEAD_DIM), F32),
                pltpu.VMEM((2, N_HEADS, n_slots, PAGE_SIZE, HEAD_DIM), F32),
                pltpu.SemaphoreType.DMA((2, 2)),
            ],
        ),
        out_shape=jax.ShapeDtypeStruct((n_seq, 1, D_MODEL), F32),
        compiler_params=_params("arbitrary"),
    )(page_table_flat, sel_flat, q3, k_new3, v_new3, cache_k, cache_v)


def _oproj_sample_kernel(x_ref, a_ref, w_ref, o_ref):
    o_ref[...] = x_ref[...] + _dot_hi(a_ref[...], w_ref[...])


def _oproj_sample(x, attn, w_o, layer):
    n, d = x.shape
    return pl.pallas_call(
        _oproj_sample_kernel,
        name="oproj_sample",
        grid=(1,),
        in_specs=[_resident((n, d)), _resident((n, d)), _layer(layer, (d, d))],
        out_specs=_whole((n, d)),
        out_shape=jax.ShapeDtypeStruct((n, d), F32),
        compiler_params=_params("arbitrary"),
    )(x, attn, w_o)


def _prompt_trunk(x, w):
    bsz, t, d = x.shape
    c, s_lo, s_hi = _rope_tables(jnp.arange(t, dtype=jnp.int32))
    row = lambda a: a.reshape(1, -1)
    bf = lambda a: a.astype(BF16)
    convs, hs = [], []
    n_rg = w["w_rg_in"].shape[0]
    depth = w["g_mix"].shape[0]

    w_ffn_in, w_ffn_out = bf(w["w_ffn_in"]), bf(w["w_ffn_out"])

    def ffn(x, l):
        y = _ffn(x.reshape(bsz * t, d), row(w["g_ffn"][l]), w_ffn_in, w_ffn_out,
                 row(w["g_final"]), l, final=(l == depth - 1), tile=FFN_TILE)
        return y.reshape(bsz, t, d)

    for l in range(n_rg):
        wg = jnp.concatenate([w["rg_gate_a_w"][l], w["rg_gate_i_w"][l]], axis=-1)
        x, conv, h = _rg_prompt(
            x, row(w["g_mix"][l]), bf(w["w_rg_in"][l]), w["rg_conv_w"][l], row(w["rg_conv_b"][l]),
            bf(wg), row(w["rg_gate_a_b"][l]), row(w["rg_gate_i_b"][l]), row(w["rg_lambda"][l]),
            bf(w["w_rg_out"][l]))
        convs.append(conv)
        hs.append(h[:, 0, :])
        x = ffn(x, l)

    k, v, kb, vt, km = _kv_prompt(x, row(w["g_kv"]), bf(w["w_kv"]), c, s_lo, s_hi)
    nb = t // MOBA_BLOCK
    kb = kb.reshape(bsz, nb, MOBA_BLOCK, KV_DIM)
    km = km.reshape(bsz, nb, KV_DIM)
    for a in range(depth - n_rg):
        l = n_rg + a
        x = _attn_prompt(x, row(w["g_mix"][l]), bf(w["w_q"][a]), bf(w["w_o"][a]), c, s_lo, s_hi, kb, vt, km)
        x = ffn(x, l)
    shape4 = (bsz, t, N_KV_HEADS, HEAD_DIM)
    return x, jnp.stack(convs), jnp.stack(hs), k.reshape(shape4), v.reshape(shape4)


def _sample_trunk(x3, state_conv, state_h, cache_k, cache_v, page_table, w):
    n_seq, _, d = x3.shape
    n_pages = page_table.shape[1]
    past_len = n_pages * PAGE_SIZE
    c, s_lo, s_hi = _rope_tables(jnp.full((1,), past_len, jnp.int32))
    row = lambda a: a.reshape(1, -1)
    n_rg = w["w_rg_in"].shape[0]
    depth = w["g_mix"].shape[0]
    x = x3.reshape(n_seq, d)
    pt_flat = page_table.reshape(-1)
    ck = cache_k.reshape((cache_k.shape[0],) + PAGE_VIEW)
    cv = cache_v.reshape((cache_v.shape[0],) + PAGE_VIEW)
    convs, hs = [], []

    def ffn(x, l):
        return _ffn(x, row(w["g_ffn"][l]), w["w_ffn_in"], w["w_ffn_out"], row(w["g_final"]), l,
                    final=(l == depth - 1), tile=n_seq)

    for l in range(n_rg):
        x, conv, h = _rg_sample(
            x, jnp.swapaxes(state_conv[l], 0, 1), state_h[l], row(w["g_mix"][l]), w["w_rg_in"],
            w["rg_conv_w"][l], row(w["rg_conv_b"][l]), w["rg_gate_a_w"], w["rg_gate_i_w"],
            row(w["rg_gate_a_b"][l]), row(w["rg_gate_i_b"][l]), row(w["rg_lambda"][l]), w["w_rg_out"], l)
        convs.append(jnp.swapaxes(conv, 0, 1))
        hs.append(h)
        x = ffn(x, l)

    kv = _proj_sample(x, row(w["g_kv"]), w["w_kv"][None], 0, c, s_lo, s_hi, N_KV_HEADS)
    k_new, v_new = kv[:, :KV_DIM], kv[:, KV_DIM:]
    km = _kmeans_sample(pt_flat, ck, n_seq, n_pages)
    for a in range(depth - n_rg):
        l = n_rg + a
        q = _proj_sample(x, row(w["g_mix"][l]), w["w_q"], a, c, s_lo, s_hi, N_HEADS)
        q3 = q.reshape(n_seq, 1, d)
        sel = _gate_sample(q3, km)[:, :, 0, :MOBA_TOPK]
        sel_flat = jnp.transpose(sel, (1, 0, 2)).reshape(-1)
        attn = _attn_sample(pt_flat, sel_flat, q3, k_new.reshape(n_seq, 1, KV_DIM),
                            v_new.reshape(n_seq, 1, KV_DIM), ck, cv, n_pages)
        x = _oproj_sample(x, attn.reshape(n_seq, d), w["w_o"], a)
        x = ffn(x, l)
    shape4 = (n_seq, 1, N_KV_HEADS, HEAD_DIM)
    return (x.reshape(n_seq, 1, d), jnp.stack(convs), jnp.stack(hs),
            k_new.reshape(shape4), v_new.reshape(shape4))


def kernel(x_prompt, x_sample, cache_k, cache_v, state_conv, state_h, page_table,
           g_mix, g_ffn, w_rg_in, rg_conv_w, rg_conv_b, rg_gate_a_w, rg_gate_a_b,
           rg_gate_i_w, rg_gate_i_b, rg_lambda, w_rg_out, g_kv, w_kv, w_q, w_o,
           w_ffn_in, w_ffn_out, g_final):
    assert x_sample.shape[1] == 1 and x_prompt.shape[1] % MOBA_BLOCK == 0
    assert MOBA_TOPK <= page_table.shape[1] // PAGES_PER_BLOCK
    w = dict(g_mix=g_mix, g_ffn=g_ffn, w_rg_in=w_rg_in, rg_conv_w=rg_conv_w, rg_conv_b=rg_conv_b,
             rg_gate_a_w=rg_gate_a_w, rg_gate_a_b=rg_gate_a_b, rg_gate_i_w=rg_gate_i_w,
             rg_gate_i_b=rg_gate_i_b, rg_lambda=rg_lambda, w_rg_out=w_rg_out, g_kv=g_kv, w_kv=w_kv,
             w_q=w_q, w_o=w_o, w_ffn_in=w_ffn_in, w_ffn_out=w_ffn_out, g_final=g_final)
    y_p, p_conv, p_h, p_k, p_v = _prompt_trunk(x_prompt, w)
    y_s, s_conv, s_h, s_k, s_v = _sample_trunk(x_sample, state_conv, state_h, cache_k, cache_v,
                                               page_table, w)
    return (y_p, y_s, p_conv, p_h, p_k, p_v, s_conv, s_h, s_k, s_v)
```

```python
import functools
import math

import jax
import jax.numpy as jnp
from jax import lax
from jax.experimental import pallas as pl
from jax.experimental.pallas import tpu as pltpu

D_MODEL = 1024
LRU_WIDTH = D_MODEL
N_GATE_BLOCKS = 8
GATE_BLOCK = LRU_WIDTH // N_GATE_BLOCKS
CONV_WIDTH = 4
LRU_C = 8.0
N_HEADS = 8
HEAD_DIM = D_MODEL // N_HEADS
N_KV_HEADS = 4
KV_DIM = N_KV_HEADS * HEAD_DIM
GROUP = N_HEADS // N_KV_HEADS
ROT_DIM = HEAD_DIM // 4
ROPE_THETA = 500000.0
MOBA_BLOCK = 256
MOBA_TOPK = 3
PAGE_SIZE = 128
PAGES_PER_BLOCK = MOBA_BLOCK // PAGE_SIZE
D_FF = -(-8 * D_MODEL // (3 * 256)) * 256
EPS = 1e-6
NEG = -1e30
SM_SCALE = 1.0 / math.sqrt(HEAD_DIM)
LOG2E = math.log2(math.e)

F32 = jnp.float32
BF16 = jnp.bfloat16
HIGHEST = lax.Precision.HIGHEST

SUBLANES = 8
BF16_ROWS = 16
FF_CHUNK = 256
RG_TILE = 256
FFN_TILE = 512
KV_TILE = 512
KMEANS_PAGES = 16
ATTN_UNROLL = 4
ROW_PARITIES = SUBLANES // N_KV_HEADS
PAGE_VIEW = (PAGE_SIZE // ROW_PARITIES, SUBLANES, HEAD_DIM)
VMEM_LIMIT = 56 * 1024 * 1024

NT_DIMS = (((1,), (1,)), ((), ()))


def _params(*sem):
    return pltpu.CompilerParams(dimension_semantics=sem, vmem_limit_bytes=VMEM_LIMIT)


def _resident(shape):
    zeros = (0,) * len(shape)
    return pl.BlockSpec(shape, lambda *_: zeros, pipeline_mode=pl.Buffered(1))


def _whole(shape):
    zeros = (0,) * len(shape)
    return pl.BlockSpec(shape, lambda *_: zeros)


def _layer(layer, shape):
    zeros = (0,) * len(shape)
    return pl.BlockSpec((None,) + shape, lambda *_: (layer,) + zeros, pipeline_mode=pl.Buffered(1))


def _rms(x, g):
    ms = jnp.mean(x * x, axis=-1, keepdims=True)
    return x * lax.rsqrt(ms + EPS) * g


def _softplus(z):
    return jnp.maximum(z, 0.0) + jnp.log1p(jnp.exp(-jnp.abs(z)))


def _rope(x, c, s_lo, s_hi):
    n = x.shape[-1]
    half = ROT_DIM // 2
    return x * c + pltpu.roll(x, half, 1) * s_hi + pltpu.roll(x, n - half, 1) * s_lo


def _rope_tables(pos):
    half = ROT_DIM // 2
    inv = ROPE_THETA ** (-jnp.arange(half, dtype=F32) / half)
    ang = pos.astype(F32)[:, None] * inv[None, :]
    cos, sin = jnp.cos(ang), jnp.sin(ang)
    t = pos.shape[0]
    rest = HEAD_DIM - ROT_DIM
    c = jnp.concatenate([cos, cos, jnp.ones((t, rest), F32)], axis=1)
    s_hi = jnp.concatenate([jnp.zeros((t, half), F32), sin, jnp.zeros((t, rest), F32)], axis=1)
    s_lo = jnp.concatenate([-sin, jnp.zeros((t, half + rest), F32)], axis=1)
    return c, s_lo, s_hi


def _lru_coeffs(z_a, z_i, uc, sp):
    r = jax.nn.sigmoid(z_a)
    i = jax.nn.sigmoid(z_i)
    log_a = (-LRU_C * r) * sp
    a = jnp.exp(log_a)
    b = jnp.sqrt(-jnp.tanh(log_a) * (a * a + 1.0)) * (i * uc)
    return a, b


def _rg_prompt_kernel(x_ref, g_ref, win_ref, cw_ref, cb_ref, wg_ref, ba_ref, bi_ref, lam_ref,
                      wout_ref, xo_ref, conv_ref, hl_ref, ext_ref, a_ref, b_ref, gg_ref, hc_ref):
    tt = RG_TILE
    t = pl.program_id(1)

    @pl.when(t == 0)
    def _():
        ext_ref[0:SUBLANES, :] = jnp.zeros((SUBLANES, LRU_WIDTH), F32)
        hc_ref[...] = jnp.zeros_like(hc_ref)

    x = x_ref[0]
    xn = _rms(x, g_ref[...]).astype(BF16)
    proj = jnp.dot(xn, win_ref[...], preferred_element_type=F32)
    gg_ref[...] = jax.nn.gelu(proj[:, :LRU_WIDTH])
    ext_ref[SUBLANES:SUBLANES + tt, :] = proj[:, LRU_WIDTH:]

    base = SUBLANES - (CONV_WIDTH - 1)
    uc = cb_ref[...] + ext_ref[base:base + tt, :] * cw_ref[0:1, :]
    for j in range(1, CONV_WIDTH):
        uc = uc + ext_ref[base + j:base + j + tt, :] * cw_ref[j:j + 1, :]
    conv_ref[0] = ext_ref[tt + base:tt + SUBLANES, :]
    ext_ref[0:SUBLANES, :] = ext_ref[tt:tt + SUBLANES, :]

    sp = _softplus(-lam_ref[...])
    for n in range(N_GATE_BLOCKS):
        sl = slice(n * GATE_BLOCK, (n + 1) * GATE_BLOCK)
        ucn = uc[:, sl]
        z = jnp.dot(ucn.astype(BF16), wg_ref[n], preferred_element_type=F32)
        a, b = _lru_coeffs(z[:, :GATE_BLOCK] + ba_ref[:, sl], z[:, GATE_BLOCK:] + bi_ref[:, sl],
                           ucn, sp[:, sl])
        a_ref[:, sl] = a
        b_ref[:, sl] = b

    row = lax.broadcasted_iota(jnp.int32, (SUBLANES, LRU_WIDTH), 0)

    def group(gi, h_prev):
        r0 = pl.multiple_of(gi * SUBLANES, SUBLANES)
        a8 = a_ref[pl.ds(r0, SUBLANES), :]
        b8 = b_ref[pl.ds(r0, SUBLANES), :]
        s = 1
        while s < SUBLANES:
            keep = row >= s
            b8 = jnp.where(keep, a8 * pltpu.roll(b8, s, 0) + b8, b8)
            a8 = jnp.where(keep, a8 * pltpu.roll(a8, s, 0), a8)
            s *= 2
        h = a8 * h_prev + b8
        b_ref[pl.ds(r0, SUBLANES), :] = h
        return h[SUBLANES - 1:SUBLANES, :]

    h_last = lax.fori_loop(0, tt // SUBLANES, group, hc_ref[...])
    hc_ref[...] = h_last
    hl_ref[0] = h_last

    y = jnp.dot((b_ref[...] * gg_ref[...]).astype(BF16), wout_ref[...], preferred_element_type=F32)
    xo_ref[0] = x + y


def _rg_prompt(x, g, w_in, conv_w, conv_b, wg, ba, bi, lam, w_out):
    bsz, t, d = x.shape
    tt = RG_TILE
    vec = lambda n: _resident((1, n))
    return pl.pallas_call(
        _rg_prompt_kernel,
        name="rg_prompt",
        grid=(bsz, t // tt),
        in_specs=[
            pl.BlockSpec((1, tt, d), lambda b, i: (b, i, 0)),
            vec(d),
            _resident((d, 2 * LRU_WIDTH)),
            _resident((CONV_WIDTH, LRU_WIDTH)),
            vec(LRU_WIDTH),
            _resident((N_GATE_BLOCKS, GATE_BLOCK, 2 * GATE_BLOCK)),
            vec(LRU_WIDTH), vec(LRU_WIDTH), vec(LRU_WIDTH),
            _resident((LRU_WIDTH, d)),
        ],
        out_specs=[
            pl.BlockSpec((1, tt, d), lambda b, i: (b, i, 0)),
            pl.BlockSpec((1, CONV_WIDTH - 1, LRU_WIDTH), lambda b, i: (b, 0, 0)),
            pl.BlockSpec((1, 1, LRU_WIDTH), lambda b, i: (b, 0, 0)),
        ],
        out_shape=[
            jax.ShapeDtypeStruct((bsz, t, d), F32),
            jax.ShapeDtypeStruct((bsz, CONV_WIDTH - 1, LRU_WIDTH), F32),
            jax.ShapeDtypeStruct((bsz, 1, LRU_WIDTH), F32),
        ],
        scratch_shapes=[
            pltpu.VMEM((tt + SUBLANES, LRU_WIDTH), F32),
            pltpu.VMEM((tt, LRU_WIDTH), F32),
            pltpu.VMEM((tt, LRU_WIDTH), F32),
            pltpu.VMEM((tt, LRU_WIDTH), F32),
            pltpu.VMEM((1, LRU_WIDTH), F32),
        ],
        compiler_params=_params("arbitrary", "arbitrary"),
    )(x, g, w_in, conv_w, conv_b, wg, ba, bi, lam, w_out)


def _ffn_kernel(x_ref, g_ref, win_ref, wout_ref, gf_ref, o_ref, act_ref, *, final, precision):
    cdt = win_ref.dtype
    x = x_ref[...]
    xn = _rms(x, g_ref[...]).astype(cdt)
    for c in range(D_FF // FF_CHUNK):
        lo = c * FF_CHUNK
        gate = jnp.dot(xn, win_ref[:, lo:lo + FF_CHUNK], preferred_element_type=F32,
                       precision=precision)
        up = jnp.dot(xn, win_ref[:, D_FF + lo:D_FF + lo + FF_CHUNK], preferred_element_type=F32,
                     precision=precision)
        act_ref[:, lo:lo + FF_CHUNK] = (jax.nn.silu(gate) * up).astype(cdt)
    y = x + jnp.dot(act_ref[...], wout_ref[...], preferred_element_type=F32, precision=precision)
    if final:
        y = _rms(y, gf_ref[...])
    o_ref[...] = y


def _ffn(x, g, w_in, w_out, g_final, layer, *, final, tile):
    n, d = x.shape
    precision = HIGHEST if w_in.dtype == F32 else None
    return pl.pallas_call(
        functools.partial(_ffn_kernel, final=final, precision=precision),
        name="ffn",
        grid=(n // tile,),
        in_specs=[
            pl.BlockSpec((tile, d), lambda i: (i, 0)),
            _resident((1, d)),
            _layer(layer, (d, 2 * D_FF)),
            _layer(layer, (D_FF, d)),
            _resident((1, d)),
        ],
        out_specs=pl.BlockSpec((tile, d), lambda i: (i, 0)),
        out_shape=jax.ShapeDtypeStruct((n, d), F32),
        scratch_shapes=[pltpu.VMEM((tile, D_FF), w_in.dtype)],
        compiler_params=_params("arbitrary"),
    )(x, g, w_in, w_out, g_final)


def _kv_prompt_kernel(x_ref, g_ref, w_ref, c_ref, slo_ref, shi_ref,
                      k_ref, v_ref, kb_ref, vt_ref, km_ref):
    xn = _rms(x_ref[0], g_ref[...]).astype(BF16)
    kv = jnp.dot(xn, w_ref[...], preferred_element_type=F32)
    tile4 = lambda r: jnp.concatenate([r[...]] * N_KV_HEADS, axis=1)
    k = _rope(kv[:, :KV_DIM], tile4(c_ref), tile4(slo_ref), tile4(shi_ref))
    v = kv[:, KV_DIM:]
    k_ref[0] = k
    v_ref[0] = v
    kb_ref[0] = k.astype(BF16)
    for bb in range(KV_TILE // MOBA_BLOCK):
        rows = slice(bb * MOBA_BLOCK, (bb + 1) * MOBA_BLOCK)
        vt_ref[0, bb] = v[rows, :].T.astype(BF16)
        km_ref[0, bb] = jnp.sum(k[rows, :], axis=0, keepdims=True) * (1.0 / MOBA_BLOCK)


def _kv_prompt(x, g, w_kv, c, s_lo, s_hi):
    bsz, t, d = x.shape
    tt = KV_TILE
    nbt = tt // MOBA_BLOCK
    nb = t // MOBA_BLOCK
    tab = pl.BlockSpec((tt, HEAD_DIM), lambda b, i: (i, 0))
    row_spec = pl.BlockSpec((1, tt, KV_DIM), lambda b, i: (b, i, 0))
    return pl.pallas_call(
        _kv_prompt_kernel,
        name="kv_prompt",
        grid=(bsz, t // tt),
        in_specs=[
            pl.BlockSpec((1, tt, d), lambda b, i: (b, i, 0)),
            _resident((1, d)),
            _resident((d, 2 * KV_DIM)),
            tab, tab, tab,
        ],
        out_specs=[
            row_spec, row_spec, row_spec,
            pl.BlockSpec((1, nbt, KV_DIM, MOBA_BLOCK), lambda b, i: (b, i, 0, 0)),
            pl.BlockSpec((1, nbt, 1, KV_DIM), lambda b, i: (b, i, 0, 0)),
        ],
        out_shape=[
            jax.ShapeDtypeStruct((bsz, t, KV_DIM), F32),
            jax.ShapeDtypeStruct((bsz, t, KV_DIM), F32),
            jax.ShapeDtypeStruct((bsz, t, KV_DIM), BF16),
            jax.ShapeDtypeStruct((bsz, nb, KV_DIM, MOBA_BLOCK), BF16),
            jax.ShapeDtypeStruct((bsz, nb, 1, KV_DIM), F32),
        ],
        compiler_params=_params("arbitrary", "arbitrary"),
    )(x, g, w_kv, c, s_lo, s_hi)


def _select_bias(gate, n_allowed, n_valid):
    nb, nq = gate.shape
    rowi = lax.broadcasted_iota(jnp.int32, (nb, nq), 0).astype(F32)
    g = jnp.where(rowi < n_allowed, gate, NEG)
    bias = jnp.where(rowi == n_allowed, 0.0, NEG)
    for r in range(MOBA_TOPK):
        mx = jnp.max(g, axis=0, keepdims=True)
        idx = jnp.min(jnp.where(g == mx, rowi, float(nb)), axis=0, keepdims=True)
        ok = (n_valid > r).astype(F32)
        idx = idx * ok - (1.0 - ok)
        pick = rowi == idx
        bias = jnp.where(pick, 0.0, bias)
        g = jnp.where(pick, NEG, g)
    return bias


def _attn_prompt_kernel(x_ref, g_ref, wq_ref, wo_ref, c_ref, slo_ref, shi_ref, k_ref, vt_ref,
                        km_ref, o_ref, q2s_ref, s0_ref, s1_ref, m_ref, acc_ref, attn_ref):
    tq = MOBA_BLOCK
    nq = GROUP * tq
    i = pl.program_id(1)
    x = x_ref[0]
    xn = _rms(x, g_ref[...]).astype(BF16)
    q = jnp.dot(xn, wq_ref[...], preferred_element_type=F32)
    c, s_lo, s_hi = c_ref[...], slo_ref[...], shi_ref[...]
    own = i.astype(F32)

    key_pos = lax.broadcasted_iota(jnp.int32, (tq, nq), 0)
    qry_pos = lax.broadcasted_iota(jnp.int32, (tq, nq), 1) % tq
    causal = key_pos <= qry_pos
    kv_lanes = [slice(kvh * HEAD_DIM, (kvh + 1) * HEAD_DIM) for kvh in range(N_KV_HEADS)]

    ones_rows = jnp.where(lax.broadcasted_iota(jnp.int32, (BF16_ROWS, tq), 0) == 0, 1.0, 0.0
                          ).astype(BF16)

    def vt_ones(j, lanes):
        return jnp.concatenate([vt_ref[0, j, lanes, :], ones_rows], axis=0)

    for kvh, lanes in enumerate(kv_lanes):
        q2 = jnp.concatenate(
            [_rope(q[:, h * HEAD_DIM:(h + 1) * HEAD_DIM], c, s_lo, s_hi)
             for h in range(kvh * GROUP, (kvh + 1) * GROUP)], axis=0)
        gate = lax.dot_general(km_ref[0, :, lanes], q2, NT_DIMS, precision=HIGHEST,
                               preferred_element_type=F32)
        bias = _select_bias(gate, own, own)
        bias_t = jnp.concatenate(
            [bias, jnp.zeros((HEAD_DIM - bias.shape[0], nq), F32)], axis=0).T
        q2s = (q2 * (SM_SCALE * LOG2E)).astype(BF16)
        q2s_ref[kvh] = jnp.concatenate([q2s, bias_t.astype(BF16)], axis=1)
        m_ref[kvh] = jnp.full((1, nq), NEG, F32)
        acc_ref[kvh] = jnp.zeros(acc_ref.shape[1:], F32)

    block_lane = lax.broadcasted_iota(jnp.int32, (1, HEAD_DIM), 1)

    def qk(j, s_out):
        one_hot = jnp.broadcast_to(jnp.where(block_lane == j, 1.0, 0.0).astype(BF16),
                                   (tq, HEAD_DIM))
        for kvh, lanes in enumerate(kv_lanes):
            k_aug = jnp.concatenate([k_ref[0, j, :, lanes], one_hot], axis=1)
            s_out[kvh] = lax.dot_general(k_aug, q2s_ref[kvh], NT_DIMS,
                                         preferred_element_type=F32)

    def block(j, s_cur, s_nxt):
        if s_nxt is not None:
            qk(j + 1, s_nxt)
        for kvh, lanes in enumerate(kv_lanes):
            s = s_cur[kvh]
            if s_nxt is None:
                s = jnp.where(causal, s, NEG)
            m = m_ref[kvh]
            m_new = jnp.maximum(m, jnp.max(s, axis=0, keepdims=True))
            alpha = jnp.exp2(m - m_new)
            p = jnp.exp2(s - m_new)
            m_ref[kvh] = m_new
            acc_ref[kvh] = alpha * acc_ref[kvh] + jnp.dot(
                vt_ones(j, lanes), p.astype(BF16), preferred_element_type=F32)

    qk(0, s0_ref)
    bufs = (s0_ref, s1_ref)

    def trip(t, carry):
        for k in range(ATTN_UNROLL):
            block(ATTN_UNROLL * t + k, bufs[k % 2], bufs[(k + 1) % 2])
        return carry

    lax.fori_loop(0, i // ATTN_UNROLL, trip, 0)

    for rem in range(ATTN_UNROLL):
        @pl.when(i % ATTN_UNROLL == rem)
        def _(rem=rem):
            for k in range(rem):
                block(i - rem + k, bufs[k % 2], bufs[(k + 1) % 2])
            block(i, bufs[rem % 2], None)

    for kvh in range(N_KV_HEADS):
        o = (acc_ref[kvh, :HEAD_DIM, :] / acc_ref[kvh, HEAD_DIM:HEAD_DIM + 1, :]).T
        for j in range(GROUP):
            h = kvh * GROUP + j
            attn_ref[:, h * HEAD_DIM:(h + 1) * HEAD_DIM] = o[j * tq:(j + 1) * tq, :].astype(BF16)

    o_ref[0] = x + jnp.dot(attn_ref[...], wo_ref[...], preferred_element_type=F32)


def _attn_prompt(x, g, w_q, w_o, c, s_lo, s_hi, kb, vt, km):
    bsz, t, d = x.shape
    tq = MOBA_BLOCK
    nb = t // tq
    tab = pl.BlockSpec((tq, HEAD_DIM), lambda b, i: (i, 0))
    per_seq = lambda shape: pl.BlockSpec((1,) + shape, lambda b, i: (b,) + (0,) * len(shape),
                                         pipeline_mode=pl.Buffered(1))
    return pl.pallas_call(
        _attn_prompt_kernel,
        name="attn_prompt",
        grid=(bsz, nb),
        in_specs=[
            pl.BlockSpec((1, tq, d), lambda b, i: (b, i, 0)),
            _resident((1, d)),
            _resident((d, d)),
            _resident((d, d)),
            tab, tab, tab,
            per_seq((nb, tq, KV_DIM)),
            per_seq((nb, KV_DIM, tq)),
            per_seq((nb, KV_DIM)),
        ],
        out_specs=pl.BlockSpec((1, tq, d), lambda b, i: (b, i, 0)),
        out_shape=jax.ShapeDtypeStruct((bsz, t, d), F32),
        scratch_shapes=[
            pltpu.VMEM((N_KV_HEADS, GROUP * tq, 2 * HEAD_DIM), BF16),
            pltpu.VMEM((N_KV_HEADS, tq, GROUP * tq), F32),
            pltpu.VMEM((N_KV_HEADS, tq, GROUP * tq), F32),
            pltpu.VMEM((N_KV_HEADS, 1, GROUP * tq), F32),
            pltpu.VMEM((N_KV_HEADS, HEAD_DIM + BF16_ROWS, GROUP * tq), F32),
            pltpu.VMEM((tq, d), BF16),
        ],
        compiler_params=_params("arbitrary", "arbitrary"),
    )(x, g, w_q, w_o, c, s_lo, s_hi, kb, vt, km)


def _dot_hi(a, b):
    return jnp.dot(a, b, preferred_element_type=F32, precision=HIGHEST)


def _rg_sample_kernel(x_ref, cs_ref, h0_ref, g_ref, win_ref, cw_ref, cb_ref, wa_ref, wi_ref,
                      ba_ref, bi_ref, lam_ref, wout_ref, xo_ref, cso_ref, ho_ref):
    x = x_ref[...]
    proj = _dot_hi(_rms(x, g_ref[...]), win_ref[...])
    gate, u = proj[:, :LRU_WIDTH], proj[:, LRU_WIDTH:]
    uc = cb_ref[...] + cs_ref[0] * cw_ref[0:1, :]
    for j in range(1, CONV_WIDTH - 1):
        uc = uc + cs_ref[j] * cw_ref[j:j + 1, :]
    uc = uc + u * cw_ref[CONV_WIDTH - 1:CONV_WIDTH, :]
    for j in range(CONV_WIDTH - 2):
        cso_ref[j] = cs_ref[j + 1]
    cso_ref[CONV_WIDTH - 2] = u

    sp = _softplus(-lam_ref[...])
    h0 = h0_ref[...]
    hs = []
    for n in range(N_GATE_BLOCKS):
        sl = slice(n * GATE_BLOCK, (n + 1) * GATE_BLOCK)
        ucn = uc[:, sl]
        a, b = _lru_coeffs(_dot_hi(ucn, wa_ref[n]) + ba_ref[:, sl],
                           _dot_hi(ucn, wi_ref[n]) + bi_ref[:, sl], ucn, sp[:, sl])
        hs.append(b + a * h0[:, sl])
    h = jnp.concatenate(hs, axis=1)
    ho_ref[...] = h
    xo_ref[...] = x + _dot_hi(h * jax.nn.gelu(gate), wout_ref[...])


def _rg_sample(x, cs, h0, g, w_in, conv_w, conv_b, wa, wi, ba, bi, lam, w_out, layer):
    n, d = x.shape
    vec = _resident((1, LRU_WIDTH))
    gates = _layer(layer, (N_GATE_BLOCKS, GATE_BLOCK, GATE_BLOCK))
    return pl.pallas_call(
        _rg_sample_kernel,
        name="rg_sample",
        grid=(1,),
        in_specs=[
            _resident((n, d)), _resident((CONV_WIDTH - 1, n, LRU_WIDTH)), _resident((n, LRU_WIDTH)),
            _resident((1, d)), _layer(layer, (d, 2 * LRU_WIDTH)), _resident((CONV_WIDTH, LRU_WIDTH)),
            vec, gates, gates, vec, vec, vec, _layer(layer, (LRU_WIDTH, d)),
        ],
        out_specs=[_whole((n, d)), _whole((CONV_WIDTH - 1, n, LRU_WIDTH)), _whole((n, LRU_WIDTH))],
        out_shape=[
            jax.ShapeDtypeStruct((n, d), F32),
            jax.ShapeDtypeStruct((CONV_WIDTH - 1, n, LRU_WIDTH), F32),
            jax.ShapeDtypeStruct((n, LRU_WIDTH), F32),
        ],
        compiler_params=_params("arbitrary"),
    )(x, cs, h0, g, w_in, conv_w, conv_b, wa, wi, ba, bi, lam, w_out)


def _proj_sample_kernel(x_ref, g_ref, w_ref, c_ref, slo_ref, shi_ref, o_ref, *, n_rope_heads):
    y = _dot_hi(_rms(x_ref[...], g_ref[...]), w_ref[...])
    if n_rope_heads:
        tile = lambda r: jnp.concatenate([r[...]] * n_rope_heads, axis=1)
        width = n_rope_heads * HEAD_DIM
        rot = _rope(y[:, :width], tile(c_ref), tile(slo_ref), tile(shi_ref))
        y = jnp.concatenate([rot, y[:, width:]], axis=1) if width < y.shape[1] else rot
    o_ref[...] = y


def _proj_sample(x, g, w, layer, c, s_lo, s_hi, n_rope_heads):
    n, d = x.shape
    width = w.shape[2]
    tab = _resident((1, HEAD_DIM))
    return pl.pallas_call(
        functools.partial(_proj_sample_kernel, n_rope_heads=n_rope_heads),
        name="proj_sample",
        grid=(1,),
        in_specs=[_resident((n, d)), _resident((1, d)), _layer(layer, (d, width)), tab, tab, tab],
        out_specs=_whole((n, width)),
        out_shape=jax.ShapeDtypeStruct((n, width), F32),
        compiler_params=_params("arbitrary"),
    )(x, g, w, c, s_lo, s_hi)


def _kmeans_sample_kernel(pt_ref, *refs):
    page_refs, o_ref = refs[:-1], refs[-1]
    step = pl.program_id(1)
    blocks = len(page_refs) // PAGES_PER_BLOCK
    means = []
    for blk in range(blocks):
        pages = page_refs[blk * PAGES_PER_BLOCK:(blk + 1) * PAGES_PER_BLOCK]
        s = sum(jnp.sum(p[0], axis=0) for p in pages)
        s = sum(s[par * N_KV_HEADS:(par + 1) * N_KV_HEADS] for par in range(ROW_PARITIES))
        means.append(jnp.concatenate([s[kvh:kvh + 1, :] for kvh in range(N_KV_HEADS)], axis=1))
    first = pl.multiple_of(step * blocks, blocks)
    o_ref[0, pl.ds(first, blocks), :] = jnp.concatenate(means, axis=0) * (1.0 / MOBA_BLOCK)


def _kmeans_sample(page_table_flat, cache_k, n_seq, n_pages):
    nb = n_pages // PAGES_PER_BLOCK

    def page(k):
        return pl.BlockSpec(
            (1,) + PAGE_VIEW,
            lambda b, n, pt: (pt[b * n_pages + KMEANS_PAGES * n + k], 0, 0, 0))

    return pl.pallas_call(
        _kmeans_sample_kernel,
        name="kmeans_sample",
        grid_spec=pltpu.PrefetchScalarGridSpec(
            num_scalar_prefetch=1,
            grid=(n_seq, n_pages // KMEANS_PAGES),
            in_specs=[page(k) for k in range(KMEANS_PAGES)],
            out_specs=pl.BlockSpec((1, nb, KV_DIM), lambda b, n, pt: (b, 0, 0)),
        ),
        out_shape=jax.ShapeDtypeStruct((n_seq, nb, KV_DIM), F32),
        compiler_params=_params("arbitrary", "arbitrary"),
    )(page_table_flat, *([cache_k] * KMEANS_PAGES))


def _gate_sample_kernel(q_ref, km_ref, sel_ref):
    km = km_ref[...]
    n_seq, nb, _ = km.shape
    blk = lax.broadcasted_iota(jnp.int32, (n_seq, nb, 1), 1).astype(F32)
    lane = lax.broadcasted_iota(jnp.int32, (n_seq, 1, HEAD_DIM), 2)
    for h in range(N_HEADS):
        kvh = h // GROUP
        qh = q_ref[:, :, h * HEAD_DIM:(h + 1) * HEAD_DIM]
        g = jnp.sum(km[:, :, kvh * HEAD_DIM:(kvh + 1) * HEAD_DIM] * qh, axis=-1, keepdims=True)
        out = jnp.zeros((n_seq, 1, HEAD_DIM), F32)
        for r in range(MOBA_TOPK):
            mx = jnp.max(g, axis=1, keepdims=True)
            idx = jnp.min(jnp.where(g == mx, blk, float(nb)), axis=1, keepdims=True)
            g = jnp.where(blk == idx, NEG, g)
            out = jnp.where(lane == r, idx, out)
        sel_ref[h] = out.astype(jnp.int32)


def _gate_sample(q3, km):
    n_seq = q3.shape[0]
    return pl.pallas_call(
        _gate_sample_kernel,
        name="gate_sample",
        out_shape=jax.ShapeDtypeStruct((N_HEADS, n_seq, 1, HEAD_DIM), jnp.int32),
        compiler_params=pltpu.CompilerParams(vmem_limit_bytes=VMEM_LIMIT),
    )(q3, km)


def _attn_sample_kernel(pt_ref, sel_ref, q_ref, kn_ref, vn_ref, ck_ref, cv_ref, o_ref,
                        kbuf, vbuf, sem, *, n_pages):
    b = pl.program_id(0)
    n_slots = MOBA_TOPK * PAGES_PER_BLOCK
    rows = PAGE_SIZE // ROW_PARITIES

    def copies(seq, buf):
        out = []
        for h in range(N_HEADS):
            kvh = h // GROUP
            for r in range(MOBA_TOPK):
                blk = sel_ref[(seq * N_HEADS + h) * MOBA_TOPK + r]
                for half in range(PAGES_PER_BLOCK):
                    page = pt_ref[seq * n_pages + PAGES_PER_BLOCK * blk + half]
                    slot = r * PAGES_PER_BLOCK + half
                    for par in range(ROW_PARITIES):
                        sub = par * N_KV_HEADS + kvh
                        dst = (buf, h, slot, pl.ds(par * rows, rows))
                        out.append(pltpu.make_async_copy(ck_ref.at[page, :, sub], kbuf.at[dst], sem.at[0, buf]))
                        out.append(pltpu.make_async_copy(cv_ref.at[page, :, sub], vbuf.at[dst], sem.at[1, buf]))
        return out

    cur = b % 2

    @pl.when(b == 0)
    def _():
        for cp in copies(b, cur):
            cp.start()

    @pl.when(b + 1 < pl.num_programs(0))
    def _():
        for cp in copies(b + 1, 1 - cur):
            cp.start()

    for cp in copies(b, cur):
        cp.wait()

    for h in range(N_HEADS):
        kvh = h // GROUP
        qh = q_ref[0, :, h * HEAD_DIM:(h + 1) * HEAD_DIM] * SM_SCALE
        kn = kn_ref[0, :, kvh * HEAD_DIM:(kvh + 1) * HEAD_DIM]
        vn = vn_ref[0, :, kvh * HEAD_DIM:(kvh + 1) * HEAD_DIM]
        kk = kbuf[cur, h].reshape(n_slots * PAGE_SIZE, HEAD_DIM)
        vv = vbuf[cur, h].reshape(n_slots * PAGE_SIZE, HEAD_DIM)
        s = jnp.sum(kk * qh, axis=-1, keepdims=True)
        s_new = jnp.sum(kn * qh, axis=-1, keepdims=True)
        m = jnp.maximum(jnp.max(s, axis=0, keepdims=True), s_new)
        p = jnp.exp(s - m)
        p_new = jnp.exp(s_new - m)
        l = jnp.sum(p, axis=0, keepdims=True) + p_new
        acc = jnp.sum(p * vv, axis=0, keepdims=True) + p_new * vn
        o_ref[0, :, h * HEAD_DIM:(h + 1) * HEAD_DIM] = acc / l


def _attn_sample(page_table_flat, sel_flat, q3, k_new3, v_new3, cache_k, cache_v, n_pages):
    n_seq = q3.shape[0]
    n_slots = MOBA_TOPK * PAGES_PER_BLOCK
    row = lambda w: pl.BlockSpec((1, 1, w), lambda b, pt, sel: (b, 0, 0))
    return pl.pallas_call(
        functools.partial(_attn_sample_kernel, n_pages=n_pages),
        name="attn_sample",
        grid_spec=pltpu.PrefetchScalarGridSpec(
            num_scalar_prefetch=2,
            grid=(n_seq,),
            in_specs=[row(D_MODEL), row(KV_DIM), row(KV_DIM),
                      pl.BlockSpec(memory_space=pl.ANY), pl.BlockSpec(memory_space=pl.ANY)],
            out_specs=row(D_MODEL),
            scratch_shapes=[
                pltpu.VMEM((2, N_HEADS, n_slots, PAGE_SIZE, HEAD_DIM), F32),
                pltpu.VMEM((2, N_HEADS, n_slots, PAGE_SIZE, HEAD_DIM), F32),
                pltpu.SemaphoreType.DMA((2, 2)),
            ],
        ),
        out_shape=jax.ShapeDtypeStruct((n_seq, 1, D_MODEL), F32),
        compiler_params=_params("arbitrary"),
    )(page_table_flat, sel_flat, q3, k_new3, v_new3, cache_k, cache_v)


def _oproj_sample_kernel(x_ref, a_ref, w_ref, o_ref):
    o_ref[...] = x_ref[...] + _dot_hi(a_ref[...], w_ref[...])


def _oproj_sample(x, attn, w_o, layer):
    n, d = x.shape
    return pl.pallas_call(
        _oproj_sample_kernel,
        name="oproj_sample",
        grid=(1,),
        in_specs=[_resident((n, d)), _resident((n, d)), _layer(layer, (d, d))],
        out_specs=_whole((n, d)),
        out_shape=jax.ShapeDtypeStruct((n, d), F32),
        compiler_params=_params("arbitrary"),
    )(x, attn, w_o)


def _prompt_trunk(x, w):
    bsz, t, d = x.shape
    c, s_lo, s_hi = _rope_tables(jnp.arange(t, dtype=jnp.int32))
    row = lambda a: a.reshape(1, -1)
    bf = lambda a: a.astype(BF16)
    convs, hs = [], []
    n_rg = w["w_rg_in"].shape[0]
    depth = w["g_mix"].shape[0]

    w_ffn_in, w_ffn_out = bf(w["w_ffn_in"]), bf(w["w_ffn_out"])

    def ffn(x, l):
        y = _ffn(x.reshape(bsz * t, d), row(w["g_ffn"][l]), w_ffn_in, w_ffn_out,
                 row(w["g_final"]), l, final=(l == depth - 1), tile=FFN_TILE)
        return y.reshape(bsz, t, d)

    for l in range(n_rg):
        wg = jnp.concatenate([w["rg_gate_a_w"][l], w["rg_gate_i_w"][l]], axis=-1)
        x, conv, h = _rg_prompt(
            x, row(w["g_mix"][l]), bf(w["w_rg_in"][l]), w["rg_conv_w"][l], row(w["rg_conv_b"][l]),
            bf(wg), row(w["rg_gate_a_b"][l]), row(w["rg_gate_i_b"][l]), row(w["rg_lambda"][l]),
            bf(w["w_rg_out"][l]))
        convs.append(conv)
        hs.append(h[:, 0, :])
        x = ffn(x, l)

    k, v, kb, vt, km = _kv_prompt(x, row(w["g_kv"]), bf(w["w_kv"]), c, s_lo, s_hi)
    nb = t // MOBA_BLOCK
    kb = kb.reshape(bsz, nb, MOBA_BLOCK, KV_DIM)
    km = km.reshape(bsz, nb, KV_DIM)
    for a in range(depth - n_rg):
        l = n_rg + a
        x = _attn_prompt(x, row(w["g_mix"][l]), bf(w["w_q"][a]), bf(w["w_o"][a]), c, s_lo, s_hi, kb, vt, km)
        x = ffn(x, l)
    shape4 = (bsz, t, N_KV_HEADS, HEAD_DIM)
    return x, jnp.stack(convs), jnp.stack(hs), k.reshape(shape4), v.reshape(shape4)


def _sample_trunk(x3, state_conv, state_h, cache_k, cache_v, page_table, w):
    n_seq, _, d = x3.shape
    n_pages = page_table.shape[1]
    past_len = n_pages * PAGE_SIZE
    c, s_lo, s_hi = _rope_tables(jnp.full((1,), past_len, jnp.int32))
    row = lambda a: a.reshape(1, -1)
    n_rg = w["w_rg_in"].shape[0]
    depth = w["g_mix"].shape[0]
    x = x3.reshape(n_seq, d)
    pt_flat = page_table.reshape(-1)
    ck = cache_k.reshape((cache_k.shape[0],) + PAGE_VIEW)
    cv = cache_v.reshape((cache_v.shape[0],) + PAGE_VIEW)
    convs, hs = [], []

    w_ffn_in_bf, w_ffn_out_bf = w["w_ffn_in"].astype(BF16), w["w_ffn_out"].astype(BF16)

    def ffn(x, l):
        w_in, w_out = (w["w_ffn_in"], w["w_ffn_out"]) if l == 0 else (w_ffn_in_bf, w_ffn_out_bf)
        return _ffn(x, row(w["g_ffn"][l]), w_in, w_out, row(w["g_final"]), l,
                    final=(l == depth - 1), tile=n_seq)

    for l in range(n_rg):
        x, conv, h = _rg_sample(
            x, jnp.swapaxes(state_conv[l], 0, 1), state_h[l], row(w["g_mix"][l]), w["w_rg_in"],
            w["rg_conv_w"][l], row(w["rg_conv_b"][l]), w["rg_gate_a_w"], w["rg_gate_i_w"],
            row(w["rg_gate_a_b"][l]), row(w["rg_gate_i_b"][l]), row(w["rg_lambda"][l]), w["w_rg_out"], l)
        convs.append(jnp.swapaxes(conv, 0, 1))
        hs.append(h)
        x = ffn(x, l)

    kv = _proj_sample(x, row(w["g_kv"]), w["w_kv"][None], 0, c, s_lo, s_hi, N_KV_HEADS)
    k_new, v_new = kv[:, :KV_DIM], kv[:, KV_DIM:]
    km = _kmeans_sample(pt_flat, ck, n_seq, n_pages)
    for a in range(depth - n_rg):
        l = n_rg + a
        q = _proj_sample(x, row(w["g_mix"][l]), w["w_q"], a, c, s_lo, s_hi, N_HEADS)
        q3 = q.reshape(n_seq, 1, d)
        sel = _gate_sample(q3, km)[:, :, 0, :MOBA_TOPK]
        sel_flat = jnp.transpose(sel, (1, 0, 2)).reshape(-1)
        attn = _attn_sample(pt_flat, sel_flat, q3, k_new.reshape(n_seq, 1, KV_DIM),
                            v_new.reshape(n_seq, 1, KV_DIM), ck, cv, n_pages)
        x = _oproj_sample(x, attn.reshape(n_seq, d), w["w_o"], a)
        x = ffn(x, l)
    shape4 = (n_seq, 1, N_KV_HEADS, HEAD_DIM)
    return (x.reshape(n_seq, 1, d), jnp.stack(convs), jnp.stack(hs),
            k_new.reshape(shape4), v_new.reshape(shape4))


def kernel(x_prompt, x_sample, cache_k, cache_v, state_conv, state_h, page_table,
           g_mix, g_ffn, w_rg_in, rg_conv_w, rg_conv_b, rg_gate_a_w, rg_gate_a_b,
           rg_gate_i_w, rg_gate_i_b, rg_lambda, w_rg_out, g_kv, w_kv, w_q, w_o,
           w_ffn_in, w_ffn_out, g_final):
    assert x_sample.shape[1] == 1 and x_prompt.shape[1] % MOBA_BLOCK == 0
    assert MOBA_TOPK <= page_table.shape[1] // PAGES_PER_BLOCK
    w = dict(g_mix=g_mix, g_ffn=g_ffn, w_rg_in=w_rg_in, rg_conv_w=rg_conv_w, rg_conv_b=rg_conv_b,
             rg_gate_a_w=rg_gate_a_w, rg_gate_a_b=rg_gate_a_b, rg_gate_i_w=rg_gate_i_w,
             rg_gate_i_b=rg_gate_i_b, rg_lambda=rg_lambda, w_rg_out=w_rg_out, g_kv=g_kv, w_kv=w_kv,
             w_q=w_q, w_o=w_o, w_ffn_in=w_ffn_in, w_ffn_out=w_ffn_out, g_final=g_final)
    y_p, p_conv, p_h, p_k, p_v = _prompt_trunk(x_prompt, w)
    y_s, s_conv, s_h, s_k, s_v = _sample_trunk(x_sample, state_conv, state_h, cache_k, cache_v,
                                               page_table, w)
    return (y_p, y_s, p_conv, p_h, p_k, p_v, s_conv, s_h, s_k, s_v)
```

```python
import functools
import math

import jax
import jax.numpy as jnp
from jax import lax
from jax.experimental import pallas as pl
from jax.experimental.pallas import tpu as pltpu

D_MODEL = 1024
LRU_WIDTH = D_MODEL
N_GATE_BLOCKS = 8
GATE_BLOCK = LRU_WIDTH // N_GATE_BLOCKS
CONV_WIDTH = 4
LRU_C = 8.0
N_HEADS = 8
HEAD_DIM = D_MODEL // N_HEADS
N_KV_HEADS = 4
KV_DIM = N_KV_HEADS * HEAD_DIM
GROUP = N_HEADS // N_KV_HEADS
ROT_DIM = HEAD_DIM // 4
ROPE_THETA = 500000.0
MOBA_BLOCK = 256
MOBA_TOPK = 3
PAGE_SIZE = 128
PAGES_PER_BLOCK = MOBA_BLOCK // PAGE_SIZE
D_FF = -(-8 * D_MODEL // (3 * 256)) * 256
EPS = 1e-6
NEG = -1e30
SM_SCALE = 1.0 / math.sqrt(HEAD_DIM)
LOG2E = math.log2(math.e)

F32 = jnp.float32
BF16 = jnp.bfloat16
HIGHEST = lax.Precision.HIGHEST

SUBLANES = 8
BF16_ROWS = 16
FF_CHUNK = 256
RG_TILE = 256
FFN_TILE = 512
KV_TILE = 512
KMEANS_PAGES = 16
ATTN_UNROLL = 8
ATTN_CHAIN_HEADS = GROUP
ROW_PARITIES = SUBLANES // N_KV_HEADS
PAGE_VIEW = (PAGE_SIZE // ROW_PARITIES, SUBLANES, HEAD_DIM)
VMEM_LIMIT = 56 * 1024 * 1024

NT_DIMS = (((1,), (1,)), ((), ()))


def _params(*sem):
    return pltpu.CompilerParams(dimension_semantics=sem, vmem_limit_bytes=VMEM_LIMIT)


def _resident(shape):
    zeros = (0,) * len(shape)
    return pl.BlockSpec(shape, lambda *_: zeros, pipeline_mode=pl.Buffered(1))


def _whole(shape):
    zeros = (0,) * len(shape)
    return pl.BlockSpec(shape, lambda *_: zeros)


def _layer(layer, shape):
    zeros = (0,) * len(shape)
    return pl.BlockSpec((None,) + shape, lambda *_: (layer,) + zeros, pipeline_mode=pl.Buffered(1))


def _rms(x, g):
    ms = jnp.mean(x * x, axis=-1, keepdims=True)
    return x * lax.rsqrt(ms + EPS) * g


def _softplus(z):
    return jnp.maximum(z, 0.0) + jnp.log1p(jnp.exp(-jnp.abs(z)))


def _rope(x, c, s_lo, s_hi):
    n = x.shape[-1]
    half = ROT_DIM // 2
    return x * c + pltpu.roll(x, half, 1) * s_hi + pltpu.roll(x, n - half, 1) * s_lo


def _rope_tables(pos):
    half = ROT_DIM // 2
    inv = ROPE_THETA ** (-jnp.arange(half, dtype=F32) / half)
    ang = pos.astype(F32)[:, None] * inv[None, :]
    cos, sin = jnp.cos(ang), jnp.sin(ang)
    t = pos.shape[0]
    rest = HEAD_DIM - ROT_DIM
    c = jnp.concatenate([cos, cos, jnp.ones((t, rest), F32)], axis=1)
    s_hi = jnp.concatenate([jnp.zeros((t, half), F32), sin, jnp.zeros((t, rest), F32)], axis=1)
    s_lo = jnp.concatenate([-sin, jnp.zeros((t, half + rest), F32)], axis=1)
    return c, s_lo, s_hi


def _lru_coeffs(z_a, z_i, uc, sp):
    r = jax.nn.sigmoid(z_a)
    i = jax.nn.sigmoid(z_i)
    log_a = (-LRU_C * r) * sp
    a = jnp.exp(log_a)
    b = jnp.sqrt(-jnp.tanh(log_a) * (a * a + 1.0)) * (i * uc)
    return a, b


def _rg_prompt_kernel(x_ref, g_ref, win_ref, cw_ref, cb_ref, wg_ref, ba_ref, bi_ref, lam_ref,
                      wout_ref, xo_ref, conv_ref, hl_ref, ext_ref, a_ref, b_ref, gg_ref, hc_ref):
    tt = RG_TILE
    t = pl.program_id(1)

    @pl.when(t == 0)
    def _():
        ext_ref[0:SUBLANES, :] = jnp.zeros((SUBLANES, LRU_WIDTH), F32)
        hc_ref[...] = jnp.zeros_like(hc_ref)

    x = x_ref[0]
    xn = _rms(x, g_ref[...]).astype(BF16)
    proj = jnp.dot(xn, win_ref[...], preferred_element_type=F32)
    gg_ref[...] = jax.nn.gelu(proj[:, :LRU_WIDTH])
    ext_ref[SUBLANES:SUBLANES + tt, :] = proj[:, LRU_WIDTH:]

    base = SUBLANES - (CONV_WIDTH - 1)
    uc = cb_ref[...] + ext_ref[base:base + tt, :] * cw_ref[0:1, :]
    for j in range(1, CONV_WIDTH):
        uc = uc + ext_ref[base + j:base + j + tt, :] * cw_ref[j:j + 1, :]
    conv_ref[0] = ext_ref[tt + base:tt + SUBLANES, :]
    ext_ref[0:SUBLANES, :] = ext_ref[tt:tt + SUBLANES, :]

    sp = _softplus(-lam_ref[...])
    for n in range(N_GATE_BLOCKS):
        sl = slice(n * GATE_BLOCK, (n + 1) * GATE_BLOCK)
        ucn = uc[:, sl]
        z = jnp.dot(ucn.astype(BF16), wg_ref[n], preferred_element_type=F32)
        a, b = _lru_coeffs(z[:, :GATE_BLOCK] + ba_ref[:, sl], z[:, GATE_BLOCK:] + bi_ref[:, sl],
                           ucn, sp[:, sl])
        a_ref[:, sl] = a
        b_ref[:, sl] = b

    row = lax.broadcasted_iota(jnp.int32, (SUBLANES, LRU_WIDTH), 0)

    def group(gi, h_prev):
        r0 = pl.multiple_of(gi * SUBLANES, SUBLANES)
        a8 = a_ref[pl.ds(r0, SUBLANES), :]
        b8 = b_ref[pl.ds(r0, SUBLANES), :]
        s = 1
        while s < SUBLANES:
            keep = row >= s
            b8 = jnp.where(keep, a8 * pltpu.roll(b8, s, 0) + b8, b8)
            a8 = jnp.where(keep, a8 * pltpu.roll(a8, s, 0), a8)
            s *= 2
        h = a8 * h_prev + b8
        b_ref[pl.ds(r0, SUBLANES), :] = h
        return h[SUBLANES - 1:SUBLANES, :]

    h_last = lax.fori_loop(0, tt // SUBLANES, group, hc_ref[...])
    hc_ref[...] = h_last
    hl_ref[0] = h_last

    y = jnp.dot((b_ref[...] * gg_ref[...]).astype(BF16), wout_ref[...], preferred_element_type=F32)
    xo_ref[0] = x + y


def _rg_prompt(x, g, w_in, conv_w, conv_b, wg, ba, bi, lam, w_out):
    bsz, t, d = x.shape
    tt = RG_TILE
    vec = lambda n: _resident((1, n))
    return pl.pallas_call(
        _rg_prompt_kernel,
        name="rg_prompt",
        grid=(bsz, t // tt),
        in_specs=[
            pl.BlockSpec((1, tt, d), lambda b, i: (b, i, 0)),
            vec(d),
            _resident((d, 2 * LRU_WIDTH)),
            _resident((CONV_WIDTH, LRU_WIDTH)),
            vec(LRU_WIDTH),
            _resident((N_GATE_BLOCKS, GATE_BLOCK, 2 * GATE_BLOCK)),
            vec(LRU_WIDTH), vec(LRU_WIDTH), vec(LRU_WIDTH),
            _resident((LRU_WIDTH, d)),
        ],
        out_specs=[
            pl.BlockSpec((1, tt, d), lambda b, i: (b, i, 0)),
            pl.BlockSpec((1, CONV_WIDTH - 1, LRU_WIDTH), lambda b, i: (b, 0, 0)),
            pl.BlockSpec((1, 1, LRU_WIDTH), lambda b, i: (b, 0, 0)),
        ],
        out_shape=[
            jax.ShapeDtypeStruct((bsz, t, d), F32),
            jax.ShapeDtypeStruct((bsz, CONV_WIDTH - 1, LRU_WIDTH), F32),
            jax.ShapeDtypeStruct((bsz, 1, LRU_WIDTH), F32),
        ],
        scratch_shapes=[
            pltpu.VMEM((tt + SUBLANES, LRU_WIDTH), F32),
            pltpu.VMEM((tt, LRU_WIDTH), F32),
            pltpu.VMEM((tt, LRU_WIDTH), F32),
            pltpu.VMEM((tt, LRU_WIDTH), F32),
            pltpu.VMEM((1, LRU_WIDTH), F32),
        ],
        compiler_params=_params("arbitrary", "arbitrary"),
    )(x, g, w_in, conv_w, conv_b, wg, ba, bi, lam, w_out)


def _ffn_body(x_ref, g_ref, win_ref, wout_ref, gf_ref, o_ref, act_ref, *, final, precision):
    cdt = win_ref.dtype
    x = x_ref[...]
    xn = _rms(x, g_ref[...]).astype(cdt)
    for c in range(D_FF // FF_CHUNK):
        lo = c * FF_CHUNK
        gate = jnp.dot(xn, win_ref[:, lo:lo + FF_CHUNK], preferred_element_type=F32,
                       precision=precision)
        up = jnp.dot(xn, win_ref[:, D_FF + lo:D_FF + lo + FF_CHUNK], preferred_element_type=F32,
                     precision=precision)
        act_ref[:, lo:lo + FF_CHUNK] = (jax.nn.silu(gate) * up).astype(cdt)
    y = x + jnp.dot(act_ref[...], wout_ref[...], preferred_element_type=F32, precision=precision)
    if final:
        y = _rms(y, gf_ref[...])
    o_ref[...] = y


def _ffn_kernel(*refs, final, precision):
    _ffn_body(*refs, final=final, precision=precision)


def _ffn_kmeans_kernel(pt_ref, x_ref, g_ref, win_ref, wout_ref, gf_ref, ck_ref, o_ref, km_ref,
                       act_ref, kbuf, sem, *, final, precision, n_pages, first_page):
    s = pl.program_id(0)
    cur = s % 2

    def copies(seq, buf):
        return [pltpu.make_async_copy(ck_ref.at[pt_ref[seq * n_pages + first_page + p]],
                                      kbuf.at[buf, p], sem.at[buf])
                for p in range(KMEANS_PAGES)]

    @pl.when(s == 0)
    def _():
        for cp in copies(s, cur):
            cp.start()

    @pl.when(s + 1 < pl.num_programs(0))
    def _():
        for cp in copies(s + 1, 1 - cur):
            cp.start()

    for cp in copies(s, cur):
        cp.wait()

    means = []
    for blk in range(KMEANS_PAGES // PAGES_PER_BLOCK):
        t = sum(jnp.sum(kbuf[cur, blk * PAGES_PER_BLOCK + h], axis=0)
                for h in range(PAGES_PER_BLOCK))
        t = sum(t[par * N_KV_HEADS:(par + 1) * N_KV_HEADS] for par in range(ROW_PARITIES))
        means.append(jnp.concatenate([t[kvh:kvh + 1, :] for kvh in range(N_KV_HEADS)], axis=1))
    km_ref[0] = jnp.concatenate(means, axis=0) * (1.0 / MOBA_BLOCK)

    _ffn_body(x_ref, g_ref, win_ref, wout_ref, gf_ref, o_ref, act_ref, final=final,
              precision=precision)


def _ffn(x, g, w_in, w_out, g_final, layer, *, final, tile, kmeans=None):
    n, d = x.shape
    precision = HIGHEST if w_in.dtype == F32 else None
    in_specs = [
        pl.BlockSpec((tile, d), lambda i, *_: (i, 0)),
        _resident((1, d)),
        _layer(layer, (d, 2 * D_FF)),
        _layer(layer, (D_FF, d)),
        _resident((1, d)),
    ]
    out_spec = pl.BlockSpec((tile, d), lambda i, *_: (i, 0))
    out_shape = jax.ShapeDtypeStruct((n, d), F32)
    act = pltpu.VMEM((tile, D_FF), w_in.dtype)
    if kmeans is None:
        return pl.pallas_call(
            functools.partial(_ffn_kernel, final=final, precision=precision),
            name="ffn",
            grid=(n // tile,),
            in_specs=in_specs,
            out_specs=out_spec,
            out_shape=out_shape,
            scratch_shapes=[act],
            compiler_params=_params("arbitrary"),
        )(x, g, w_in, w_out, g_final)

    pt_flat, cache, n_pages, first_page = kmeans
    n_seq = n // tile
    assert pt_flat.shape[0] == n_seq * n_pages
    blocks = KMEANS_PAGES // PAGES_PER_BLOCK
    return pl.pallas_call(
        functools.partial(_ffn_kmeans_kernel, final=final, precision=precision, n_pages=n_pages,
                          first_page=first_page),
        name="ffn_kmeans",
        grid_spec=pltpu.PrefetchScalarGridSpec(
            num_scalar_prefetch=1,
            grid=(n_seq,),
            in_specs=in_specs + [pl.BlockSpec(memory_space=pl.ANY)],
            out_specs=[out_spec, pl.BlockSpec((1, blocks, KV_DIM), lambda i, *_: (i, 0, 0))],
            scratch_shapes=[act, pltpu.VMEM((2, KMEANS_PAGES) + PAGE_VIEW, F32),
                            pltpu.SemaphoreType.DMA((2,))],
        ),
        out_shape=[out_shape, jax.ShapeDtypeStruct((n_seq, blocks, KV_DIM), F32)],
        compiler_params=_params("arbitrary"),
    )(pt_flat, x, g, w_in, w_out, g_final, cache)


def _kv_prompt_kernel(x_ref, g_ref, w_ref, c_ref, slo_ref, shi_ref,
                      k_ref, v_ref, kb_ref, vt_ref, km_ref):
    xn = _rms(x_ref[0], g_ref[...]).astype(BF16)
    kv = jnp.dot(xn, w_ref[...], preferred_element_type=F32)
    tile4 = lambda r: jnp.concatenate([r[...]] * N_KV_HEADS, axis=1)
    k = _rope(kv[:, :KV_DIM], tile4(c_ref), tile4(slo_ref), tile4(shi_ref))
    v = kv[:, KV_DIM:]
    k_ref[0] = k
    v_ref[0] = v
    kb_ref[0] = k.astype(BF16)
    for bb in range(KV_TILE // MOBA_BLOCK):
        rows = slice(bb * MOBA_BLOCK, (bb + 1) * MOBA_BLOCK)
        vt_ref[0, bb] = v[rows, :].T.astype(BF16)
        km_ref[0, bb] = jnp.sum(k[rows, :], axis=0, keepdims=True) * (1.0 / MOBA_BLOCK)


def _kv_prompt(x, g, w_kv, c, s_lo, s_hi):
    bsz, t, d = x.shape
    tt = KV_TILE
    nbt = tt // MOBA_BLOCK
    nb = t // MOBA_BLOCK
    tab = pl.BlockSpec((tt, HEAD_DIM), lambda b, i: (i, 0))
    row_spec = pl.BlockSpec((1, tt, KV_DIM), lambda b, i: (b, i, 0))
    return pl.pallas_call(
        _kv_prompt_kernel,
        name="kv_prompt",
        grid=(bsz, t // tt),
        in_specs=[
            pl.BlockSpec((1, tt, d), lambda b, i: (b, i, 0)),
            _resident((1, d)),
            _resident((d, 2 * KV_DIM)),
            tab, tab, tab,
        ],
        out_specs=[
            row_spec, row_spec, row_spec,
            pl.BlockSpec((1, nbt, KV_DIM, MOBA_BLOCK), lambda b, i: (b, i, 0, 0)),
            pl.BlockSpec((1, nbt, 1, KV_DIM), lambda b, i: (b, i, 0, 0)),
        ],
        out_shape=[
            jax.ShapeDtypeStruct((bsz, t, KV_DIM), F32),
            jax.ShapeDtypeStruct((bsz, t, KV_DIM), F32),
            jax.ShapeDtypeStruct((bsz, t, KV_DIM), BF16),
            jax.ShapeDtypeStruct((bsz, nb, KV_DIM, MOBA_BLOCK), BF16),
            jax.ShapeDtypeStruct((bsz, nb, 1, KV_DIM), F32),
        ],
        compiler_params=_params("arbitrary", "arbitrary"),
    )(x, g, w_kv, c, s_lo, s_hi)


def _select_bias(gate, n_allowed, n_valid):
    nb, nq = gate.shape
    rowi = lax.broadcasted_iota(jnp.int32, (nb, nq), 0).astype(F32)
    g = jnp.where(rowi < n_allowed, gate, NEG)
    bias = jnp.where(rowi == n_allowed, 0.0, NEG)
    for r in range(MOBA_TOPK):
        mx = jnp.max(g, axis=0, keepdims=True)
        idx = jnp.min(jnp.where(g == mx, rowi, float(nb)), axis=0, keepdims=True)
        ok = (n_valid > r).astype(F32)
        idx = idx * ok - (1.0 - ok)
        pick = rowi == idx
        bias = jnp.where(pick, 0.0, bias)
        g = jnp.where(pick, NEG, g)
    return bias


def _attn_prompt_kernel(x_ref, g_ref, wq_ref, wo_ref, c_ref, slo_ref, shi_ref, k_ref, vt_ref,
                        km_ref, o_ref, q2s_ref, s0_ref, s1_ref, m_ref, acc_ref, attn_ref):
    tq = MOBA_BLOCK
    nq = ATTN_CHAIN_HEADS * tq
    i = pl.program_id(1)
    x = x_ref[0]
    xn = _rms(x, g_ref[...]).astype(BF16)
    q = jnp.dot(xn, wq_ref[...], preferred_element_type=F32)
    c, s_lo, s_hi = c_ref[...], slo_ref[...], shi_ref[...]
    own = i.astype(F32)

    key_pos = lax.broadcasted_iota(jnp.int32, (tq, nq), 0)
    qry_pos = lax.broadcasted_iota(jnp.int32, (tq, nq), 1) % tq
    causal = key_pos <= qry_pos
    n_chains = N_HEADS // ATTN_CHAIN_HEADS
    chain_heads = [range(ch * ATTN_CHAIN_HEADS, (ch + 1) * ATTN_CHAIN_HEADS)
                   for ch in range(n_chains)]
    kv_lanes = [slice((hs[0] // GROUP) * HEAD_DIM, (hs[0] // GROUP + 1) * HEAD_DIM)
                for hs in chain_heads]

    ones_rows = jnp.where(lax.broadcasted_iota(jnp.int32, (BF16_ROWS, tq), 0) == 0, 1.0, 0.0
                          ).astype(BF16)

    def vt_ones(j, lanes):
        return jnp.concatenate([vt_ref[0, j, lanes, :], ones_rows], axis=0)

    for kvh, lanes in enumerate(kv_lanes):
        q2 = jnp.concatenate(
            [_rope(q[:, h * HEAD_DIM:(h + 1) * HEAD_DIM], c, s_lo, s_hi)
             for h in chain_heads[kvh]], axis=0)
        gate = lax.dot_general(km_ref[0, :, lanes], q2, NT_DIMS, precision=HIGHEST,
                               preferred_element_type=F32)
        bias = _select_bias(gate, own, own)
        bias_t = jnp.concatenate(
            [bias, jnp.zeros((HEAD_DIM - bias.shape[0], nq), F32)], axis=0).T
        q2s = (q2 * (SM_SCALE * LOG2E)).astype(BF16)
        q2s_ref[kvh] = jnp.concatenate([q2s, bias_t.astype(BF16)], axis=1)
        m_ref[kvh] = jnp.full((1, nq), NEG, F32)
        acc_ref[kvh] = jnp.zeros(acc_ref.shape[1:], F32)

    block_lane = lax.broadcasted_iota(jnp.int32, (1, HEAD_DIM), 1)

    def qk(j, s_out):
        one_hot = jnp.broadcast_to(jnp.where(block_lane == j, 1.0, 0.0).astype(BF16),
                                   (tq, HEAD_DIM))
        for kvh, lanes in enumerate(kv_lanes):
            k_aug = jnp.concatenate([k_ref[0, j, :, lanes], one_hot], axis=1)
            s_out[kvh] = lax.dot_general(k_aug, q2s_ref[kvh], NT_DIMS,
                                         preferred_element_type=F32)

    def block(j, s_cur, s_nxt):
        if s_nxt is not None:
            qk(j + 1, s_nxt)
        for kvh, lanes in enumerate(kv_lanes):
            s = s_cur[kvh]
            if s_nxt is None:
                s = jnp.where(causal, s, NEG)
            m = m_ref[kvh]
            m_new = jnp.maximum(m, jnp.max(s, axis=0, keepdims=True))
            alpha = jnp.exp2(m - m_new)
            p = jnp.exp2(s - m_new)
            m_ref[kvh] = m_new
            acc_ref[kvh] = alpha * acc_ref[kvh] + jnp.dot(
                vt_ones(j, lanes), p.astype(BF16), preferred_element_type=F32)

    qk(0, s0_ref)
    bufs = (s0_ref, s1_ref)

    def run(first, count):
        for k in range(count):
            block(first + k, bufs[k % 2], bufs[(k + 1) % 2])

    def long_trip(t, carry):
        run(ATTN_UNROLL * t, ATTN_UNROLL)
        return carry

    n_long = i // ATTN_UNROLL
    lax.fori_loop(0, n_long, long_trip, 0)
    done = n_long * ATTN_UNROLL

    def pair_trip(t, carry):
        run(done + 2 * t, 2)
        return carry

    lax.fori_loop(0, (i - done) // 2, pair_trip, 0)

    @pl.when(i % 2 == 1)
    def _():
        run(i - 1, 1)
        block(i, s1_ref, None)

    @pl.when(i % 2 == 0)
    def _():
        block(i, s0_ref, None)

    for kvh in range(n_chains):
        o = (acc_ref[kvh, :HEAD_DIM, :] / acc_ref[kvh, HEAD_DIM:HEAD_DIM + 1, :]).T
        for j, h in enumerate(chain_heads[kvh]):
            attn_ref[:, h * HEAD_DIM:(h + 1) * HEAD_DIM] = o[j * tq:(j + 1) * tq, :].astype(BF16)

    o_ref[0] = x + jnp.dot(attn_ref[...], wo_ref[...], preferred_element_type=F32)


def _attn_prompt(x, g, w_q, w_o, c, s_lo, s_hi, kb, vt, km):
    bsz, t, d = x.shape
    tq = MOBA_BLOCK
    nb = t // tq
    n_chains, nq = N_HEADS // ATTN_CHAIN_HEADS, ATTN_CHAIN_HEADS * tq
    tab = pl.BlockSpec((tq, HEAD_DIM), lambda b, i: (i, 0))
    per_seq = lambda shape: pl.BlockSpec((1,) + shape, lambda b, i: (b,) + (0,) * len(shape),
                                         pipeline_mode=pl.Buffered(1))
    return pl.pallas_call(
        _attn_prompt_kernel,
        name="attn_prompt",
        grid=(bsz, nb),
        in_specs=[
            pl.BlockSpec((1, tq, d), lambda b, i: (b, i, 0)),
            _resident((1, d)),
            _resident((d, d)),
            _resident((d, d)),
            tab, tab, tab,
            per_seq((nb, tq, KV_DIM)),
            per_seq((nb, KV_DIM, tq)),
            per_seq((nb, KV_DIM)),
        ],
        out_specs=pl.BlockSpec((1, tq, d), lambda b, i: (b, i, 0)),
        out_shape=jax.ShapeDtypeStruct((bsz, t, d), F32),
        scratch_shapes=[
            pltpu.VMEM((n_chains, nq, 2 * HEAD_DIM), BF16),
            pltpu.VMEM((n_chains, tq, nq), F32),
            pltpu.VMEM((n_chains, tq, nq), F32),
            pltpu.VMEM((n_chains, 1, nq), F32),
            pltpu.VMEM((n_chains, HEAD_DIM + BF16_ROWS, nq), F32),
            pltpu.VMEM((tq, d), BF16),
        ],
        compiler_params=_params("arbitrary", "arbitrary"),
    )(x, g, w_q, w_o, c, s_lo, s_hi, kb, vt, km)


def _dot_hi(a, b):
    return jnp.dot(a, b, preferred_element_type=F32, precision=HIGHEST)


def _rg_sample_kernel(x_ref, cs_ref, h0_ref, g_ref, win_ref, cw_ref, cb_ref, wa_ref, wi_ref,
                      ba_ref, bi_ref, lam_ref, wout_ref, xo_ref, cso_ref, ho_ref):
    x = x_ref[...]
    proj = _dot_hi(_rms(x, g_ref[...]), win_ref[...])
    gate, u = proj[:, :LRU_WIDTH], proj[:, LRU_WIDTH:]
    uc = cb_ref[...] + cs_ref[0] * cw_ref[0:1, :]
    for j in range(1, CONV_WIDTH - 1):
        uc = uc + cs_ref[j] * cw_ref[j:j + 1, :]
    uc = uc + u * cw_ref[CONV_WIDTH - 1:CONV_WIDTH, :]
    for j in range(CONV_WIDTH - 2):
        cso_ref[j] = cs_ref[j + 1]
    cso_ref[CONV_WIDTH - 2] = u

    sp = _softplus(-lam_ref[...])
    h0 = h0_ref[...]
    hs = []
    for n in range(N_GATE_BLOCKS):
        sl = slice(n * GATE_BLOCK, (n + 1) * GATE_BLOCK)
        ucn = uc[:, sl]
        a, b = _lru_coeffs(_dot_hi(ucn, wa_ref[n]) + ba_ref[:, sl],
                           _dot_hi(ucn, wi_ref[n]) + bi_ref[:, sl], ucn, sp[:, sl])
        hs.append(b + a * h0[:, sl])
    h = jnp.concatenate(hs, axis=1)
    ho_ref[...] = h
    xo_ref[...] = x + _dot_hi(h * jax.nn.gelu(gate), wout_ref[...])


def _rg_sample(x, cs, h0, g, w_in, conv_w, conv_b, wa, wi, ba, bi, lam, w_out, layer):
    n, d = x.shape
    vec = _resident((1, LRU_WIDTH))
    gates = _layer(layer, (N_GATE_BLOCKS, GATE_BLOCK, GATE_BLOCK))
    return pl.pallas_call(
        _rg_sample_kernel,
        name="rg_sample",
        grid=(1,),
        in_specs=[
            _resident((n, d)), _resident((CONV_WIDTH - 1, n, LRU_WIDTH)), _resident((n, LRU_WIDTH)),
            _resident((1, d)), _layer(layer, (d, 2 * LRU_WIDTH)), _resident((CONV_WIDTH, LRU_WIDTH)),
            vec, gates, gates, vec, vec, vec, _layer(layer, (LRU_WIDTH, d)),
        ],
        out_specs=[_whole((n, d)), _whole((CONV_WIDTH - 1, n, LRU_WIDTH)), _whole((n, LRU_WIDTH))],
        out_shape=[
            jax.ShapeDtypeStruct((n, d), F32),
            jax.ShapeDtypeStruct((CONV_WIDTH - 1, n, LRU_WIDTH), F32),
            jax.ShapeDtypeStruct((n, LRU_WIDTH), F32),
        ],
        compiler_params=_params("arbitrary"),
    )(x, cs, h0, g, w_in, conv_w, conv_b, wa, wi, ba, bi, lam, w_out)


def _proj_sample_kernel(x_ref, g_ref, w_ref, c_ref, slo_ref, shi_ref, o_ref, *, n_rope_heads):
    y = _dot_hi(_rms(x_ref[...], g_ref[...]), w_ref[...])
    if n_rope_heads:
        tile = lambda r: jnp.concatenate([r[...]] * n_rope_heads, axis=1)
        width = n_rope_heads * HEAD_DIM
        rot = _rope(y[:, :width], tile(c_ref), tile(slo_ref), tile(shi_ref))
        y = jnp.concatenate([rot, y[:, width:]], axis=1) if width < y.shape[1] else rot
    o_ref[...] = y


def _proj_sample(x, g, w, layer, c, s_lo, s_hi, n_rope_heads):
    n, d = x.shape
    width = w.shape[2]
    tab = _resident((1, HEAD_DIM))
    return pl.pallas_call(
        functools.partial(_proj_sample_kernel, n_rope_heads=n_rope_heads),
        name="proj_sample",
        grid=(1,),
        in_specs=[_resident((n, d)), _resident((1, d)), _layer(layer, (d, width)), tab, tab, tab],
        out_specs=_whole((n, width)),
        out_shape=jax.ShapeDtypeStruct((n, width), F32),
        compiler_params=_params("arbitrary"),
    )(x, g, w, c, s_lo, s_hi)


def _gate_sample_kernel(q_ref, km_ref, sel_ref):
    km = km_ref[...]
    n_seq, nb, _ = km.shape
    blk = lax.broadcasted_iota(jnp.int32, (n_seq, nb, 1), 1).astype(F32)
    lane = lax.broadcasted_iota(jnp.int32, (n_seq, 1, HEAD_DIM), 2)
    for h in range(N_HEADS):
        kvh = h // GROUP
        qh = q_ref[:, :, h * HEAD_DIM:(h + 1) * HEAD_DIM]
        g = jnp.sum(km[:, :, kvh * HEAD_DIM:(kvh + 1) * HEAD_DIM] * qh, axis=-1, keepdims=True)
        out = jnp.zeros((n_seq, 1, HEAD_DIM), F32)
        for r in range(MOBA_TOPK):
            mx = jnp.max(g, axis=1, keepdims=True)
            idx = jnp.min(jnp.where(g == mx, blk, float(nb)), axis=1, keepdims=True)
            g = jnp.where(blk == idx, NEG, g)
            out = jnp.where(lane == r, idx, out)
        sel_ref[h] = out.astype(jnp.int32)


def _gate_sample(q3, km):
    n_seq = q3.shape[0]
    return pl.pallas_call(
        _gate_sample_kernel,
        name="gate_sample",
        out_shape=jax.ShapeDtypeStruct((N_HEADS, n_seq, 1, HEAD_DIM), jnp.int32),
        compiler_params=pltpu.CompilerParams(vmem_limit_bytes=VMEM_LIMIT),
    )(q3, km)


def _attn_sample_kernel(pt_ref, sel_ref, q_ref, kn_ref, vn_ref, ck_ref, cv_ref, o_ref,
                        kbuf, vbuf, sem, *, n_pages):
    b = pl.program_id(0)
    n_slots = MOBA_TOPK * PAGES_PER_BLOCK
    rows = PAGE_SIZE // ROW_PARITIES

    def copies(seq, buf):
        out = []
        for h in range(N_HEADS):
            kvh = h // GROUP
            for r in range(MOBA_TOPK):
                blk = sel_ref[(seq * N_HEADS + h) * MOBA_TOPK + r]
                for half in range(PAGES_PER_BLOCK):
                    page = pt_ref[seq * n_pages + PAGES_PER_BLOCK * blk + half]
                    slot = r * PAGES_PER_BLOCK + half
                    for par in range(ROW_PARITIES):
                        sub = par * N_KV_HEADS + kvh
                        dst = (buf, h, slot, pl.ds(par * rows, rows))
                        out.append(pltpu.make_async_copy(ck_ref.at[page, :, sub], kbuf.at[dst], sem.at[0, buf]))
                        out.append(pltpu.make_async_copy(cv_ref.at[page, :, sub], vbuf.at[dst], sem.at[1, buf]))
        return out

    cur = b % 2

    @pl.when(b == 0)
    def _():
        for cp in copies(b, cur):
            cp.start()

    @pl.when(b + 1 < pl.num_programs(0))
    def _():
        for cp in copies(b + 1, 1 - cur):
            cp.start()

    for cp in copies(b, cur):
        cp.wait()

    for h in range(N_HEADS):
        kvh = h // GROUP
        qh = q_ref[0, :, h * HEAD_DIM:(h + 1) * HEAD_DIM] * SM_SCALE
        kn = kn_ref[0, :, kvh * HEAD_DIM:(kvh + 1) * HEAD_DIM]
        vn = vn_ref[0, :, kvh * HEAD_DIM:(kvh + 1) * HEAD_DIM]
        kk = kbuf[cur, h].reshape(n_slots * PAGE_SIZE, HEAD_DIM)
        vv = vbuf[cur, h].reshape(n_slots * PAGE_SIZE, HEAD_DIM)
        s = jnp.sum(kk * qh, axis=-1, keepdims=True)
        s_new = jnp.sum(kn * qh, axis=-1, keepdims=True)
        m = jnp.maximum(jnp.max(s, axis=0, keepdims=True), s_new)
        p = jnp.exp(s - m)
        p_new = jnp.exp(s_new - m)
        l = jnp.sum(p, axis=0, keepdims=True) + p_new
        acc = jnp.sum(p * vv, axis=0, keepdims=True) + p_new * vn
        o_ref[0, :, h * HEAD_DIM:(h + 1) * HEAD_DIM] = acc / l


def _attn_sample(page_table_flat, sel_flat, q3, k_new3, v_new3, cache_k, cache_v, n_pages):
    n_seq = q3.shape[0]
    n_slots = MOBA_TOPK * PAGES_PER_BLOCK
    row = lambda w: pl.BlockSpec((1, 1, w), lambda b, pt, sel: (b, 0, 0))
    return pl.pallas_call(
        functools.partial(_attn_sample_kernel, n_pages=n_pages),
        name="attn_sample",
        grid_spec=pltpu.PrefetchScalarGridSpec(
            num_scalar_prefetch=2,
            grid=(n_seq,),
            in_specs=[row(D_MODEL), row(KV_DIM), row(KV_DIM),
                      pl.BlockSpec(memory_space=pl.ANY), pl.BlockSpec(memory_space=pl.ANY)],
            out_specs=row(D_MODEL),
            scratch_shapes=[
                pltpu.VMEM((2, N_HEADS, n_slots, PAGE_SIZE, HEAD_DIM), F32),
                pltpu.VMEM((2, N_HEADS, n_slots, PAGE_SIZE, HEAD_DIM), F32),
                pltpu.SemaphoreType.DMA((2, 2)),
            ],
        ),
        out_shape=jax.ShapeDtypeStruct((n_seq, 1, D_MODEL), F32),
        compiler_params=_params("arbitrary"),
    )(page_table_flat, sel_flat, q3, k_new3, v_new3, cache_k, cache_v)


def _oproj_sample_kernel(x_ref, a_ref, w_ref, o_ref):
    o_ref[...] = x_ref[...] + _dot_hi(a_ref[...], w_ref[...])


def _oproj_sample(x, attn, w_o, layer):
    n, d = x.shape
    return pl.pallas_call(
        _oproj_sample_kernel,
        name="oproj_sample",
        grid=(1,),
        in_specs=[_resident((n, d)), _resident((n, d)), _layer(layer, (d, d))],
        out_specs=_whole((n, d)),
        out_shape=jax.ShapeDtypeStruct((n, d), F32),
        compiler_params=_params("arbitrary"),
    )(x, attn, w_o)


def _prompt_trunk(x, w, pt_flat, cache_k_pages, n_pages):
    bsz, t, d = x.shape
    c, s_lo, s_hi = _rope_tables(jnp.arange(t, dtype=jnp.int32))
    row = lambda a: a.reshape(1, -1)
    bf = lambda a: a.astype(BF16)
    convs, hs, kmeans = [], [], []
    n_rg = w["w_rg_in"].shape[0]
    depth = w["g_mix"].shape[0]
    assert n_pages == depth * KMEANS_PAGES

    w_ffn_in, w_ffn_out = bf(w["w_ffn_in"]), bf(w["w_ffn_out"])

    def ffn(x, l):
        y, km_l = _ffn(x.reshape(bsz * t, d), row(w["g_ffn"][l]), w_ffn_in, w_ffn_out,
                       row(w["g_final"]), l, final=(l == depth - 1), tile=FFN_TILE,
                       kmeans=(pt_flat, cache_k_pages, n_pages, l * KMEANS_PAGES))
        kmeans.append(km_l)
        return y.reshape(bsz, t, d)

    for l in range(n_rg):
        wg = jnp.concatenate([w["rg_gate_a_w"][l], w["rg_gate_i_w"][l]], axis=-1)
        x, conv, h = _rg_prompt(
            x, row(w["g_mix"][l]), bf(w["w_rg_in"][l]), w["rg_conv_w"][l], row(w["rg_conv_b"][l]),
            bf(wg), row(w["rg_gate_a_b"][l]), row(w["rg_gate_i_b"][l]), row(w["rg_lambda"][l]),
            bf(w["w_rg_out"][l]))
        convs.append(conv)
        hs.append(h[:, 0, :])
        x = ffn(x, l)

    k, v, kb, vt, km = _kv_prompt(x, row(w["g_kv"]), bf(w["w_kv"]), c, s_lo, s_hi)
    nb = t // MOBA_BLOCK
    kb = kb.reshape(bsz, nb, MOBA_BLOCK, KV_DIM)
    km = km.reshape(bsz, nb, KV_DIM)
    for a in range(depth - n_rg):
        l = n_rg + a
        x = _attn_prompt(x, row(w["g_mix"][l]), bf(w["w_q"][a]), bf(w["w_o"][a]), c, s_lo, s_hi, kb, vt, km)
        x = ffn(x, l)
    shape4 = (bsz, t, N_KV_HEADS, HEAD_DIM)
    outs = (x, jnp.stack(convs), jnp.stack(hs), k.reshape(shape4), v.reshape(shape4))
    return outs, jnp.concatenate(kmeans, axis=1)


def _sample_trunk(x3, state_conv, state_h, ck, cv, pt_flat, n_pages, km, w):
    n_seq, _, d = x3.shape
    past_len = n_pages * PAGE_SIZE
    c, s_lo, s_hi = _rope_tables(jnp.full((1,), past_len, jnp.int32))
    row = lambda a: a.reshape(1, -1)
    n_rg = w["w_rg_in"].shape[0]
    depth = w["g_mix"].shape[0]
    x = x3.reshape(n_seq, d)
    convs, hs = [], []

    w_ffn_in_bf, w_ffn_out_bf = w["w_ffn_in"].astype(BF16), w["w_ffn_out"].astype(BF16)

    def ffn(x, l):
        w_in, w_out = (w["w_ffn_in"], w["w_ffn_out"]) if l == 0 else (w_ffn_in_bf, w_ffn_out_bf)
        return _ffn(x, row(w["g_ffn"][l]), w_in, w_out, row(w["g_final"]), l,
                    final=(l == depth - 1), tile=n_seq)

    for l in range(n_rg):
        x, conv, h = _rg_sample(
            x, jnp.swapaxes(state_conv[l], 0, 1), state_h[l], row(w["g_mix"][l]), w["w_rg_in"],
            w["rg_conv_w"][l], row(w["rg_conv_b"][l]), w["rg_gate_a_w"], w["rg_gate_i_w"],
            row(w["rg_gate_a_b"][l]), row(w["rg_gate_i_b"][l]), row(w["rg_lambda"][l]), w["w_rg_out"], l)
        convs.append(jnp.swapaxes(conv, 0, 1))
        hs.append(h)
        x = ffn(x, l)

    kv = _proj_sample(x, row(w["g_kv"]), w["w_kv"][None], 0, c, s_lo, s_hi, N_KV_HEADS)
    k_new, v_new = kv[:, :KV_DIM], kv[:, KV_DIM:]
    for a in range(depth - n_rg):
        l = n_rg + a
        q = _proj_sample(x, row(w["g_mix"][l]), w["w_q"], a, c, s_lo, s_hi, N_HEADS)
        q3 = q.reshape(n_seq, 1, d)
        sel = _gate_sample(q3, km)[:, :, 0, :MOBA_TOPK]
        sel_flat = jnp.transpose(sel, (1, 0, 2)).reshape(-1)
        attn = _attn_sample(pt_flat, sel_flat, q3, k_new.reshape(n_seq, 1, KV_DIM),
                            v_new.reshape(n_seq, 1, KV_DIM), ck, cv, n_pages)
        x = _oproj_sample(x, attn.reshape(n_seq, d), w["w_o"], a)
        x = ffn(x, l)
    shape4 = (n_seq, 1, N_KV_HEADS, HEAD_DIM)
    return (x.reshape(n_seq, 1, d), jnp.stack(convs), jnp.stack(hs),
            k_new.reshape(shape4), v_new.reshape(shape4))


def kernel(x_prompt, x_sample, cache_k, cache_v, state_conv, state_h, page_table,
           g_mix, g_ffn, w_rg_in, rg_conv_w, rg_conv_b, rg_gate_a_w, rg_gate_a_b,
           rg_gate_i_w, rg_gate_i_b, rg_lambda, w_rg_out, g_kv, w_kv, w_q, w_o,
           w_ffn_in, w_ffn_out, g_final):
    assert x_sample.shape[1] == 1 and x_prompt.shape[1] % MOBA_BLOCK == 0
    assert MOBA_TOPK <= page_table.shape[1] // PAGES_PER_BLOCK
    w = dict(g_mix=g_mix, g_ffn=g_ffn, w_rg_in=w_rg_in, rg_conv_w=rg_conv_w, rg_conv_b=rg_conv_b,
             rg_gate_a_w=rg_gate_a_w, rg_gate_a_b=rg_gate_a_b, rg_gate_i_w=rg_gate_i_w,
             rg_gate_i_b=rg_gate_i_b, rg_lambda=rg_lambda, w_rg_out=w_rg_out, g_kv=g_kv, w_kv=w_kv,
             w_q=w_q, w_o=w_o, w_ffn_in=w_ffn_in, w_ffn_out=w_ffn_out, g_final=g_final)
    n_pages = page_table.shape[1]
    pt_flat = page_table.reshape(-1)
    ck = cache_k.reshape((cache_k.shape[0],) + PAGE_VIEW)
    cv = cache_v.reshape((cache_v.shape[0],) + PAGE_VIEW)
    (y_p, p_conv, p_h, p_k, p_v), km = _prompt_trunk(x_prompt, w, pt_flat, ck, n_pages)
    y_s, s_conv, s_h, s_k, s_v = _sample_trunk(x_sample, state_conv, state_h, ck, cv, pt_flat,
                                               n_pages, km, w)
    return (y_p, y_s, p_conv, p_h, p_k, p_v, s_conv, s_h, s_k, s_v)
```

```python
import functools
import math

import jax
import jax.numpy as jnp
from jax import lax
from jax.experimental import pallas as pl
from jax.experimental.pallas import tpu as pltpu

D_MODEL = 1024
LRU_WIDTH = D_MODEL
N_GATE_BLOCKS = 8
GATE_BLOCK = LRU_WIDTH // N_GATE_BLOCKS
CONV_WIDTH = 4
LRU_C = 8.0
N_HEADS = 8
HEAD_DIM = D_MODEL // N_HEADS
N_KV_HEADS = 4
KV_DIM = N_KV_HEADS * HEAD_DIM
GROUP = N_HEADS // N_KV_HEADS
ROT_DIM = HEAD_DIM // 4
ROPE_THETA = 500000.0
MOBA_BLOCK = 256
MOBA_TOPK = 3
PAGE_SIZE = 128
PAGES_PER_BLOCK = MOBA_BLOCK // PAGE_SIZE
D_FF = -(-8 * D_MODEL // (3 * 256)) * 256
EPS = 1e-6
NEG = -1e30
SM_SCALE = 1.0 / math.sqrt(HEAD_DIM)
LOG2E = math.log2(math.e)

F32 = jnp.float32
BF16 = jnp.bfloat16
HIGHEST = lax.Precision.HIGHEST

SUBLANES = 8
BF16_ROWS = 16
FF_CHUNK = 256
RG_TILE = 256
FFN_TILE = 512
KV_TILE = 512
KMEANS_PAGES = 32
ATTN_UNROLL = 8
ATTN_CHAIN_HEADS = GROUP
ROW_PARITIES = SUBLANES // N_KV_HEADS
PAGE_VIEW = (PAGE_SIZE // ROW_PARITIES, SUBLANES, HEAD_DIM)
VMEM_LIMIT = 56 * 1024 * 1024

NT_DIMS = (((1,), (1,)), ((), ()))


def _params(*sem):
    return pltpu.CompilerParams(dimension_semantics=sem, vmem_limit_bytes=VMEM_LIMIT)


def _resident(shape):
    zeros = (0,) * len(shape)
    return pl.BlockSpec(shape, lambda *_: zeros, pipeline_mode=pl.Buffered(1))


def _whole(shape):
    zeros = (0,) * len(shape)
    return pl.BlockSpec(shape, lambda *_: zeros)


def _layer(layer, shape):
    zeros = (0,) * len(shape)
    return pl.BlockSpec((None,) + shape, lambda *_: (layer,) + zeros, pipeline_mode=pl.Buffered(1))


def _rms(x, g):
    ms = jnp.mean(x * x, axis=-1, keepdims=True)
    return x * lax.rsqrt(ms + EPS) * g


def _softplus(z):
    return jnp.maximum(z, 0.0) + jnp.log1p(jnp.exp(-jnp.abs(z)))


def _rope(x, c, s_lo, s_hi):
    n = x.shape[-1]
    half = ROT_DIM // 2
    return x * c + pltpu.roll(x, half, 1) * s_hi + pltpu.roll(x, n - half, 1) * s_lo


def _rope_tables(pos):
    half = ROT_DIM // 2
    inv = ROPE_THETA ** (-jnp.arange(half, dtype=F32) / half)
    ang = pos.astype(F32)[:, None] * inv[None, :]
    cos, sin = jnp.cos(ang), jnp.sin(ang)
    t = pos.shape[0]
    rest = HEAD_DIM - ROT_DIM
    c = jnp.concatenate([cos, cos, jnp.ones((t, rest), F32)], axis=1)
    s_hi = jnp.concatenate([jnp.zeros((t, half), F32), sin, jnp.zeros((t, rest), F32)], axis=1)
    s_lo = jnp.concatenate([-sin, jnp.zeros((t, half + rest), F32)], axis=1)
    return c, s_lo, s_hi


def _lru_coeffs(z_a, z_i, uc, sp):
    r = jax.nn.sigmoid(z_a)
    i = jax.nn.sigmoid(z_i)
    log_a = (-LRU_C * r) * sp
    a = jnp.exp(log_a)
    b = jnp.sqrt(-jnp.tanh(log_a) * (a * a + 1.0)) * (i * uc)
    return a, b


def _rg_prompt_kernel(x_ref, g_ref, win_ref, cw_ref, cb_ref, wg_ref, ba_ref, bi_ref, lam_ref,
                      wout_ref, xo_ref, conv_ref, hl_ref, ext_ref, a_ref, b_ref, gg_ref, hc_ref):
    tt = RG_TILE
    t = pl.program_id(1)

    @pl.when(t == 0)
    def _():
        ext_ref[0:SUBLANES, :] = jnp.zeros((SUBLANES, LRU_WIDTH), F32)
        hc_ref[...] = jnp.zeros_like(hc_ref)

    x = x_ref[0]
    xn = _rms(x, g_ref[...]).astype(BF16)
    proj = jnp.dot(xn, win_ref[...], preferred_element_type=F32)
    gg_ref[...] = jax.nn.gelu(proj[:, :LRU_WIDTH])
    ext_ref[SUBLANES:SUBLANES + tt, :] = proj[:, LRU_WIDTH:]

    base = SUBLANES - (CONV_WIDTH - 1)
    ext = ext_ref[...]

    def lagged(j):
        lag = CONV_WIDTH - 1 - j
        rolled = pltpu.roll(ext, lag, 0) if lag else ext
        return rolled[SUBLANES:SUBLANES + tt, :]

    uc = cb_ref[...] + lagged(0) * cw_ref[0:1, :]
    for j in range(1, CONV_WIDTH):
        uc = uc + lagged(j) * cw_ref[j:j + 1, :]
    conv_ref[0] = ext_ref[tt + base:tt + SUBLANES, :]
    ext_ref[0:SUBLANES, :] = ext_ref[tt:tt + SUBLANES, :]

    sp = _softplus(-lam_ref[...])
    for n in range(N_GATE_BLOCKS):
        sl = slice(n * GATE_BLOCK, (n + 1) * GATE_BLOCK)
        ucn = uc[:, sl]
        z = jnp.dot(ucn.astype(BF16), wg_ref[n], preferred_element_type=F32)
        a, b = _lru_coeffs(z[:, :GATE_BLOCK] + ba_ref[:, sl], z[:, GATE_BLOCK:] + bi_ref[:, sl],
                           ucn, sp[:, sl])
        a_ref[:, sl] = a
        b_ref[:, sl] = b

    row = lax.broadcasted_iota(jnp.int32, (SUBLANES, LRU_WIDTH), 0)

    def group(gi, h_prev):
        r0 = pl.multiple_of(gi * SUBLANES, SUBLANES)
        a8 = a_ref[pl.ds(r0, SUBLANES), :]
        b8 = b_ref[pl.ds(r0, SUBLANES), :]
        s = 1
        while s < SUBLANES:
            keep = row >= s
            b8 = jnp.where(keep, a8 * pltpu.roll(b8, s, 0) + b8, b8)
            a8 = jnp.where(keep, a8 * pltpu.roll(a8, s, 0), a8)
            s *= 2
        h = a8 * h_prev + b8
        b_ref[pl.ds(r0, SUBLANES), :] = h
        return h[SUBLANES - 1:SUBLANES, :]

    h_last = lax.fori_loop(0, tt // SUBLANES, group, hc_ref[...])
    hc_ref[...] = h_last
    hl_ref[0] = h_last

    y = jnp.dot((b_ref[...] * gg_ref[...]).astype(BF16), wout_ref[...], preferred_element_type=F32)
    xo_ref[0] = x + y


def _rg_prompt(x, g, w_in, conv_w, conv_b, wg, ba, bi, lam, w_out):
    bsz, t, d = x.shape
    tt = RG_TILE
    vec = lambda n: _resident((1, n))
    return pl.pallas_call(
        _rg_prompt_kernel,
        name="rg_prompt",
        grid=(bsz, t // tt),
        in_specs=[
            pl.BlockSpec((1, tt, d), lambda b, i: (b, i, 0)),
            vec(d),
            _resident((d, 2 * LRU_WIDTH)),
            _resident((CONV_WIDTH, LRU_WIDTH)),
            vec(LRU_WIDTH),
            _resident((N_GATE_BLOCKS, GATE_BLOCK, 2 * GATE_BLOCK)),
            vec(LRU_WIDTH), vec(LRU_WIDTH), vec(LRU_WIDTH),
            _resident((LRU_WIDTH, d)),
        ],
        out_specs=[
            pl.BlockSpec((1, tt, d), lambda b, i: (b, i, 0)),
            pl.BlockSpec((1, CONV_WIDTH - 1, LRU_WIDTH), lambda b, i: (b, 0, 0)),
            pl.BlockSpec((1, 1, LRU_WIDTH), lambda b, i: (b, 0, 0)),
        ],
        out_shape=[
            jax.ShapeDtypeStruct((bsz, t, d), F32),
            jax.ShapeDtypeStruct((bsz, CONV_WIDTH - 1, LRU_WIDTH), F32),
            jax.ShapeDtypeStruct((bsz, 1, LRU_WIDTH), F32),
        ],
        scratch_shapes=[
            pltpu.VMEM((tt + SUBLANES, LRU_WIDTH), F32),
            pltpu.VMEM((tt, LRU_WIDTH), F32),
            pltpu.VMEM((tt, LRU_WIDTH), F32),
            pltpu.VMEM((tt, LRU_WIDTH), F32),
            pltpu.VMEM((1, LRU_WIDTH), F32),
        ],
        compiler_params=_params("arbitrary", "arbitrary"),
    )(x, g, w_in, conv_w, conv_b, wg, ba, bi, lam, w_out)


def _ffn_body(x_ref, g_ref, win_ref, wout_ref, gf_ref, o_ref, act_ref, *, final, precision):
    cdt = win_ref.dtype
    x = x_ref[...]
    xn = _rms(x, g_ref[...]).astype(cdt)
    for c in range(D_FF // FF_CHUNK):
        lo = c * FF_CHUNK
        gate = jnp.dot(xn, win_ref[:, lo:lo + FF_CHUNK], preferred_element_type=F32,
                       precision=precision)
        up = jnp.dot(xn, win_ref[:, D_FF + lo:D_FF + lo + FF_CHUNK], preferred_element_type=F32,
                     precision=precision)
        act_ref[:, lo:lo + FF_CHUNK] = (jax.nn.silu(gate) * up).astype(cdt)
    y = x + jnp.dot(act_ref[...], wout_ref[...], preferred_element_type=F32, precision=precision)
    if final:
        y = _rms(y, gf_ref[...])
    o_ref[...] = y


def _ffn_kernel(*refs, final, precision):
    _ffn_body(*refs, final=final, precision=precision)


def _ffn_kmeans_kernel(pt_ref, x_ref, g_ref, win_ref, wout_ref, gf_ref, ck_ref, o_ref, km_ref,
                       act_ref, kbuf, sem, *, final, precision, n_pages, first_page):
    s = pl.program_id(0)
    cur = s % 2

    def copies(seq, buf):
        return [pltpu.make_async_copy(ck_ref.at[pt_ref[seq * n_pages + first_page + p]],
                                      kbuf.at[buf, p], sem.at[buf])
                for p in range(KMEANS_PAGES)]

    @pl.when(s == 0)
    def _():
        for cp in copies(s, cur):
            cp.start()

    @pl.when(s + 1 < pl.num_programs(0))
    def _():
        for cp in copies(s + 1, 1 - cur):
            cp.start()

    for cp in copies(s, cur):
        cp.wait()

    means = []
    for blk in range(KMEANS_PAGES // PAGES_PER_BLOCK):
        t = sum(jnp.sum(kbuf[cur, blk * PAGES_PER_BLOCK + h], axis=0)
                for h in range(PAGES_PER_BLOCK))
        t = sum(t[par * N_KV_HEADS:(par + 1) * N_KV_HEADS] for par in range(ROW_PARITIES))
        means.append(jnp.concatenate([t[kvh:kvh + 1, :] for kvh in range(N_KV_HEADS)], axis=1))
    km_ref[0] = jnp.concatenate(means, axis=0) * (1.0 / MOBA_BLOCK)

    _ffn_body(x_ref, g_ref, win_ref, wout_ref, gf_ref, o_ref, act_ref, final=final,
              precision=precision)


def _ffn_attn_kernel(pt_ref, sel_ref, x_ref, g_ref, win_ref, wout_ref, gf_ref, q_ref, kn_ref,
                     vn_ref, ck_ref, cv_ref, o_ref, ao_ref, act_ref, kbuf, vbuf, sem, *,
                     final, precision, n_pages):
    _attn_sample_body(pt_ref, sel_ref, q_ref, kn_ref, vn_ref, ck_ref, cv_ref, ao_ref, kbuf, vbuf,
                      sem, n_pages=n_pages)
    _ffn_body(x_ref, g_ref, win_ref, wout_ref, gf_ref, o_ref, act_ref, final=final,
              precision=precision)


def _ffn(x, g, w_in, w_out, g_final, layer, *, final, tile, kmeans=None, attn=None):
    n, d = x.shape
    precision = HIGHEST if w_in.dtype == F32 else None
    in_specs = [
        pl.BlockSpec((tile, d), lambda i, *_: (i, 0)),
        _resident((1, d)),
        _layer(layer, (d, 2 * D_FF)),
        _layer(layer, (D_FF, d)),
        _resident((1, d)),
    ]
    out_spec = pl.BlockSpec((tile, d), lambda i, *_: (i, 0))
    out_shape = jax.ShapeDtypeStruct((n, d), F32)
    act = pltpu.VMEM((tile, D_FF), w_in.dtype)
    any_space = pl.BlockSpec(memory_space=pl.ANY)
    if attn is not None:
        pt_flat, sel_flat, q3, k_new3, v_new3, ck, cv, n_pages = attn
        n_seq = n // tile
        assert q3.shape[0] == n_seq
        n_slots = MOBA_TOPK * PAGES_PER_BLOCK
        row = lambda width: pl.BlockSpec((1, 1, width), lambda i, *_: (i, 0, 0))
        gathered = pltpu.VMEM((2, N_HEADS, n_slots, PAGE_SIZE, HEAD_DIM), F32)
        return pl.pallas_call(
            functools.partial(_ffn_attn_kernel, final=final, precision=precision, n_pages=n_pages),
            name="ffn_attn",
            grid_spec=pltpu.PrefetchScalarGridSpec(
                num_scalar_prefetch=2,
                grid=(n_seq,),
                in_specs=in_specs + [row(D_MODEL), row(KV_DIM), row(KV_DIM), any_space, any_space],
                out_specs=[out_spec, row(D_MODEL)],
                scratch_shapes=[act, gathered, gathered, pltpu.SemaphoreType.DMA((2, 2))],
            ),
            out_shape=[out_shape, jax.ShapeDtypeStruct((n_seq, 1, D_MODEL), F32)],
            compiler_params=_params("arbitrary"),
        )(pt_flat, sel_flat, x, g, w_in, w_out, g_final, q3, k_new3, v_new3, ck, cv)
    if kmeans is None:
        return pl.pallas_call(
            functools.partial(_ffn_kernel, final=final, precision=precision),
            name="ffn",
            grid=(n // tile,),
            in_specs=in_specs,
            out_specs=out_spec,
            out_shape=out_shape,
            scratch_shapes=[act],
            compiler_params=_params("arbitrary"),
        )(x, g, w_in, w_out, g_final)

    pt_flat, cache, n_pages, first_page = kmeans
    n_seq = n // tile
    assert pt_flat.shape[0] == n_seq * n_pages
    blocks = KMEANS_PAGES // PAGES_PER_BLOCK
    return pl.pallas_call(
        functools.partial(_ffn_kmeans_kernel, final=final, precision=precision, n_pages=n_pages,
                          first_page=first_page),
        name="ffn_kmeans",
        grid_spec=pltpu.PrefetchScalarGridSpec(
            num_scalar_prefetch=1,
            grid=(n_seq,),
            in_specs=in_specs + [any_space],
            out_specs=[out_spec, pl.BlockSpec((1, blocks, KV_DIM), lambda i, *_: (i, 0, 0))],
            scratch_shapes=[act, pltpu.VMEM((2, KMEANS_PAGES) + PAGE_VIEW, F32),
                            pltpu.SemaphoreType.DMA((2,))],
        ),
        out_shape=[out_shape, jax.ShapeDtypeStruct((n_seq, blocks, KV_DIM), F32)],
        compiler_params=_params("arbitrary"),
    )(pt_flat, x, g, w_in, w_out, g_final, cache)


def _kv_prompt_kernel(x_ref, g_ref, w_ref, c_ref, slo_ref, shi_ref,
                      k_ref, v_ref, kb_ref, vt_ref, km_ref):
    xn = _rms(x_ref[0], g_ref[...]).astype(BF16)
    kv = jnp.dot(xn, w_ref[...], preferred_element_type=F32)
    tile4 = lambda r: jnp.concatenate([r[...]] * N_KV_HEADS, axis=1)
    k = _rope(kv[:, :KV_DIM], tile4(c_ref), tile4(slo_ref), tile4(shi_ref))
    v = kv[:, KV_DIM:]
    k_ref[0] = k
    v_ref[0] = v
    kb_ref[0] = k.astype(BF16)
    for bb in range(KV_TILE // MOBA_BLOCK):
        rows = slice(bb * MOBA_BLOCK, (bb + 1) * MOBA_BLOCK)
        vt_ref[0, bb] = v[rows, :].T.astype(BF16)
        km_ref[0, bb] = jnp.sum(k[rows, :], axis=0, keepdims=True) * (1.0 / MOBA_BLOCK)


def _kv_prompt(x, g, w_kv, c, s_lo, s_hi):
    bsz, t, d = x.shape
    tt = KV_TILE
    nbt = tt // MOBA_BLOCK
    nb = t // MOBA_BLOCK
    tab = pl.BlockSpec((tt, HEAD_DIM), lambda b, i: (i, 0))
    row_spec = pl.BlockSpec((1, tt, KV_DIM), lambda b, i: (b, i, 0))
    return pl.pallas_call(
        _kv_prompt_kernel,
        name="kv_prompt",
        grid=(bsz, t // tt),
        in_specs=[
            pl.BlockSpec((1, tt, d), lambda b, i: (b, i, 0)),
            _resident((1, d)),
            _resident((d, 2 * KV_DIM)),
            tab, tab, tab,
        ],
        out_specs=[
            row_spec, row_spec, row_spec,
            pl.BlockSpec((1, nbt, KV_DIM, MOBA_BLOCK), lambda b, i: (b, i, 0, 0)),
            pl.BlockSpec((1, nbt, 1, KV_DIM), lambda b, i: (b, i, 0, 0)),
        ],
        out_shape=[
            jax.ShapeDtypeStruct((bsz, t, KV_DIM), F32),
            jax.ShapeDtypeStruct((bsz, t, KV_DIM), F32),
            jax.ShapeDtypeStruct((bsz, t, KV_DIM), BF16),
            jax.ShapeDtypeStruct((bsz, nb, KV_DIM, MOBA_BLOCK), BF16),
            jax.ShapeDtypeStruct((bsz, nb, 1, KV_DIM), F32),
        ],
        compiler_params=_params("arbitrary", "arbitrary"),
    )(x, g, w_kv, c, s_lo, s_hi)


def _select_bias(gate, n_allowed, n_valid):
    nb, nq = gate.shape
    rowi = lax.broadcasted_iota(jnp.int32, (nb, nq), 0).astype(F32)
    g = jnp.where(rowi < n_allowed, gate, NEG)
    bias = jnp.where(rowi == n_allowed, 0.0, NEG)
    for r in range(MOBA_TOPK):
        mx = jnp.max(g, axis=0, keepdims=True)
        idx = jnp.min(jnp.where(g == mx, rowi, float(nb)), axis=0, keepdims=True)
        ok = (n_valid > r).astype(F32)
        idx = idx * ok - (1.0 - ok)
        pick = rowi == idx
        bias = jnp.where(pick, 0.0, bias)
        g = jnp.where(pick, NEG, g)
    return bias


def _attn_prompt_kernel(x_ref, g_ref, wq_ref, wo_ref, c_ref, slo_ref, shi_ref, k_ref, vt_ref,
                        km_ref, o_ref, q2s_ref, s0_ref, s1_ref, m_ref, acc_ref, attn_ref):
    tq = MOBA_BLOCK
    nq = ATTN_CHAIN_HEADS * tq
    i = pl.program_id(1)
    x = x_ref[0]
    xn = _rms(x, g_ref[...]).astype(BF16)
    q = jnp.dot(xn, wq_ref[...], preferred_element_type=F32)
    c, s_lo, s_hi = c_ref[...], slo_ref[...], shi_ref[...]
    own = i.astype(F32)

    key_pos = lax.broadcasted_iota(jnp.int32, (tq, nq), 0)
    qry_pos = lax.broadcasted_iota(jnp.int32, (tq, nq), 1) % tq
    causal = key_pos <= qry_pos
    n_chains = N_HEADS // ATTN_CHAIN_HEADS
    chain_heads = [range(ch * ATTN_CHAIN_HEADS, (ch + 1) * ATTN_CHAIN_HEADS)
                   for ch in range(n_chains)]
    kv_lanes = [slice((hs[0] // GROUP) * HEAD_DIM, (hs[0] // GROUP + 1) * HEAD_DIM)
                for hs in chain_heads]

    ones_rows = jnp.where(lax.broadcasted_iota(jnp.int32, (BF16_ROWS, tq), 0) == 0, 1.0, 0.0
                          ).astype(BF16)

    def vt_ones(j, lanes):
        return jnp.concatenate([vt_ref[0, j, lanes, :], ones_rows], axis=0)

    for kvh, lanes in enumerate(kv_lanes):
        q2 = jnp.concatenate(
            [_rope(q[:, h * HEAD_DIM:(h + 1) * HEAD_DIM], c, s_lo, s_hi)
             for h in chain_heads[kvh]], axis=0)
        gate = lax.dot_general(km_ref[0, :, lanes], q2, NT_DIMS, precision=HIGHEST,
                               preferred_element_type=F32)
        bias = _select_bias(gate, own, own)
        bias_t = jnp.concatenate(
            [bias, jnp.zeros((HEAD_DIM - bias.shape[0], nq), F32)], axis=0).T
        q2s = (q2 * (SM_SCALE * LOG2E)).astype(BF16)
        q2s_ref[kvh] = jnp.concatenate([q2s, bias_t.astype(BF16)], axis=1)
        m_ref[kvh] = jnp.full((1, nq), NEG, F32)
        acc_ref[kvh] = jnp.zeros(acc_ref.shape[1:], F32)

    block_lane = lax.broadcasted_iota(jnp.int32, (1, HEAD_DIM), 1)

    def qk(j, s_out):
        one_hot = jnp.broadcast_to(jnp.where(block_lane == j, 1.0, 0.0).astype(BF16),
                                   (tq, HEAD_DIM))
        for kvh, lanes in enumerate(kv_lanes):
            k_aug = jnp.concatenate([k_ref[0, j, :, lanes], one_hot], axis=1)
            s_out[kvh] = lax.dot_general(k_aug, q2s_ref[kvh], NT_DIMS,
                                         preferred_element_type=F32)

    def block(j, s_cur, s_nxt):
        if s_nxt is not None:
            qk(j + 1, s_nxt)
        for kvh, lanes in enumerate(kv_lanes):
            s = s_cur[kvh]
            if s_nxt is None:
                s = jnp.where(causal, s, NEG)
            m = m_ref[kvh]
            m_new = jnp.maximum(m, jnp.max(s, axis=0, keepdims=True))
            alpha = jnp.exp2(m - m_new)
            p = jnp.exp2(s - m_new)
            m_ref[kvh] = m_new
            acc_ref[kvh] = alpha * acc_ref[kvh] + jnp.dot(
                vt_ones(j, lanes), p.astype(BF16), preferred_element_type=F32)

    qk(0, s0_ref)
    bufs = (s0_ref, s1_ref)

    def run(first, count):
        for k in range(count):
            block(first + k, bufs[k % 2], bufs[(k + 1) % 2])

    def long_trip(t, carry):
        run(ATTN_UNROLL * t, ATTN_UNROLL)
        return carry

    n_long = i // ATTN_UNROLL
    lax.fori_loop(0, n_long, long_trip, 0)
    done = n_long * ATTN_UNROLL

    def pair_trip(t, carry):
        run(done + 2 * t, 2)
        return carry

    lax.fori_loop(0, (i - done) // 2, pair_trip, 0)

    @pl.when(i % 2 == 1)
    def _():
        run(i - 1, 1)
        block(i, s1_ref, None)

    @pl.when(i % 2 == 0)
    def _():
        block(i, s0_ref, None)

    for kvh in range(n_chains):
        o = (acc_ref[kvh, :HEAD_DIM, :] / acc_ref[kvh, HEAD_DIM:HEAD_DIM + 1, :]).T
        for j, h in enumerate(chain_heads[kvh]):
            attn_ref[:, h * HEAD_DIM:(h + 1) * HEAD_DIM] = o[j * tq:(j + 1) * tq, :].astype(BF16)

    o_ref[0] = x + jnp.dot(attn_ref[...], wo_ref[...], preferred_element_type=F32)


def _attn_prompt(x, g, w_q, w_o, c, s_lo, s_hi, kb, vt, km):
    bsz, t, d = x.shape
    tq = MOBA_BLOCK
    nb = t // tq
    n_chains, nq = N_HEADS // ATTN_CHAIN_HEADS, ATTN_CHAIN_HEADS * tq
    tab = pl.BlockSpec((tq, HEAD_DIM), lambda b, i: (i, 0))
    per_seq = lambda shape: pl.BlockSpec((1,) + shape, lambda b, i: (b,) + (0,) * len(shape),
                                         pipeline_mode=pl.Buffered(1))
    return pl.pallas_call(
        _attn_prompt_kernel,
        name="attn_prompt",
        grid=(bsz, nb),
        in_specs=[
            pl.BlockSpec((1, tq, d), lambda b, i: (b, i, 0)),
            _resident((1, d)),
            _resident((d, d)),
            _resident((d, d)),
            tab, tab, tab,
            per_seq((nb, tq, KV_DIM)),
            per_seq((nb, KV_DIM, tq)),
            per_seq((nb, KV_DIM)),
        ],
        out_specs=pl.BlockSpec((1, tq, d), lambda b, i: (b, i, 0)),
        out_shape=jax.ShapeDtypeStruct((bsz, t, d), F32),
        scratch_shapes=[
            pltpu.VMEM((n_chains, nq, 2 * HEAD_DIM), BF16),
            pltpu.VMEM((n_chains, tq, nq), F32),
            pltpu.VMEM((n_chains, tq, nq), F32),
            pltpu.VMEM((n_chains, 1, nq), F32),
            pltpu.VMEM((n_chains, HEAD_DIM + BF16_ROWS, nq), F32),
            pltpu.VMEM((tq, d), BF16),
        ],
        compiler_params=_params("arbitrary", "arbitrary"),
    )(x, g, w_q, w_o, c, s_lo, s_hi, kb, vt, km)


def _dot_hi(a, b):
    return jnp.dot(a, b, preferred_element_type=F32, precision=HIGHEST)


def _rg_sample_kernel(x_ref, cs_ref, h0_ref, g_ref, win_ref, cw_ref, cb_ref, wa_ref, wi_ref,
                      ba_ref, bi_ref, lam_ref, wout_ref, xo_ref, cso_ref, ho_ref):
    x = x_ref[...]
    proj = _dot_hi(_rms(x, g_ref[...]), win_ref[...])
    gate, u = proj[:, :LRU_WIDTH], proj[:, LRU_WIDTH:]
    uc = cb_ref[...] + cs_ref[0] * cw_ref[0:1, :]
    for j in range(1, CONV_WIDTH - 1):
        uc = uc + cs_ref[j] * cw_ref[j:j + 1, :]
    uc = uc + u * cw_ref[CONV_WIDTH - 1:CONV_WIDTH, :]
    for j in range(CONV_WIDTH - 2):
        cso_ref[j] = cs_ref[j + 1]
    cso_ref[CONV_WIDTH - 2] = u

    sp = _softplus(-lam_ref[...])
    h0 = h0_ref[...]
    hs = []
    for n in range(N_GATE_BLOCKS):
        sl = slice(n * GATE_BLOCK, (n + 1) * GATE_BLOCK)
        ucn = uc[:, sl]
        a, b = _lru_coeffs(_dot_hi(ucn, wa_ref[n]) + ba_ref[:, sl],
                           _dot_hi(ucn, wi_ref[n]) + bi_ref[:, sl], ucn, sp[:, sl])
        hs.append(b + a * h0[:, sl])
    h = jnp.concatenate(hs, axis=1)
    ho_ref[...] = h
    xo_ref[...] = x + _dot_hi(h * jax.nn.gelu(gate), wout_ref[...])


def _rg_sample(x, cs, h0, g, w_in, conv_w, conv_b, wa, wi, ba, bi, lam, w_out, layer):
    n, d = x.shape
    vec = _resident((1, LRU_WIDTH))
    gates = _layer(layer, (N_GATE_BLOCKS, GATE_BLOCK, GATE_BLOCK))
    return pl.pallas_call(
        _rg_sample_kernel,
        name="rg_sample",
        grid=(1,),
        in_specs=[
            _resident((n, d)), _resident((CONV_WIDTH - 1, n, LRU_WIDTH)), _resident((n, LRU_WIDTH)),
            _resident((1, d)), _layer(layer, (d, 2 * LRU_WIDTH)), _resident((CONV_WIDTH, LRU_WIDTH)),
            vec, gates, gates, vec, vec, vec, _layer(layer, (LRU_WIDTH, d)),
        ],
        out_specs=[_whole((n, d)), _whole((CONV_WIDTH - 1, n, LRU_WIDTH)), _whole((n, LRU_WIDTH))],
        out_shape=[
            jax.ShapeDtypeStruct((n, d), F32),
            jax.ShapeDtypeStruct((CONV_WIDTH - 1, n, LRU_WIDTH), F32),
            jax.ShapeDtypeStruct((n, LRU_WIDTH), F32),
        ],
        compiler_params=_params("arbitrary"),
    )(x, cs, h0, g, w_in, conv_w, conv_b, wa, wi, ba, bi, lam, w_out)


def _proj_sample_kernel(x_ref, g_ref, w_ref, c_ref, slo_ref, shi_ref, o_ref, *, n_rope_heads):
    y = _dot_hi(_rms(x_ref[...], g_ref[...]), w_ref[...])
    if n_rope_heads:
        tile = lambda r: jnp.concatenate([r[...]] * n_rope_heads, axis=1)
        width = n_rope_heads * HEAD_DIM
        rot = _rope(y[:, :width], tile(c_ref), tile(slo_ref), tile(shi_ref))
        y = jnp.concatenate([rot, y[:, width:]], axis=1) if width < y.shape[1] else rot
    o_ref[...] = y


def _proj_sample(x, g, w, layer, c, s_lo, s_hi, n_rope_heads):
    n, d = x.shape
    width = w.shape[2]
    tab = _resident((1, HEAD_DIM))
    return pl.pallas_call(
        functools.partial(_proj_sample_kernel, n_rope_heads=n_rope_heads),
        name="proj_sample",
        grid=(1,),
        in_specs=[_resident((n, d)), _resident((1, d)), _layer(layer, (d, width)), tab, tab, tab],
        out_specs=_whole((n, width)),
        out_shape=jax.ShapeDtypeStruct((n, width), F32),
        compiler_params=_params("arbitrary"),
    )(x, g, w, c, s_lo, s_hi)


def _gate_sample_kernel(q_ref, km_ref, sel_ref):
    km = km_ref[...]
    n_seq, nb, _ = km.shape
    blk = lax.broadcasted_iota(jnp.int32, (n_seq, nb, 1), 1).astype(F32)
    lane = lax.broadcasted_iota(jnp.int32, (n_seq, 1, HEAD_DIM), 2)
    for h in range(N_HEADS):
        kvh = h // GROUP
        qh = q_ref[:, :, h * HEAD_DIM:(h + 1) * HEAD_DIM]
        g = jnp.sum(km[:, :, kvh * HEAD_DIM:(kvh + 1) * HEAD_DIM] * qh, axis=-1, keepdims=True)
        out = jnp.zeros((n_seq, 1, HEAD_DIM), F32)
        for r in range(MOBA_TOPK):
            mx = jnp.max(g, axis=1, keepdims=True)
            idx = jnp.min(jnp.where(g == mx, blk, float(nb)), axis=1, keepdims=True)
            g = jnp.where(blk == idx, NEG, g)
            out = jnp.where(lane == r, idx, out)
        sel_ref[h] = out.astype(jnp.int32)


def _gate_sample(q3, km):
    n_seq = q3.shape[0]
    return pl.pallas_call(
        _gate_sample_kernel,
        name="gate_sample",
        out_shape=jax.ShapeDtypeStruct((N_HEADS, n_seq, 1, HEAD_DIM), jnp.int32),
        compiler_params=pltpu.CompilerParams(vmem_limit_bytes=VMEM_LIMIT),
    )(q3, km)


def _attn_sample_body(pt_ref, sel_ref, q_ref, kn_ref, vn_ref, ck_ref, cv_ref, o_ref,
                      kbuf, vbuf, sem, *, n_pages):
    b = pl.program_id(0)
    n_slots = MOBA_TOPK * PAGES_PER_BLOCK
    rows = PAGE_SIZE // ROW_PARITIES

    def copies(seq, buf):
        out = []
        for h in range(N_HEADS):
            kvh = h // GROUP
            for r in range(MOBA_TOPK):
                blk = sel_ref[(seq * N_HEADS + h) * MOBA_TOPK + r]
                for half in range(PAGES_PER_BLOCK):
                    page = pt_ref[seq * n_pages + PAGES_PER_BLOCK * blk + half]
                    slot = r * PAGES_PER_BLOCK + half
                    for par in range(ROW_PARITIES):
                        sub = par * N_KV_HEADS + kvh
                        dst = (buf, h, slot, pl.ds(par * rows, rows))
                        out.append(pltpu.make_async_copy(ck_ref.at[page, :, sub], kbuf.at[dst], sem.at[0, buf]))
                        out.append(pltpu.make_async_copy(cv_ref.at[page, :, sub], vbuf.at[dst], sem.at[1, buf]))
        return out

    cur = b % 2

    def start_all(seq, buf):
        for k, cp in enumerate(copies(seq, buf)):
            cp.start(priority=(k // 2) % 2)

    @pl.when(b == 0)
    def _():
        start_all(b, cur)

    @pl.when(b + 1 < pl.num_programs(0))
    def _():
        start_all(b + 1, 1 - cur)

    for cp in copies(b, cur):
        cp.wait()

    for h in range(N_HEADS):
        kvh = h // GROUP
        qh = q_ref[0, :, h * HEAD_DIM:(h + 1) * HEAD_DIM] * SM_SCALE
        kn = kn_ref[0, :, kvh * HEAD_DIM:(kvh + 1) * HEAD_DIM]
        vn = vn_ref[0, :, kvh * HEAD_DIM:(kvh + 1) * HEAD_DIM]
        kk = kbuf[cur, h].reshape(n_slots * PAGE_SIZE, HEAD_DIM)
        vv = vbuf[cur, h].reshape(n_slots * PAGE_SIZE, HEAD_DIM)
        s = jnp.sum(kk * qh, axis=-1, keepdims=True)
        s_new = jnp.sum(kn * qh, axis=-1, keepdims=True)
        m = jnp.maximum(jnp.max(s, axis=0, keepdims=True), s_new)
        p = jnp.exp(s - m)
        p_new = jnp.exp(s_new - m)
        l = jnp.sum(p, axis=0, keepdims=True) + p_new
        acc = jnp.sum(p * vv, axis=0, keepdims=True) + p_new * vn
        o_ref[0, :, h * HEAD_DIM:(h + 1) * HEAD_DIM] = acc / l


def _oproj_sample_kernel(x_ref, a_ref, w_ref, o_ref):
    o_ref[...] = x_ref[...] + _dot_hi(a_ref[...], w_ref[...])


def _oproj_sample(x, attn, w_o, layer):
    n, d = x.shape
    return pl.pallas_call(
        _oproj_sample_kernel,
        name="oproj_sample",
        grid=(1,),
        in_specs=[_resident((n, d)), _resident((n, d)), _layer(layer, (d, d))],
        out_specs=_whole((n, d)),
        out_shape=jax.ShapeDtypeStruct((n, d), F32),
        compiler_params=_params("arbitrary"),
    )(x, attn, w_o)


def _step(x_prompt, x_sample3, state_conv, state_h, ck, cv, pt_flat, n_pages, w):
    bsz, t, d = x_prompt.shape
    n_seq = x_sample3.shape[0]
    row = lambda a: a.reshape(1, -1)
    bf = lambda a: a.astype(BF16)
    n_rg = w["w_rg_in"].shape[0]
    depth = w["g_mix"].shape[0]
    nb = t // MOBA_BLOCK
    assert n_pages == n_rg * KMEANS_PAGES and (bsz * t) // FFN_TILE == n_seq

    c_p, slo_p, shi_p = _rope_tables(jnp.arange(t, dtype=jnp.int32))
    c_s, slo_s, shi_s = _rope_tables(jnp.full((1,), n_pages * PAGE_SIZE, jnp.int32))
    w_ffn_in_bf, w_ffn_out_bf = bf(w["w_ffn_in"]), bf(w["w_ffn_out"])

    def ffn_prompt(x, l, **side):
        out = _ffn(x.reshape(bsz * t, d), row(w["g_ffn"][l]), w_ffn_in_bf, w_ffn_out_bf,
                   row(w["g_final"]), l, final=(l == depth - 1), tile=FFN_TILE, **side)
        return out[0].reshape(bsz, t, d), out[1]

    def ffn_sample(x, l):
        w_in, w_out = (w["w_ffn_in"], w["w_ffn_out"]) if l == 0 else (w_ffn_in_bf, w_ffn_out_bf)
        return _ffn(x, row(w["g_ffn"][l]), w_in, w_out, row(w["g_final"]), l,
                    final=(l == depth - 1), tile=n_seq)

    xp, xs = x_prompt, x_sample3.reshape(n_seq, d)
    p_convs, p_hs, s_convs, s_hs, kmeans = [], [], [], [], []
    for l in range(n_rg):
        wg = jnp.concatenate([w["rg_gate_a_w"][l], w["rg_gate_i_w"][l]], axis=-1)
        xp, conv, h = _rg_prompt(
            xp, row(w["g_mix"][l]), bf(w["w_rg_in"][l]), w["rg_conv_w"][l], row(w["rg_conv_b"][l]),
            bf(wg), row(w["rg_gate_a_b"][l]), row(w["rg_gate_i_b"][l]), row(w["rg_lambda"][l]),
            bf(w["w_rg_out"][l]))
        p_convs.append(conv)
        p_hs.append(h[:, 0, :])
        xp, km_l = ffn_prompt(xp, l, kmeans=(pt_flat, ck, n_pages, l * KMEANS_PAGES))
        kmeans.append(km_l)

        xs, conv, h = _rg_sample(
            xs, jnp.swapaxes(state_conv[l], 0, 1), state_h[l], row(w["g_mix"][l]), w["w_rg_in"],
            w["rg_conv_w"][l], row(w["rg_conv_b"][l]), w["rg_gate_a_w"], w["rg_gate_i_w"],
            row(w["rg_gate_a_b"][l]), row(w["rg_gate_i_b"][l]), row(w["rg_lambda"][l]), w["w_rg_out"], l)
        s_convs.append(jnp.swapaxes(conv, 0, 1))
        s_hs.append(h)
        xs = ffn_sample(xs, l)

    k, v, kb, vt, km_p = _kv_prompt(xp, row(w["g_kv"]), bf(w["w_kv"]), c_p, slo_p, shi_p)
    kb = kb.reshape(bsz, nb, MOBA_BLOCK, KV_DIM)
    km_p = km_p.reshape(bsz, nb, KV_DIM)
    kv_s = _proj_sample(xs, row(w["g_kv"]), w["w_kv"][None], 0, c_s, slo_s, shi_s, N_KV_HEADS)
    k_new, v_new = kv_s[:, :KV_DIM], kv_s[:, KV_DIM:]
    km_s = jnp.concatenate(kmeans, axis=1)

    for a in range(depth - n_rg):
        l = n_rg + a
        xp = _attn_prompt(xp, row(w["g_mix"][l]), bf(w["w_q"][a]), bf(w["w_o"][a]), c_p, slo_p, shi_p,
                          kb, vt, km_p)
        q3 = _proj_sample(xs, row(w["g_mix"][l]), w["w_q"], a, c_s, slo_s, shi_s,
                          N_HEADS).reshape(n_seq, 1, d)
        sel = _gate_sample(q3, km_s)[:, :, 0, :MOBA_TOPK]
        sel_flat = jnp.transpose(sel, (1, 0, 2)).reshape(-1)
        xp, attn_s = ffn_prompt(xp, l, attn=(pt_flat, sel_flat, q3, k_new.reshape(n_seq, 1, KV_DIM),
                                             v_new.reshape(n_seq, 1, KV_DIM), ck, cv, n_pages))
        xs = _oproj_sample(xs, attn_s.reshape(n_seq, d), w["w_o"], a)
        xs = ffn_sample(xs, l)

    shape_p = (bsz, t, N_KV_HEADS, HEAD_DIM)
    shape_s = (n_seq, 1, N_KV_HEADS, HEAD_DIM)
    return (xp, xs.reshape(n_seq, 1, d), jnp.stack(p_convs), jnp.stack(p_hs),
            k.reshape(shape_p), v.reshape(shape_p), jnp.stack(s_convs), jnp.stack(s_hs),
            k_new.reshape(shape_s), v_new.reshape(shape_s))


def kernel(x_prompt, x_sample, cache_k, cache_v, state_conv, state_h, page_table,
           g_mix, g_ffn, w_rg_in, rg_conv_w, rg_conv_b, rg_gate_a_w, rg_gate_a_b,
           rg_gate_i_w, rg_gate_i_b, rg_lambda, w_rg_out, g_kv, w_kv, w_q, w_o,
           w_ffn_in, w_ffn_out, g_final):
    assert x_sample.shape[1] == 1 and x_prompt.shape[1] % MOBA_BLOCK == 0
    assert MOBA_TOPK <= page_table.shape[1] // PAGES_PER_BLOCK
    w = dict(g_mix=g_mix, g_ffn=g_ffn, w_rg_in=w_rg_in, rg_conv_w=rg_conv_w, rg_conv_b=rg_conv_b,
             rg_gate_a_w=rg_gate_a_w, rg_gate_a_b=rg_gate_a_b, rg_gate_i_w=rg_gate_i_w,
             rg_gate_i_b=rg_gate_i_b, rg_lambda=rg_lambda, w_rg_out=w_rg_out, g_kv=g_kv, w_kv=w_kv,
             w_q=w_q, w_o=w_o, w_ffn_in=w_ffn_in, w_ffn_out=w_ffn_out, g_final=g_final)
    ck = cache_k.reshape((cache_k.shape[0],) + PAGE_VIEW)
    cv = cache_v.reshape((cache_v.shape[0],) + PAGE_VIEW)
    return _step(x_prompt, x_sample, state_conv, state_h, ck, cv, page_table.reshape(-1),
                 page_table.shape[1], w)
```

```python
import functools
import math

import jax
import jax.numpy as jnp
from jax import lax
from jax.experimental import pallas as pl
from jax.experimental.pallas import tpu as pltpu

D_MODEL = 1024
LRU_WIDTH = D_MODEL
N_GATE_BLOCKS = 8
GATE_BLOCK = LRU_WIDTH // N_GATE_BLOCKS
CONV_WIDTH = 4
LRU_C = 8.0
N_HEADS = 8
HEAD_DIM = D_MODEL // N_HEADS
N_KV_HEADS = 4
KV_DIM = N_KV_HEADS * HEAD_DIM
GROUP = N_HEADS // N_KV_HEADS
ROT_DIM = HEAD_DIM // 4
ROPE_THETA = 500000.0
MOBA_BLOCK = 256
MOBA_TOPK = 3
PAGE_SIZE = 128
PAGES_PER_BLOCK = MOBA_BLOCK // PAGE_SIZE
D_FF = -(-8 * D_MODEL // (3 * 256)) * 256
EPS = 1e-6
NEG = -1e30
SM_SCALE = 1.0 / math.sqrt(HEAD_DIM)
LOG2E = math.log2(math.e)

F32 = jnp.float32
BF16 = jnp.bfloat16
HIGHEST = lax.Precision.HIGHEST

SUBLANES = 8
BF16_ROWS = 16
FF_CHUNK = 256
RG_TILE = 512
FFN_TILE = 512
KV_TILE = 512
KMEANS_PAGES = 32
ATTN_UNROLL = 8
ATTN_CHAIN_HEADS = GROUP
ROW_PARITIES = SUBLANES // N_KV_HEADS
PAGE_VIEW = (PAGE_SIZE // ROW_PARITIES, SUBLANES, HEAD_DIM)
VMEM_LIMIT = 56 * 1024 * 1024

NT_DIMS = (((1,), (1,)), ((), ()))


def _params(*sem):
    return pltpu.CompilerParams(dimension_semantics=sem, vmem_limit_bytes=VMEM_LIMIT)


def _resident(shape):
    zeros = (0,) * len(shape)
    return pl.BlockSpec(shape, lambda *_: zeros, pipeline_mode=pl.Buffered(1))


def _whole(shape):
    zeros = (0,) * len(shape)
    return pl.BlockSpec(shape, lambda *_: zeros)


def _layer(layer, shape):
    zeros = (0,) * len(shape)
    return pl.BlockSpec((None,) + shape, lambda *_: (layer,) + zeros, pipeline_mode=pl.Buffered(1))


def _rms(x, g):
    ms = jnp.mean(x * x, axis=-1, keepdims=True)
    return x * lax.rsqrt(ms + EPS) * g


def _softplus(z):
    return jnp.maximum(z, 0.0) + jnp.log1p(jnp.exp(-jnp.abs(z)))


def _rope(x, c, s_lo, s_hi):
    n = x.shape[-1]
    half = ROT_DIM // 2
    return x * c + pltpu.roll(x, half, 1) * s_hi + pltpu.roll(x, n - half, 1) * s_lo


def _rope_tables(pos):
    half = ROT_DIM // 2
    inv = ROPE_THETA ** (-jnp.arange(half, dtype=F32) / half)
    ang = pos.astype(F32)[:, None] * inv[None, :]
    cos, sin = jnp.cos(ang), jnp.sin(ang)
    t = pos.shape[0]
    rest = HEAD_DIM - ROT_DIM
    c = jnp.concatenate([cos, cos, jnp.ones((t, rest), F32)], axis=1)
    s_hi = jnp.concatenate([jnp.zeros((t, half), F32), sin, jnp.zeros((t, rest), F32)], axis=1)
    s_lo = jnp.concatenate([-sin, jnp.zeros((t, half + rest), F32)], axis=1)
    return c, s_lo, s_hi


def _lru_coeffs(z_a, z_i, uc, sp):
    r = jax.nn.sigmoid(z_a)
    i = jax.nn.sigmoid(z_i)
    log_a = (-LRU_C * r) * sp
    a = jnp.exp(log_a)
    b = jnp.sqrt(-jnp.tanh(log_a) * (a * a + 1.0)) * (i * uc)
    return a, b


def _rg_prompt_kernel(x_ref, g_ref, win_ref, cw_ref, cb_ref, wg_ref, ba_ref, bi_ref, lam_ref,
                      wout_ref, xo_ref, conv_ref, hl_ref, ext_ref, a_ref, b_ref, gg_ref, hc_ref):
    tt = RG_TILE
    t = pl.program_id(1)

    @pl.when(t == 0)
    def _():
        ext_ref[0:SUBLANES, :] = jnp.zeros((SUBLANES, LRU_WIDTH), F32)
        hc_ref[...] = jnp.zeros_like(hc_ref)

    x = x_ref[0]
    xn = _rms(x, g_ref[...]).astype(BF16)
    proj = jnp.dot(xn, win_ref[...], preferred_element_type=F32)
    gg_ref[...] = jax.nn.gelu(proj[:, :LRU_WIDTH])
    ext_ref[SUBLANES:SUBLANES + tt, :] = proj[:, LRU_WIDTH:]

    base = SUBLANES - (CONV_WIDTH - 1)
    ext = ext_ref[...]

    def lagged(j):
        lag = CONV_WIDTH - 1 - j
        rolled = pltpu.roll(ext, lag, 0) if lag else ext
        return rolled[SUBLANES:SUBLANES + tt, :]

    uc = cb_ref[...] + lagged(0) * cw_ref[0:1, :]
    for j in range(1, CONV_WIDTH):
        uc = uc + lagged(j) * cw_ref[j:j + 1, :]
    conv_ref[0] = ext_ref[tt + base:tt + SUBLANES, :]
    ext_ref[0:SUBLANES, :] = ext_ref[tt:tt + SUBLANES, :]

    sp = _softplus(-lam_ref[...])
    for n in range(N_GATE_BLOCKS):
        sl = slice(n * GATE_BLOCK, (n + 1) * GATE_BLOCK)
        ucn = uc[:, sl]
        z = jnp.dot(ucn.astype(BF16), wg_ref[n], preferred_element_type=F32)
        a, b = _lru_coeffs(z[:, :GATE_BLOCK] + ba_ref[:, sl], z[:, GATE_BLOCK:] + bi_ref[:, sl],
                           ucn, sp[:, sl])
        a_ref[:, sl] = a
        b_ref[:, sl] = b

    row = lax.broadcasted_iota(jnp.int32, (SUBLANES, LRU_WIDTH), 0)

    def group(gi, h_prev):
        r0 = pl.multiple_of(gi * SUBLANES, SUBLANES)
        a8 = a_ref[pl.ds(r0, SUBLANES), :]
        b8 = b_ref[pl.ds(r0, SUBLANES), :]
        s = 1
        while s < SUBLANES:
            keep = row >= s
            b8 = jnp.where(keep, a8 * pltpu.roll(b8, s, 0) + b8, b8)
            a8 = jnp.where(keep, a8 * pltpu.roll(a8, s, 0), a8)
            s *= 2
        h = a8 * h_prev + b8
        b_ref[pl.ds(r0, SUBLANES), :] = h
        return h[SUBLANES - 1:SUBLANES, :]

    h_last = lax.fori_loop(0, tt // SUBLANES, group, hc_ref[...])
    hc_ref[...] = h_last
    hl_ref[0] = h_last

    y = jnp.dot((b_ref[...] * gg_ref[...]).astype(BF16), wout_ref[...], preferred_element_type=F32)
    xo_ref[0] = x + y


def _rg_prompt(x, g, w_in, conv_w, conv_b, wg, ba, bi, lam, w_out):
    bsz, t, d = x.shape
    tt = RG_TILE
    vec = lambda n: _resident((1, n))
    return pl.pallas_call(
        _rg_prompt_kernel,
        name="rg_prompt",
        grid=(bsz, t // tt),
        in_specs=[
            pl.BlockSpec((1, tt, d), lambda b, i: (b, i, 0)),
            vec(d),
            _resident((d, 2 * LRU_WIDTH)),
            _resident((CONV_WIDTH, LRU_WIDTH)),
            vec(LRU_WIDTH),
            _resident((N_GATE_BLOCKS, GATE_BLOCK, 2 * GATE_BLOCK)),
            vec(LRU_WIDTH), vec(LRU_WIDTH), vec(LRU_WIDTH),
            _resident((LRU_WIDTH, d)),
        ],
        out_specs=[
            pl.BlockSpec((1, tt, d), lambda b, i: (b, i, 0)),
            pl.BlockSpec((1, CONV_WIDTH - 1, LRU_WIDTH), lambda b, i: (b, 0, 0)),
            pl.BlockSpec((1, 1, LRU_WIDTH), lambda b, i: (b, 0, 0)),
        ],
        out_shape=[
            jax.ShapeDtypeStruct((bsz, t, d), F32),
            jax.ShapeDtypeStruct((bsz, CONV_WIDTH - 1, LRU_WIDTH), F32),
            jax.ShapeDtypeStruct((bsz, 1, LRU_WIDTH), F32),
        ],
        scratch_shapes=[
            pltpu.VMEM((tt + SUBLANES, LRU_WIDTH), F32),
            pltpu.VMEM((tt, LRU_WIDTH), F32),
            pltpu.VMEM((tt, LRU_WIDTH), F32),
            pltpu.VMEM((tt, LRU_WIDTH), F32),
            pltpu.VMEM((1, LRU_WIDTH), F32),
        ],
        compiler_params=_params("arbitrary", "arbitrary"),
    )(x, g, w_in, conv_w, conv_b, wg, ba, bi, lam, w_out)


def _ffn_body(x_ref, g_ref, win_ref, wout_ref, gf_ref, o_ref, act_ref, *, final, precision):
    cdt = win_ref.dtype
    x = x_ref[...]
    xn = _rms(x, g_ref[...]).astype(cdt)
    for c in range(D_FF // FF_CHUNK):
        lo = c * FF_CHUNK
        gate = jnp.dot(xn, win_ref[:, lo:lo + FF_CHUNK], preferred_element_type=F32,
                       precision=precision)
        up = jnp.dot(xn, win_ref[:, D_FF + lo:D_FF + lo + FF_CHUNK], preferred_element_type=F32,
                     precision=precision)
        act_ref[:, lo:lo + FF_CHUNK] = (jax.nn.silu(gate) * up).astype(cdt)
    y = x + jnp.dot(act_ref[...], wout_ref[...], preferred_element_type=F32, precision=precision)
    if final:
        y = _rms(y, gf_ref[...])
    o_ref[...] = y


def _ffn_kernel(*refs, final, precision):
    _ffn_body(*refs, final=final, precision=precision)


def _ffn_kmeans_kernel(pt_ref, x_ref, g_ref, win_ref, wout_ref, gf_ref, ck_ref, o_ref, km_ref,
                       act_ref, kbuf, sem, *, final, precision, n_pages, first_page):
    s = pl.program_id(0)
    cur = s % 2

    def copies(seq, buf):
        return [pltpu.make_async_copy(ck_ref.at[pt_ref[seq * n_pages + first_page + p]],
                                      kbuf.at[buf, p], sem.at[buf])
                for p in range(KMEANS_PAGES)]

    @pl.when(s == 0)
    def _():
        for cp in copies(s, cur):
            cp.start()

    @pl.when(s + 1 < pl.num_programs(0))
    def _():
        for cp in copies(s + 1, 1 - cur):
            cp.start()

    for cp in copies(s, cur):
        cp.wait()

    means = []
    for blk in range(KMEANS_PAGES // PAGES_PER_BLOCK):
        t = sum(jnp.sum(kbuf[cur, blk * PAGES_PER_BLOCK + h], axis=0)
                for h in range(PAGES_PER_BLOCK))
        t = sum(t[par * N_KV_HEADS:(par + 1) * N_KV_HEADS] for par in range(ROW_PARITIES))
        means.append(jnp.concatenate([t[kvh:kvh + 1, :] for kvh in range(N_KV_HEADS)], axis=1))
    km_ref[0] = jnp.concatenate(means, axis=0) * (1.0 / MOBA_BLOCK)

    _ffn_body(x_ref, g_ref, win_ref, wout_ref, gf_ref, o_ref, act_ref, final=final,
              precision=precision)


def _ffn_attn_kernel(pt_ref, sel_ref, x_ref, g_ref, win_ref, wout_ref, gf_ref, q_ref, kn_ref,
                     vn_ref, ck_ref, cv_ref, o_ref, ao_ref, act_ref, kbuf, vbuf, sem, *,
                     final, precision, n_pages):
    _attn_sample_body(pt_ref, sel_ref, q_ref, kn_ref, vn_ref, ck_ref, cv_ref, ao_ref, kbuf, vbuf,
                      sem, n_pages=n_pages)
    _ffn_body(x_ref, g_ref, win_ref, wout_ref, gf_ref, o_ref, act_ref, final=final,
              precision=precision)


def _ffn(x, g, w_in, w_out, g_final, layer, *, final, tile, kmeans=None, attn=None):
    n, d = x.shape
    precision = HIGHEST if w_in.dtype == F32 else None
    in_specs = [
        pl.BlockSpec((tile, d), lambda i, *_: (i, 0)),
        _resident((1, d)),
        _layer(layer, (d, 2 * D_FF)),
        _layer(layer, (D_FF, d)),
        _resident((1, d)),
    ]
    out_spec = pl.BlockSpec((tile, d), lambda i, *_: (i, 0))
    out_shape = jax.ShapeDtypeStruct((n, d), F32)
    act = pltpu.VMEM((tile, D_FF), w_in.dtype)
    any_space = pl.BlockSpec(memory_space=pl.ANY)
    if attn is not None:
        pt_flat, sel_flat, q3, k_new3, v_new3, ck, cv, n_pages = attn
        n_seq = n // tile
        assert q3.shape[0] == n_seq
        n_slots = MOBA_TOPK * PAGES_PER_BLOCK
        row = lambda width: pl.BlockSpec((1, 1, width), lambda i, *_: (i, 0, 0))
        gathered = pltpu.VMEM((2, N_HEADS, n_slots, PAGE_SIZE, HEAD_DIM), F32)
        return pl.pallas_call(
            functools.partial(_ffn_attn_kernel, final=final, precision=precision, n_pages=n_pages),
            name="ffn_attn",
            grid_spec=pltpu.PrefetchScalarGridSpec(
                num_scalar_prefetch=2,
                grid=(n_seq,),
                in_specs=in_specs + [row(D_MODEL), row(KV_DIM), row(KV_DIM), any_space, any_space],
                out_specs=[out_spec, row(D_MODEL)],
                scratch_shapes=[act, gathered, gathered, pltpu.SemaphoreType.DMA((2, 2))],
            ),
            out_shape=[out_shape, jax.ShapeDtypeStruct((n_seq, 1, D_MODEL), F32)],
            compiler_params=_params("arbitrary"),
        )(pt_flat, sel_flat, x, g, w_in, w_out, g_final, q3, k_new3, v_new3, ck, cv)
    if kmeans is None:
        return pl.pallas_call(
            functools.partial(_ffn_kernel, final=final, precision=precision),
            name="ffn",
            grid=(n // tile,),
            in_specs=in_specs,
            out_specs=out_spec,
            out_shape=out_shape,
            scratch_shapes=[act],
            compiler_params=_params("arbitrary"),
        )(x, g, w_in, w_out, g_final)

    pt_flat, cache, n_pages, first_page = kmeans
    n_seq = n // tile
    assert pt_flat.shape[0] == n_seq * n_pages
    blocks = KMEANS_PAGES // PAGES_PER_BLOCK
    return pl.pallas_call(
        functools.partial(_ffn_kmeans_kernel, final=final, precision=precision, n_pages=n_pages,
                          first_page=first_page),
        name="ffn_kmeans",
        grid_spec=pltpu.PrefetchScalarGridSpec(
            num_scalar_prefetch=1,
            grid=(n_seq,),
            in_specs=in_specs + [any_space],
            out_specs=[out_spec, pl.BlockSpec((1, blocks, KV_DIM), lambda i, *_: (i, 0, 0))],
            scratch_shapes=[act, pltpu.VMEM((2, KMEANS_PAGES) + PAGE_VIEW, F32),
                            pltpu.SemaphoreType.DMA((2,))],
        ),
        out_shape=[out_shape, jax.ShapeDtypeStruct((n_seq, blocks, KV_DIM), F32)],
        compiler_params=_params("arbitrary"),
    )(pt_flat, x, g, w_in, w_out, g_final, cache)


def _kv_prompt_kernel(x_ref, g_ref, w_ref, c_ref, slo_ref, shi_ref,
                      k_ref, v_ref, kb_ref, vt_ref, km_ref):
    xn = _rms(x_ref[0], g_ref[...]).astype(BF16)
    kv = jnp.dot(xn, w_ref[...], preferred_element_type=F32)
    tile4 = lambda r: jnp.concatenate([r[...]] * N_KV_HEADS, axis=1)
    k = _rope(kv[:, :KV_DIM], tile4(c_ref), tile4(slo_ref), tile4(shi_ref))
    v = kv[:, KV_DIM:]
    k_ref[0] = k
    v_ref[0] = v
    kb_ref[0] = k.astype(BF16)
    for bb in range(KV_TILE // MOBA_BLOCK):
        rows = slice(bb * MOBA_BLOCK, (bb + 1) * MOBA_BLOCK)
        vt_ref[0, bb] = v[rows, :].T.astype(BF16)
        km_ref[0, bb] = jnp.sum(k[rows, :], axis=0, keepdims=True) * (1.0 / MOBA_BLOCK)


def _kv_prompt(x, g, w_kv, c, s_lo, s_hi):
    bsz, t, d = x.shape
    tt = KV_TILE
    nbt = tt // MOBA_BLOCK
    nb = t // MOBA_BLOCK
    tab = pl.BlockSpec((tt, HEAD_DIM), lambda b, i: (i, 0))
    row_spec = pl.BlockSpec((1, tt, KV_DIM), lambda b, i: (b, i, 0))
    return pl.pallas_call(
        _kv_prompt_kernel,
        name="kv_prompt",
        grid=(bsz, t // tt),
        in_specs=[
            pl.BlockSpec((1, tt, d), lambda b, i: (b, i, 0)),
            _resident((1, d)),
            _resident((d, 2 * KV_DIM)),
            tab, tab, tab,
        ],
        out_specs=[
            row_spec, row_spec, row_spec,
            pl.BlockSpec((1, nbt, KV_DIM, MOBA_BLOCK), lambda b, i: (b, i, 0, 0)),
            pl.BlockSpec((1, nbt, 1, KV_DIM), lambda b, i: (b, i, 0, 0)),
        ],
        out_shape=[
            jax.ShapeDtypeStruct((bsz, t, KV_DIM), F32),
            jax.ShapeDtypeStruct((bsz, t, KV_DIM), F32),
            jax.ShapeDtypeStruct((bsz, t, KV_DIM), BF16),
            jax.ShapeDtypeStruct((bsz, nb, KV_DIM, MOBA_BLOCK), BF16),
            jax.ShapeDtypeStruct((bsz, nb, 1, KV_DIM), F32),
        ],
        compiler_params=_params("arbitrary", "arbitrary"),
    )(x, g, w_kv, c, s_lo, s_hi)


def _select_bias(gate, n_allowed, n_valid):
    nb, nq = gate.shape
    rowi = lax.broadcasted_iota(jnp.int32, (nb, nq), 0).astype(F32)
    g = jnp.where(rowi < n_allowed, gate, NEG)
    bias = jnp.where(rowi == n_allowed, 0.0, NEG)
    for r in range(MOBA_TOPK):
        mx = jnp.max(g, axis=0, keepdims=True)
        idx = jnp.min(jnp.where(g == mx, rowi, float(nb)), axis=0, keepdims=True)
        ok = (n_valid > r).astype(F32)
        idx = idx * ok - (1.0 - ok)
        pick = rowi == idx
        bias = jnp.where(pick, 0.0, bias)
        g = jnp.where(pick, NEG, g)
    return bias


def _attn_prompt_kernel(x_ref, g_ref, wq_ref, wo_ref, c_ref, slo_ref, shi_ref, k_ref, vt_ref,
                        km_ref, o_ref, q2s_ref, s0_ref, s1_ref, m_ref, acc_ref, attn_ref):
    tq = MOBA_BLOCK
    nq = ATTN_CHAIN_HEADS * tq
    i = pl.program_id(1)
    x = x_ref[0]
    xn = _rms(x, g_ref[...]).astype(BF16)
    q = jnp.dot(xn, wq_ref[...], preferred_element_type=F32)
    c, s_lo, s_hi = c_ref[...], slo_ref[...], shi_ref[...]
    own = i.astype(F32)

    key_pos = lax.broadcasted_iota(jnp.int32, (tq, nq), 0)
    qry_pos = lax.broadcasted_iota(jnp.int32, (tq, nq), 1) % tq
    causal = key_pos <= qry_pos
    n_chains = N_HEADS // ATTN_CHAIN_HEADS
    chain_heads = [range(ch * ATTN_CHAIN_HEADS, (ch + 1) * ATTN_CHAIN_HEADS)
                   for ch in range(n_chains)]
    kv_lanes = [slice((hs[0] // GROUP) * HEAD_DIM, (hs[0] // GROUP + 1) * HEAD_DIM)
                for hs in chain_heads]

    ones_rows = jnp.where(lax.broadcasted_iota(jnp.int32, (BF16_ROWS, tq), 0) == 0, 1.0, 0.0
                          ).astype(BF16)

    def vt_ones(j, lanes):
        return jnp.concatenate([vt_ref[0, j, lanes, :], ones_rows], axis=0)

    for kvh, lanes in enumerate(kv_lanes):
        q2 = jnp.concatenate(
            [_rope(q[:, h * HEAD_DIM:(h + 1) * HEAD_DIM], c, s_lo, s_hi)
             for h in chain_heads[kvh]], axis=0)
        gate = lax.dot_general(km_ref[0, :, lanes], q2, NT_DIMS, precision=HIGHEST,
                               preferred_element_type=F32)
        bias = _select_bias(gate, own, own)
        bias_t = jnp.concatenate(
            [bias, jnp.zeros((HEAD_DIM - bias.shape[0], nq), F32)], axis=0).T
        q2s = (q2 * (SM_SCALE * LOG2E)).astype(BF16)
        q2s_ref[kvh] = jnp.concatenate([q2s, bias_t.astype(BF16)], axis=1)
        m_ref[kvh] = jnp.full((1, nq), NEG, F32)
        acc_ref[kvh] = jnp.zeros(acc_ref.shape[1:], F32)

    block_lane = lax.broadcasted_iota(jnp.int32, (1, HEAD_DIM), 1)

    def qk(j, s_out):
        one_hot = jnp.broadcast_to(jnp.where(block_lane == j, 1.0, 0.0).astype(BF16),
                                   (tq, HEAD_DIM))
        for kvh, lanes in enumerate(kv_lanes):
            k_aug = jnp.concatenate([k_ref[0, j, :, lanes], one_hot], axis=1)
            s_out[kvh] = lax.dot_general(k_aug, q2s_ref[kvh], NT_DIMS,
                                         preferred_element_type=F32)

    def block(j, s_cur, s_nxt):
        if s_nxt is not None:
            qk(j + 1, s_nxt)
        for kvh, lanes in enumerate(kv_lanes):
            s = s_cur[kvh]
            if s_nxt is None:
                s = jnp.where(causal, s, NEG)
            m = m_ref[kvh]
            m_new = jnp.maximum(m, jnp.max(s, axis=0, keepdims=True))
            alpha = jnp.exp2(m - m_new)
            p = jnp.exp2(s - m_new)
            m_ref[kvh] = m_new
            acc_ref[kvh] = alpha * acc_ref[kvh] + jnp.dot(
                vt_ones(j, lanes), p.astype(BF16), preferred_element_type=F32)

    qk(0, s0_ref)
    bufs = (s0_ref, s1_ref)

    def run(first, count):
        for k in range(count):
            block(first + k, bufs[k % 2], bufs[(k + 1) % 2])

    def long_trip(t, carry):
        run(ATTN_UNROLL * t, ATTN_UNROLL)
        return carry

    n_long = i // ATTN_UNROLL
    lax.fori_loop(0, n_long, long_trip, 0)
    done = n_long * ATTN_UNROLL

    def pair_trip(t, carry):
        run(done + 2 * t, 2)
        return carry

    lax.fori_loop(0, (i - done) // 2, pair_trip, 0)

    @pl.when(i % 2 == 1)
    def _():
        run(i - 1, 1)
        block(i, s1_ref, None)

    @pl.when(i % 2 == 0)
    def _():
        block(i, s0_ref, None)

    for kvh in range(n_chains):
        o = (acc_ref[kvh, :HEAD_DIM, :] / acc_ref[kvh, HEAD_DIM:HEAD_DIM + 1, :]).T
        for j, h in enumerate(chain_heads[kvh]):
            attn_ref[:, h * HEAD_DIM:(h + 1) * HEAD_DIM] = o[j * tq:(j + 1) * tq, :].astype(BF16)

    o_ref[0] = x + jnp.dot(attn_ref[...], wo_ref[...], preferred_element_type=F32)


def _attn_prompt(x, g, w_q, w_o, c, s_lo, s_hi, kb, vt, km):
    bsz, t, d = x.shape
    tq = MOBA_BLOCK
    nb = t // tq
    n_chains, nq = N_HEADS // ATTN_CHAIN_HEADS, ATTN_CHAIN_HEADS * tq
    tab = pl.BlockSpec((tq, HEAD_DIM), lambda b, i: (i, 0))
    per_seq = lambda shape: pl.BlockSpec((1,) + shape, lambda b, i: (b,) + (0,) * len(shape),
                                         pipeline_mode=pl.Buffered(1))
    return pl.pallas_call(
        _attn_prompt_kernel,
        name="attn_prompt",
        grid=(bsz, nb),
        in_specs=[
            pl.BlockSpec((1, tq, d), lambda b, i: (b, i, 0)),
            _resident((1, d)),
            _resident((d, d)),
            _resident((d, d)),
            tab, tab, tab,
            per_seq((nb, tq, KV_DIM)),
            per_seq((nb, KV_DIM, tq)),
            per_seq((nb, KV_DIM)),
        ],
        out_specs=pl.BlockSpec((1, tq, d), lambda b, i: (b, i, 0)),
        out_shape=jax.ShapeDtypeStruct((bsz, t, d), F32),
        scratch_shapes=[
            pltpu.VMEM((n_chains, nq, 2 * HEAD_DIM), BF16),
            pltpu.VMEM((n_chains, tq, nq), F32),
            pltpu.VMEM((n_chains, tq, nq), F32),
            pltpu.VMEM((n_chains, 1, nq), F32),
            pltpu.VMEM((n_chains, HEAD_DIM + BF16_ROWS, nq), F32),
            pltpu.VMEM((tq, d), BF16),
        ],
        compiler_params=_params("arbitrary", "arbitrary"),
    )(x, g, w_q, w_o, c, s_lo, s_hi, kb, vt, km)


def _dot_hi(a, b):
    return jnp.dot(a, b, preferred_element_type=F32, precision=HIGHEST)


def _rg_sample_kernel(x_ref, cs_ref, h0_ref, g_ref, win_ref, cw_ref, cb_ref, wa_ref, wi_ref,
                      ba_ref, bi_ref, lam_ref, wout_ref, xo_ref, cso_ref, ho_ref):
    x = x_ref[...]
    proj = _dot_hi(_rms(x, g_ref[...]), win_ref[...])
    gate, u = proj[:, :LRU_WIDTH], proj[:, LRU_WIDTH:]
    uc = cb_ref[...] + cs_ref[0] * cw_ref[0:1, :]
    for j in range(1, CONV_WIDTH - 1):
        uc = uc + cs_ref[j] * cw_ref[j:j + 1, :]
    uc = uc + u * cw_ref[CONV_WIDTH - 1:CONV_WIDTH, :]
    for j in range(CONV_WIDTH - 2):
        cso_ref[j] = cs_ref[j + 1]
    cso_ref[CONV_WIDTH - 2] = u

    sp = _softplus(-lam_ref[...])
    h0 = h0_ref[...]
    hs = []
    for n in range(N_GATE_BLOCKS):
        sl = slice(n * GATE_BLOCK, (n + 1) * GATE_BLOCK)
        ucn = uc[:, sl]
        a, b = _lru_coeffs(_dot_hi(ucn, wa_ref[n]) + ba_ref[:, sl],
                           _dot_hi(ucn, wi_ref[n]) + bi_ref[:, sl], ucn, sp[:, sl])
        hs.append(b + a * h0[:, sl])
    h = jnp.concatenate(hs, axis=1)
    ho_ref[...] = h
    xo_ref[...] = x + _dot_hi(h * jax.nn.gelu(gate), wout_ref[...])


def _rg_sample(x, cs, h0, g, w_in, conv_w, conv_b, wa, wi, ba, bi, lam, w_out, layer):
    n, d = x.shape
    vec = _resident((1, LRU_WIDTH))
    gates = _layer(layer, (N_GATE_BLOCKS, GATE_BLOCK, GATE_BLOCK))
    return pl.pallas_call(
        _rg_sample_kernel,
        name="rg_sample",
        grid=(1,),
        in_specs=[
            _resident((n, d)), _resident((CONV_WIDTH - 1, n, LRU_WIDTH)), _resident((n, LRU_WIDTH)),
            _resident((1, d)), _layer(layer, (d, 2 * LRU_WIDTH)), _resident((CONV_WIDTH, LRU_WIDTH)),
            vec, gates, gates, vec, vec, vec, _layer(layer, (LRU_WIDTH, d)),
        ],
        out_specs=[_whole((n, d)), _whole((CONV_WIDTH - 1, n, LRU_WIDTH)), _whole((n, LRU_WIDTH))],
        out_shape=[
            jax.ShapeDtypeStruct((n, d), F32),
            jax.ShapeDtypeStruct((CONV_WIDTH - 1, n, LRU_WIDTH), F32),
            jax.ShapeDtypeStruct((n, LRU_WIDTH), F32),
        ],
        compiler_params=_params("arbitrary"),
    )(x, cs, h0, g, w_in, conv_w, conv_b, wa, wi, ba, bi, lam, w_out)


def _proj_sample_kernel(x_ref, g_ref, w_ref, c_ref, slo_ref, shi_ref, o_ref, *, n_rope_heads):
    y = _dot_hi(_rms(x_ref[...], g_ref[...]), w_ref[...])
    if n_rope_heads:
        tile = lambda r: jnp.concatenate([r[...]] * n_rope_heads, axis=1)
        width = n_rope_heads * HEAD_DIM
        rot = _rope(y[:, :width], tile(c_ref), tile(slo_ref), tile(shi_ref))
        y = jnp.concatenate([rot, y[:, width:]], axis=1) if width < y.shape[1] else rot
    o_ref[...] = y


def _proj_sample(x, g, w, layer, c, s_lo, s_hi, n_rope_heads):
    n, d = x.shape
    width = w.shape[2]
    tab = _resident((1, HEAD_DIM))
    return pl.pallas_call(
        functools.partial(_proj_sample_kernel, n_rope_heads=n_rope_heads),
        name="proj_sample",
        grid=(1,),
        in_specs=[_resident((n, d)), _resident((1, d)), _layer(layer, (d, width)), tab, tab, tab],
        out_specs=_whole((n, width)),
        out_shape=jax.ShapeDtypeStruct((n, width), F32),
        compiler_params=_params("arbitrary"),
    )(x, g, w, c, s_lo, s_hi)


def _gate_sample_kernel(q_ref, km_ref, sel_ref):
    km = km_ref[...]
    n_seq, nb, _ = km.shape
    blk = lax.broadcasted_iota(jnp.int32, (n_seq, nb, 1), 1).astype(F32)
    lane = lax.broadcasted_iota(jnp.int32, (n_seq, 1, HEAD_DIM), 2)
    for h in range(N_HEADS):
        kvh = h // GROUP
        qh = q_ref[:, :, h * HEAD_DIM:(h + 1) * HEAD_DIM]
        g = jnp.sum(km[:, :, kvh * HEAD_DIM:(kvh + 1) * HEAD_DIM] * qh, axis=-1, keepdims=True)
        out = jnp.zeros((n_seq, 1, HEAD_DIM), F32)
        for r in range(MOBA_TOPK):
            mx = jnp.max(g, axis=1, keepdims=True)
            idx = jnp.min(jnp.where(g == mx, blk, float(nb)), axis=1, keepdims=True)
            g = jnp.where(blk == idx, NEG, g)
            out = jnp.where(lane == r, idx, out)
        sel_ref[h] = out.astype(jnp.int32)


def _gate_sample(q3, km):
    n_seq = q3.shape[0]
    return pl.pallas_call(
        _gate_sample_kernel,
        name="gate_sample",
        out_shape=jax.ShapeDtypeStruct((N_HEADS, n_seq, 1, HEAD_DIM), jnp.int32),
        compiler_params=pltpu.CompilerParams(vmem_limit_bytes=VMEM_LIMIT),
    )(q3, km)


def _attn_sample_body(pt_ref, sel_ref, q_ref, kn_ref, vn_ref, ck_ref, cv_ref, o_ref,
                      kbuf, vbuf, sem, *, n_pages):
    b = pl.program_id(0)
    n_slots = MOBA_TOPK * PAGES_PER_BLOCK
    rows = PAGE_SIZE // ROW_PARITIES

    def copies(seq, buf):
        out = []
        for h in range(N_HEADS):
            kvh = h // GROUP
            for r in range(MOBA_TOPK):
                blk = sel_ref[(seq * N_HEADS + h) * MOBA_TOPK + r]
                for half in range(PAGES_PER_BLOCK):
                    page = pt_ref[seq * n_pages + PAGES_PER_BLOCK * blk + half]
                    slot = r * PAGES_PER_BLOCK + half
                    for par in range(ROW_PARITIES):
                        sub = par * N_KV_HEADS + kvh
                        dst = (buf, h, slot, pl.ds(par * rows, rows))
                        out.append(pltpu.make_async_copy(ck_ref.at[page, :, sub], kbuf.at[dst], sem.at[0, buf]))
                        out.append(pltpu.make_async_copy(cv_ref.at[page, :, sub], vbuf.at[dst], sem.at[1, buf]))
        return out

    cur = b % 2

    def start_all(seq, buf):
        for k, cp in enumerate(copies(seq, buf)):
            cp.start(priority=(k // 2) % 2)

    @pl.when(b == 0)
    def _():
        start_all(b, cur)

    @pl.when(b + 1 < pl.num_programs(0))
    def _():
        start_all(b + 1, 1 - cur)

    for cp in copies(b, cur):
        cp.wait()

    for h in range(N_HEADS):
        kvh = h // GROUP
        qh = q_ref[0, :, h * HEAD_DIM:(h + 1) * HEAD_DIM] * SM_SCALE
        kn = kn_ref[0, :, kvh * HEAD_DIM:(kvh + 1) * HEAD_DIM]
        vn = vn_ref[0, :, kvh * HEAD_DIM:(kvh + 1) * HEAD_DIM]
        kk = kbuf[cur, h].reshape(n_slots * PAGE_SIZE, HEAD_DIM)
        vv = vbuf[cur, h].reshape(n_slots * PAGE_SIZE, HEAD_DIM)
        s = jnp.sum(kk * qh, axis=-1, keepdims=True)
        s_new = jnp.sum(kn * qh, axis=-1, keepdims=True)
        m = jnp.maximum(jnp.max(s, axis=0, keepdims=True), s_new)
        p = jnp.exp(s - m)
        p_new = jnp.exp(s_new - m)
        l = jnp.sum(p, axis=0, keepdims=True) + p_new
        acc = jnp.sum(p * vv, axis=0, keepdims=True) + p_new * vn
        o_ref[0, :, h * HEAD_DIM:(h + 1) * HEAD_DIM] = acc / l


def _oproj_sample_kernel(x_ref, a_ref, w_ref, o_ref):
    o_ref[...] = x_ref[...] + _dot_hi(a_ref[...], w_ref[...])


def _oproj_sample(x, attn, w_o, layer):
    n, d = x.shape
    return pl.pallas_call(
        _oproj_sample_kernel,
        name="oproj_sample",
        grid=(1,),
        in_specs=[_resident((n, d)), _resident((n, d)), _layer(layer, (d, d))],
        out_specs=_whole((n, d)),
        out_shape=jax.ShapeDtypeStruct((n, d), F32),
        compiler_params=_params("arbitrary"),
    )(x, attn, w_o)


def _step(x_prompt, x_sample3, state_conv, state_h, ck, cv, pt_flat, n_pages, w):
    bsz, t, d = x_prompt.shape
    n_seq = x_sample3.shape[0]
    row = lambda a: a.reshape(1, -1)
    bf = lambda a: a.astype(BF16)
    n_rg = w["w_rg_in"].shape[0]
    depth = w["g_mix"].shape[0]
    nb = t // MOBA_BLOCK
    assert n_pages == n_rg * KMEANS_PAGES and (bsz * t) // FFN_TILE == n_seq

    c_p, slo_p, shi_p = _rope_tables(jnp.arange(t, dtype=jnp.int32))
    c_s, slo_s, shi_s = _rope_tables(jnp.full((1,), n_pages * PAGE_SIZE, jnp.int32))
    w_ffn_in_bf, w_ffn_out_bf = bf(w["w_ffn_in"]), bf(w["w_ffn_out"])

    def ffn_prompt(x, l, **side):
        out = _ffn(x.reshape(bsz * t, d), row(w["g_ffn"][l]), w_ffn_in_bf, w_ffn_out_bf,
                   row(w["g_final"]), l, final=(l == depth - 1), tile=FFN_TILE, **side)
        return out[0].reshape(bsz, t, d), out[1]

    def ffn_sample(x, l):
        w_in, w_out = (w["w_ffn_in"], w["w_ffn_out"]) if l == 0 else (w_ffn_in_bf, w_ffn_out_bf)
        return _ffn(x, row(w["g_ffn"][l]), w_in, w_out, row(w["g_final"]), l,
                    final=(l == depth - 1), tile=n_seq)

    xp, xs = x_prompt, x_sample3.reshape(n_seq, d)
    p_convs, p_hs, s_convs, s_hs, kmeans = [], [], [], [], []
    for l in range(n_rg):
        wg = jnp.concatenate([w["rg_gate_a_w"][l], w["rg_gate_i_w"][l]], axis=-1)
        xp, conv, h = _rg_prompt(
            xp, row(w["g_mix"][l]), bf(w["w_rg_in"][l]), w["rg_conv_w"][l], row(w["rg_conv_b"][l]),
            bf(wg), row(w["rg_gate_a_b"][l]), row(w["rg_gate_i_b"][l]), row(w["rg_lambda"][l]),
            bf(w["w_rg_out"][l]))
        p_convs.append(conv)
        p_hs.append(h[:, 0, :])
        xp, km_l = ffn_prompt(xp, l, kmeans=(pt_flat, ck, n_pages, l * KMEANS_PAGES))
        kmeans.append(km_l)

        xs, conv, h = _rg_sample(
            xs, jnp.swapaxes(state_conv[l], 0, 1), state_h[l], row(w["g_mix"][l]), w["w_rg_in"],
            w["rg_conv_w"][l], row(w["rg_conv_b"][l]), w["rg_gate_a_w"], w["rg_gate_i_w"],
            row(w["rg_gate_a_b"][l]), row(w["rg_gate_i_b"][l]), row(w["rg_lambda"][l]), w["w_rg_out"], l)
        s_convs.append(jnp.swapaxes(conv, 0, 1))
        s_hs.append(h)
        xs = ffn_sample(xs, l)

    k, v, kb, vt, km_p = _kv_prompt(xp, row(w["g_kv"]), bf(w["w_kv"]), c_p, slo_p, shi_p)
    kb = kb.reshape(bsz, nb, MOBA_BLOCK, KV_DIM)
    km_p = km_p.reshape(bsz, nb, KV_DIM)
    kv_s = _proj_sample(xs, row(w["g_kv"]), w["w_kv"][None], 0, c_s, slo_s, shi_s, N_KV_HEADS)
    k_new, v_new = kv_s[:, :KV_DIM], kv_s[:, KV_DIM:]
    km_s = jnp.concatenate(kmeans, axis=1)

    for a in range(depth - n_rg):
        l = n_rg + a
        xp = _attn_prompt(xp, row(w["g_mix"][l]), bf(w["w_q"][a]), bf(w["w_o"][a]), c_p, slo_p, shi_p,
                          kb, vt, km_p)
        q3 = _proj_sample(xs, row(w["g_mix"][l]), w["w_q"], a, c_s, slo_s, shi_s,
                          N_HEADS).reshape(n_seq, 1, d)
        sel = _gate_sample(q3, km_s)[:, :, 0, :MOBA_TOPK]
        sel_flat = jnp.transpose(sel, (1, 0, 2)).reshape(-1)
        xp, attn_s = ffn_prompt(xp, l, attn=(pt_flat, sel_flat, q3, k_new.reshape(n_seq, 1, KV_DIM),
                                             v_new.reshape(n_seq, 1, KV_DIM), ck, cv, n_pages))
        xs = _oproj_sample(xs, attn_s.reshape(n_seq, d), w["w_o"], a)
        xs = ffn_sample(xs, l)

    shape_p = (bsz, t, N_KV_HEADS, HEAD_DIM)
    shape_s = (n_seq, 1, N_KV_HEADS, HEAD_DIM)
    return (xp, xs.reshape(n_seq, 1, d), jnp.stack(p_convs), jnp.stack(p_hs),
            k.reshape(shape_p), v.reshape(shape_p), jnp.stack(s_convs), jnp.stack(s_hs),
            k_new.reshape(shape_s), v_new.reshape(shape_s))


def kernel(x_prompt, x_sample, cache_k, cache_v, state_conv, state_h, page_table,
           g_mix, g_ffn, w_rg_in, rg_conv_w, rg_conv_b, rg_gate_a_w, rg_gate_a_b,
           rg_gate_i_w, rg_gate_i_b, rg_lambda, w_rg_out, g_kv, w_kv, w_q, w_o,
           w_ffn_in, w_ffn_out, g_final):
    assert x_sample.shape[1] == 1 and x_prompt.shape[1] % MOBA_BLOCK == 0
    assert MOBA_TOPK <= page_table.shape[1] // PAGES_PER_BLOCK
    w = dict(g_mix=g_mix, g_ffn=g_ffn, w_rg_in=w_rg_in, rg_conv_w=rg_conv_w, rg_conv_b=rg_conv_b,
             rg_gate_a_w=rg_gate_a_w, rg_gate_a_b=rg_gate_a_b, rg_gate_i_w=rg_gate_i_w,
             rg_gate_i_b=rg_gate_i_b, rg_lambda=rg_lambda, w_rg_out=w_rg_out, g_kv=g_kv, w_kv=w_kv,
             w_q=w_q, w_o=w_o, w_ffn_in=w_ffn_in, w_ffn_out=w_ffn_out, g_final=g_final)
    ck = cache_k.reshape((cache_k.shape[0],) + PAGE_VIEW)
    cv = cache_v.reshape((cache_v.shape[0],) + PAGE_VIEW)
    return _step(x_prompt, x_sample, state_conv, state_h, ck, cv, page_table.reshape(-1),
                 page_table.shape[1], w)
```

```python
import functools
import math

import jax
import jax.numpy as jnp
from jax import lax
from jax.experimental import pallas as pl
from jax.experimental.pallas import tpu as pltpu

D_MODEL = 1024
LRU_WIDTH = D_MODEL
N_GATE_BLOCKS = 8
GATE_BLOCK = LRU_WIDTH // N_GATE_BLOCKS
CONV_WIDTH = 4
LRU_C = 8.0
N_HEADS = 8
HEAD_DIM = D_MODEL // N_HEADS
N_KV_HEADS = 4
KV_DIM = N_KV_HEADS * HEAD_DIM
GROUP = N_HEADS // N_KV_HEADS
ROT_DIM = HEAD_DIM // 4
ROPE_THETA = 500000.0
MOBA_BLOCK = 256
MOBA_TOPK = 3
PAGE_SIZE = 128
PAGES_PER_BLOCK = MOBA_BLOCK // PAGE_SIZE
D_FF = -(-8 * D_MODEL // (3 * 256)) * 256
EPS = 1e-6
NEG = -1e30
SM_SCALE = 1.0 / math.sqrt(HEAD_DIM)
LOG2E = math.log2(math.e)

F32 = jnp.float32
BF16 = jnp.bfloat16
HIGHEST = lax.Precision.HIGHEST

SUBLANES = 8
BF16_ROWS = 16
FF_CHUNK = 256
RG_TILE = 512
FFN_TILE = 512
KV_TILE = 512
KMEANS_PAGES = 32
ATTN_UNROLL = 8
ATTN_CHAIN_HEADS = GROUP
ROW_PARITIES = SUBLANES // N_KV_HEADS
PAGE_VIEW = (PAGE_SIZE // ROW_PARITIES, SUBLANES, HEAD_DIM)
VMEM_LIMIT = 56 * 1024 * 1024

NT_DIMS = (((1,), (1,)), ((), ()))


def _params(*sem):
    return pltpu.CompilerParams(dimension_semantics=sem, vmem_limit_bytes=VMEM_LIMIT)


def _resident(shape):
    zeros = (0,) * len(shape)
    return pl.BlockSpec(shape, lambda *_: zeros, pipeline_mode=pl.Buffered(1))


def _whole(shape):
    zeros = (0,) * len(shape)
    return pl.BlockSpec(shape, lambda *_: zeros)


def _layer(layer, shape):
    zeros = (0,) * len(shape)
    return pl.BlockSpec((None,) + shape, lambda *_: (layer,) + zeros, pipeline_mode=pl.Buffered(1))


def _rms(x, g):
    ms = jnp.mean(x * x, axis=-1, keepdims=True)
    return x * lax.rsqrt(ms + EPS) * g


def _softplus(z):
    return jnp.maximum(z, 0.0) + jnp.log1p(jnp.exp(-jnp.abs(z)))


def _rope(x, c, s_lo, s_hi):
    n = x.shape[-1]
    half = ROT_DIM // 2
    return x * c + pltpu.roll(x, half, 1) * s_hi + pltpu.roll(x, n - half, 1) * s_lo


def _rope_tables(pos):
    half = ROT_DIM // 2
    inv = ROPE_THETA ** (-jnp.arange(half, dtype=F32) / half)
    ang = pos.astype(F32)[:, None] * inv[None, :]
    cos, sin = jnp.cos(ang), jnp.sin(ang)
    t = pos.shape[0]
    rest = HEAD_DIM - ROT_DIM
    c = jnp.concatenate([cos, cos, jnp.ones((t, rest), F32)], axis=1)
    s_hi = jnp.concatenate([jnp.zeros((t, half), F32), sin, jnp.zeros((t, rest), F32)], axis=1)
    s_lo = jnp.concatenate([-sin, jnp.zeros((t, half + rest), F32)], axis=1)
    return c, s_lo, s_hi


def _lru_coeffs(z_a, z_i, uc, sp):
    r = jax.nn.sigmoid(z_a)
    i = jax.nn.sigmoid(z_i)
    log_a = (-LRU_C * r) * sp
    a = jnp.exp(log_a)
    b = jnp.sqrt(-jnp.tanh(log_a) * (a * a + 1.0)) * (i * uc)
    return a, b


def _rg_prompt_kernel(x_ref, g_ref, win_ref, cw_ref, cb_ref, wg_ref, ba_ref, bi_ref, lam_ref,
                      wout_ref, xo_ref, conv_ref, hl_ref, ext_ref, a_ref, b_ref, gg_ref, hc_ref):
    tt = RG_TILE
    t = pl.program_id(1)

    @pl.when(t == 0)
    def _():
        ext_ref[0:SUBLANES, :] = jnp.zeros((SUBLANES, LRU_WIDTH), F32)
        hc_ref[...] = jnp.zeros_like(hc_ref)

    x = x_ref[0]
    xn = _rms(x, g_ref[...]).astype(BF16)
    proj = jnp.dot(xn, win_ref[...], preferred_element_type=F32)
    gg_ref[...] = jax.nn.gelu(proj[:, :LRU_WIDTH])
    ext_ref[SUBLANES:SUBLANES + tt, :] = proj[:, LRU_WIDTH:]

    base = SUBLANES - (CONV_WIDTH - 1)
    ext = ext_ref[...]

    def lagged(j):
        lag = CONV_WIDTH - 1 - j
        rolled = pltpu.roll(ext, lag, 0) if lag else ext
        return rolled[SUBLANES:SUBLANES + tt, :]

    uc = cb_ref[...] + lagged(0) * cw_ref[0:1, :]
    for j in range(1, CONV_WIDTH):
        uc = uc + lagged(j) * cw_ref[j:j + 1, :]
    conv_ref[0] = ext_ref[tt + base:tt + SUBLANES, :]
    ext_ref[0:SUBLANES, :] = ext_ref[tt:tt + SUBLANES, :]

    sp = _softplus(-lam_ref[...])
    for n in range(N_GATE_BLOCKS):
        sl = slice(n * GATE_BLOCK, (n + 1) * GATE_BLOCK)
        ucn = uc[:, sl]
        z = jnp.dot(ucn.astype(BF16), wg_ref[n], preferred_element_type=F32)
        a, b = _lru_coeffs(z[:, :GATE_BLOCK] + ba_ref[:, sl], z[:, GATE_BLOCK:] + bi_ref[:, sl],
                           ucn, sp[:, sl])
        a_ref[:, sl] = a
        b_ref[:, sl] = b

    row = lax.broadcasted_iota(jnp.int32, (SUBLANES, LRU_WIDTH), 0)

    def group(gi, h_prev):
        r0 = pl.multiple_of(gi * SUBLANES, SUBLANES)
        a8 = a_ref[pl.ds(r0, SUBLANES), :]
        b8 = b_ref[pl.ds(r0, SUBLANES), :]
        s = 1
        while s < SUBLANES:
            keep = row >= s
            b8 = jnp.where(keep, a8 * pltpu.roll(b8, s, 0) + b8, b8)
            a8 = jnp.where(keep, a8 * pltpu.roll(a8, s, 0), a8)
            s *= 2
        h = a8 * h_prev + b8
        b_ref[pl.ds(r0, SUBLANES), :] = h
        return h[SUBLANES - 1:SUBLANES, :]

    h_last = lax.fori_loop(0, tt // SUBLANES, group, hc_ref[...])
    hc_ref[...] = h_last
    hl_ref[0] = h_last

    y = jnp.dot((b_ref[...] * gg_ref[...]).astype(BF16), wout_ref[...], preferred_element_type=F32)
    xo_ref[0] = x + y


def _rg_prompt(x, g, w_in, conv_w, conv_b, wg, ba, bi, lam, w_out):
    bsz, t, d = x.shape
    tt = RG_TILE
    vec = lambda n: _resident((1, n))
    return pl.pallas_call(
        _rg_prompt_kernel,
        name="rg_prompt",
        grid=(bsz, t // tt),
        in_specs=[
            pl.BlockSpec((1, tt, d), lambda b, i: (b, i, 0)),
            vec(d),
            _resident((d, 2 * LRU_WIDTH)),
            _resident((CONV_WIDTH, LRU_WIDTH)),
            vec(LRU_WIDTH),
            _resident((N_GATE_BLOCKS, GATE_BLOCK, 2 * GATE_BLOCK)),
            vec(LRU_WIDTH), vec(LRU_WIDTH), vec(LRU_WIDTH),
            _resident((LRU_WIDTH, d)),
        ],
        out_specs=[
            pl.BlockSpec((1, tt, d), lambda b, i: (b, i, 0)),
            pl.BlockSpec((1, CONV_WIDTH - 1, LRU_WIDTH), lambda b, i: (b, 0, 0)),
            pl.BlockSpec((1, 1, LRU_WIDTH), lambda b, i: (b, 0, 0)),
        ],
        out_shape=[
            jax.ShapeDtypeStruct((bsz, t, d), F32),
            jax.ShapeDtypeStruct((bsz, CONV_WIDTH - 1, LRU_WIDTH), F32),
            jax.ShapeDtypeStruct((bsz, 1, LRU_WIDTH), F32),
        ],
        scratch_shapes=[
            pltpu.VMEM((tt + SUBLANES, LRU_WIDTH), F32),
            pltpu.VMEM((tt, LRU_WIDTH), F32),
            pltpu.VMEM((tt, LRU_WIDTH), F32),
            pltpu.VMEM((tt, LRU_WIDTH), F32),
            pltpu.VMEM((1, LRU_WIDTH), F32),
        ],
        compiler_params=_params("arbitrary", "arbitrary"),
    )(x, g, w_in, conv_w, conv_b, wg, ba, bi, lam, w_out)


def _ffn_body(x_ref, g_ref, win_ref, wout_ref, gf_ref, o_ref, act_ref, *, final, precision):
    cdt = win_ref.dtype
    x = x_ref[...]
    xn = _rms(x, g_ref[...]).astype(cdt)
    for c in range(D_FF // FF_CHUNK):
        lo = c * FF_CHUNK
        gate = jnp.dot(xn, win_ref[:, lo:lo + FF_CHUNK], preferred_element_type=F32,
                       precision=precision)
        up = jnp.dot(xn, win_ref[:, D_FF + lo:D_FF + lo + FF_CHUNK], preferred_element_type=F32,
                     precision=precision)
        act_ref[:, lo:lo + FF_CHUNK] = (jax.nn.silu(gate) * up).astype(cdt)
    y = x + jnp.dot(act_ref[...], wout_ref[...], preferred_element_type=F32, precision=precision)
    if final:
        y = _rms(y, gf_ref[...])
    o_ref[...] = y


def _ffn_kernel(*refs, final, precision):
    _ffn_body(*refs, final=final, precision=precision)


def _ffn_kmeans_kernel(pt_ref, x_ref, g_ref, win_ref, wout_ref, gf_ref, ck_ref, o_ref, km_ref,
                       act_ref, kbuf, sem, *, final, precision, n_pages, first_page):
    s = pl.program_id(0)
    cur = s % 2

    def copies(seq, buf):
        return [pltpu.make_async_copy(ck_ref.at[pt_ref[seq * n_pages + first_page + p]],
                                      kbuf.at[buf, p], sem.at[buf])
                for p in range(KMEANS_PAGES)]

    @pl.when(s == 0)
    def _():
        for cp in copies(s, cur):
            cp.start()

    @pl.when(s + 1 < pl.num_programs(0))
    def _():
        for cp in copies(s + 1, 1 - cur):
            cp.start()

    for cp in copies(s, cur):
        cp.wait()

    means = []
    for blk in range(KMEANS_PAGES // PAGES_PER_BLOCK):
        t = sum(jnp.sum(kbuf[cur, blk * PAGES_PER_BLOCK + h], axis=0)
                for h in range(PAGES_PER_BLOCK))
        t = sum(t[par * N_KV_HEADS:(par + 1) * N_KV_HEADS] for par in range(ROW_PARITIES))
        means.append(jnp.concatenate([t[kvh:kvh + 1, :] for kvh in range(N_KV_HEADS)], axis=1))
    km_ref[0] = jnp.concatenate(means, axis=0) * (1.0 / MOBA_BLOCK)

    _ffn_body(x_ref, g_ref, win_ref, wout_ref, gf_ref, o_ref, act_ref, final=final,
              precision=precision)


def _ffn_attn_kernel(pt_ref, sel_ref, x_ref, g_ref, win_ref, wout_ref, gf_ref, q_ref, kn_ref,
                     vn_ref, ck_ref, cv_ref, o_ref, ao_ref, act_ref, kbuf, vbuf, sem, *,
                     final, precision, n_pages):
    _attn_sample_body(pt_ref, sel_ref, q_ref, kn_ref, vn_ref, ck_ref, cv_ref, ao_ref, kbuf, vbuf,
                      sem, n_pages=n_pages)
    _ffn_body(x_ref, g_ref, win_ref, wout_ref, gf_ref, o_ref, act_ref, final=final,
              precision=precision)


def _ffn(x, g, w_in, w_out, g_final, layer, *, final, tile, kmeans=None, attn=None):
    n, d = x.shape
    precision = HIGHEST if w_in.dtype == F32 else None
    in_specs = [
        pl.BlockSpec((tile, d), lambda i, *_: (i, 0)),
        _resident((1, d)),
        _layer(layer, (d, 2 * D_FF)),
        _layer(layer, (D_FF, d)),
        _resident((1, d)),
    ]
    out_spec = pl.BlockSpec((tile, d), lambda i, *_: (i, 0))
    out_shape = jax.ShapeDtypeStruct((n, d), F32)
    act = pltpu.VMEM((tile, D_FF), w_in.dtype)
    any_space = pl.BlockSpec(memory_space=pl.ANY)
    if attn is not None:
        pt_flat, sel_flat, q3, k_new3, v_new3, ck, cv, n_pages = attn
        n_seq = n // tile
        assert q3.shape[0] == n_seq
        n_slots = MOBA_TOPK * PAGES_PER_BLOCK
        row = lambda width: pl.BlockSpec((1, 1, width), lambda i, *_: (i, 0, 0))
        gathered = pltpu.VMEM((2, N_HEADS, n_slots, PAGE_SIZE, HEAD_DIM), F32)
        return pl.pallas_call(
            functools.partial(_ffn_attn_kernel, final=final, precision=precision, n_pages=n_pages),
            name="ffn_attn",
            grid_spec=pltpu.PrefetchScalarGridSpec(
                num_scalar_prefetch=2,
                grid=(n_seq,),
                in_specs=in_specs + [row(D_MODEL), row(KV_DIM), row(KV_DIM), any_space, any_space],
                out_specs=[out_spec, row(D_MODEL)],
                scratch_shapes=[act, gathered, gathered, pltpu.SemaphoreType.DMA((2, 2))],
            ),
            out_shape=[out_shape, jax.ShapeDtypeStruct((n_seq, 1, D_MODEL), F32)],
            compiler_params=_params("arbitrary"),
        )(pt_flat, sel_flat, x, g, w_in, w_out, g_final, q3, k_new3, v_new3, ck, cv)
    if kmeans is None:
        return pl.pallas_call(
            functools.partial(_ffn_kernel, final=final, precision=precision),
            name="ffn",
            grid=(n // tile,),
            in_specs=in_specs,
            out_specs=out_spec,
            out_shape=out_shape,
            scratch_shapes=[act],
            compiler_params=_params("arbitrary"),
        )(x, g, w_in, w_out, g_final)

    pt_flat, cache, n_pages, first_page = kmeans
    n_seq = n // tile
    assert pt_flat.shape[0] == n_seq * n_pages
    blocks = KMEANS_PAGES // PAGES_PER_BLOCK
    return pl.pallas_call(
        functools.partial(_ffn_kmeans_kernel, final=final, precision=precision, n_pages=n_pages,
                          first_page=first_page),
        name="ffn_kmeans",
        grid_spec=pltpu.PrefetchScalarGridSpec(
            num_scalar_prefetch=1,
            grid=(n_seq,),
            in_specs=in_specs + [any_space],
            out_specs=[out_spec, pl.BlockSpec((1, blocks, KV_DIM), lambda i, *_: (i, 0, 0))],
            scratch_shapes=[act, pltpu.VMEM((2, KMEANS_PAGES) + PAGE_VIEW, F32),
                            pltpu.SemaphoreType.DMA((2,))],
        ),
        out_shape=[out_shape, jax.ShapeDtypeStruct((n_seq, blocks, KV_DIM), F32)],
        compiler_params=_params("arbitrary"),
    )(pt_flat, x, g, w_in, w_out, g_final, cache)


def _kv_prompt_kernel(x_ref, g_ref, w_ref, c_ref, slo_ref, shi_ref,
                      k_ref, v_ref, kb_ref, vt_ref, km_ref):
    xn = _rms(x_ref[0], g_ref[...]).astype(BF16)
    kv = jnp.dot(xn, w_ref[...], preferred_element_type=F32)
    tile4 = lambda r: jnp.concatenate([r[...]] * N_KV_HEADS, axis=1)
    k = _rope(kv[:, :KV_DIM], tile4(c_ref), tile4(slo_ref), tile4(shi_ref))
    v = kv[:, KV_DIM:]
    for kvh in range(N_KV_HEADS):
        lanes = slice(kvh * HEAD_DIM, (kvh + 1) * HEAD_DIM)
        rows = pl.ds(kvh, KV_TILE, stride=N_KV_HEADS)
        k_ref[0, rows, :] = k[:, lanes]
        v_ref[0, rows, :] = v[:, lanes]
    kb_ref[0] = k.astype(BF16)
    for bb in range(KV_TILE // MOBA_BLOCK):
        rows = slice(bb * MOBA_BLOCK, (bb + 1) * MOBA_BLOCK)
        vt_ref[0, bb] = v[rows, :].T.astype(BF16)
        km_ref[0, bb] = jnp.sum(k[rows, :], axis=0, keepdims=True) * (1.0 / MOBA_BLOCK)


def _kv_prompt(x, g, w_kv, c, s_lo, s_hi):
    bsz, t, d = x.shape
    tt = KV_TILE
    nbt = tt // MOBA_BLOCK
    nb = t // MOBA_BLOCK
    tab = pl.BlockSpec((tt, HEAD_DIM), lambda b, i: (i, 0))
    row_spec = pl.BlockSpec((1, tt, KV_DIM), lambda b, i: (b, i, 0))
    head_rows = pl.BlockSpec((1, tt * N_KV_HEADS, HEAD_DIM), lambda b, i: (b, i, 0))
    return pl.pallas_call(
        _kv_prompt_kernel,
        name="kv_prompt",
        grid=(bsz, t // tt),
        in_specs=[
            pl.BlockSpec((1, tt, d), lambda b, i: (b, i, 0)),
            _resident((1, d)),
            _resident((d, 2 * KV_DIM)),
            tab, tab, tab,
        ],
        out_specs=[
            head_rows, head_rows, row_spec,
            pl.BlockSpec((1, nbt, KV_DIM, MOBA_BLOCK), lambda b, i: (b, i, 0, 0)),
            pl.BlockSpec((1, nbt, 1, KV_DIM), lambda b, i: (b, i, 0, 0)),
        ],
        out_shape=[
            jax.ShapeDtypeStruct((bsz, t * N_KV_HEADS, HEAD_DIM), F32),
            jax.ShapeDtypeStruct((bsz, t * N_KV_HEADS, HEAD_DIM), F32),
            jax.ShapeDtypeStruct((bsz, t, KV_DIM), BF16),
            jax.ShapeDtypeStruct((bsz, nb, KV_DIM, MOBA_BLOCK), BF16),
            jax.ShapeDtypeStruct((bsz, nb, 1, KV_DIM), F32),
        ],
        compiler_params=_params("arbitrary", "arbitrary"),
    )(x, g, w_kv, c, s_lo, s_hi)


def _select_bias(gate, n_allowed, n_valid):
    nb, nq = gate.shape
    rowi = lax.broadcasted_iota(jnp.int32, (nb, nq), 0).astype(F32)
    g = jnp.where(rowi < n_allowed, gate, NEG)
    bias = jnp.where(rowi == n_allowed, 0.0, NEG)
    for r in range(MOBA_TOPK):
        mx = jnp.max(g, axis=0, keepdims=True)
        idx = jnp.min(jnp.where(g == mx, rowi, float(nb)), axis=0, keepdims=True)
        ok = (n_valid > r).astype(F32)
        idx = idx * ok - (1.0 - ok)
        pick = rowi == idx
        bias = jnp.where(pick, 0.0, bias)
        g = jnp.where(pick, NEG, g)
    return bias


def _attn_prompt_kernel(x_ref, g_ref, wq_ref, wo_ref, c_ref, slo_ref, shi_ref, k_ref, vt_ref,
                        km_ref, o_ref, q2s_ref, s0_ref, s1_ref, m_ref, acc_ref, attn_ref):
    tq = MOBA_BLOCK
    nq = ATTN_CHAIN_HEADS * tq
    i = pl.program_id(1)
    x = x_ref[0]
    xn = _rms(x, g_ref[...]).astype(BF16)
    q = jnp.dot(xn, wq_ref[...], preferred_element_type=F32)
    c, s_lo, s_hi = c_ref[...], slo_ref[...], shi_ref[...]
    own = i.astype(F32)

    key_pos = lax.broadcasted_iota(jnp.int32, (tq, nq), 0)
    qry_pos = lax.broadcasted_iota(jnp.int32, (tq, nq), 1) % tq
    causal = key_pos <= qry_pos
    n_chains = N_HEADS // ATTN_CHAIN_HEADS
    chain_heads = [range(ch * ATTN_CHAIN_HEADS, (ch + 1) * ATTN_CHAIN_HEADS)
                   for ch in range(n_chains)]
    kv_lanes = [slice((hs[0] // GROUP) * HEAD_DIM, (hs[0] // GROUP + 1) * HEAD_DIM)
                for hs in chain_heads]

    ones_rows = jnp.where(lax.broadcasted_iota(jnp.int32, (BF16_ROWS, tq), 0) == 0, 1.0, 0.0
                          ).astype(BF16)

    def vt_ones(j, lanes):
        return jnp.concatenate([vt_ref[0, j, lanes, :], ones_rows], axis=0)

    for kvh, lanes in enumerate(kv_lanes):
        q2 = jnp.concatenate(
            [_rope(q[:, h * HEAD_DIM:(h + 1) * HEAD_DIM], c, s_lo, s_hi)
             for h in chain_heads[kvh]], axis=0)
        gate = lax.dot_general(km_ref[0, :, lanes], q2, NT_DIMS, precision=HIGHEST,
                               preferred_element_type=F32)
        bias = _select_bias(gate, own, own)
        bias_t = jnp.concatenate(
            [bias, jnp.zeros((HEAD_DIM - bias.shape[0], nq), F32)], axis=0).T
        q2s = (q2 * (SM_SCALE * LOG2E)).astype(BF16)
        q2s_ref[kvh] = jnp.concatenate([q2s, bias_t.astype(BF16)], axis=1)
        m_ref[kvh] = jnp.full((1, nq), NEG, F32)
        acc_ref[kvh] = jnp.zeros(acc_ref.shape[1:], F32)

    block_lane = lax.broadcasted_iota(jnp.int32, (1, HEAD_DIM), 1)

    def qk(j, s_out):
        one_hot = jnp.broadcast_to(jnp.where(block_lane == j, 1.0, 0.0).astype(BF16),
                                   (tq, HEAD_DIM))
        for kvh, lanes in enumerate(kv_lanes):
            k_aug = jnp.concatenate([k_ref[0, j, :, lanes], one_hot], axis=1)
            s_out[kvh] = lax.dot_general(k_aug, q2s_ref[kvh], NT_DIMS,
                                         preferred_element_type=F32)

    def block(j, s_cur, s_nxt):
        if s_nxt is not None:
            qk(j + 1, s_nxt)
        for kvh, lanes in enumerate(kv_lanes):
            s = s_cur[kvh]
            if s_nxt is None:
                s = jnp.where(causal, s, NEG)
            m = m_ref[kvh]
            m_new = jnp.maximum(m, jnp.max(s, axis=0, keepdims=True))
            alpha = jnp.exp2(m - m_new)
            p = jnp.exp2(s - m_new)
            m_ref[kvh] = m_new
            acc_ref[kvh] = alpha * acc_ref[kvh] + jnp.dot(
                vt_ones(j, lanes), p.astype(BF16), preferred_element_type=F32)

    qk(0, s0_ref)
    bufs = (s0_ref, s1_ref)

    def run(first, count):
        for k in range(count):
            block(first + k, bufs[k % 2], bufs[(k + 1) % 2])

    def long_trip(t, carry):
        run(ATTN_UNROLL * t, ATTN_UNROLL)
        return carry

    n_long = i // ATTN_UNROLL
    lax.fori_loop(0, n_long, long_trip, 0)
    done = n_long * ATTN_UNROLL

    def pair_trip(t, carry):
        run(done + 2 * t, 2)
        return carry

    lax.fori_loop(0, (i - done) // 2, pair_trip, 0)

    @pl.when(i % 2 == 1)
    def _():
        run(i - 1, 1)
        block(i, s1_ref, None)

    @pl.when(i % 2 == 0)
    def _():
        block(i, s0_ref, None)

    for kvh in range(n_chains):
        o = (acc_ref[kvh, :HEAD_DIM, :] / acc_ref[kvh, HEAD_DIM:HEAD_DIM + 1, :]).T
        for j, h in enumerate(chain_heads[kvh]):
            attn_ref[:, h * HEAD_DIM:(h + 1) * HEAD_DIM] = o[j * tq:(j + 1) * tq, :].astype(BF16)

    o_ref[0] = x + jnp.dot(attn_ref[...], wo_ref[...], preferred_element_type=F32)


def _attn_prompt(x, g, w_q, w_o, c, s_lo, s_hi, kb, vt, km):
    bsz, t, d = x.shape
    tq = MOBA_BLOCK
    nb = t // tq
    n_chains, nq = N_HEADS // ATTN_CHAIN_HEADS, ATTN_CHAIN_HEADS * tq
    tab = pl.BlockSpec((tq, HEAD_DIM), lambda b, i: (i, 0))
    per_seq = lambda shape: pl.BlockSpec((1,) + shape, lambda b, i: (b,) + (0,) * len(shape),
                                         pipeline_mode=pl.Buffered(1))
    return pl.pallas_call(
        _attn_prompt_kernel,
        name="attn_prompt",
        grid=(bsz, nb),
        in_specs=[
            pl.BlockSpec((1, tq, d), lambda b, i: (b, i, 0)),
            _resident((1, d)),
            _resident((d, d)),
            _resident((d, d)),
            tab, tab, tab,
            per_seq((nb, tq, KV_DIM)),
            per_seq((nb, KV_DIM, tq)),
            per_seq((nb, KV_DIM)),
        ],
        out_specs=pl.BlockSpec((1, tq, d), lambda b, i: (b, i, 0)),
        out_shape=jax.ShapeDtypeStruct((bsz, t, d), F32),
        scratch_shapes=[
            pltpu.VMEM((n_chains, nq, 2 * HEAD_DIM), BF16),
            pltpu.VMEM((n_chains, tq, nq), F32),
            pltpu.VMEM((n_chains, tq, nq), F32),
            pltpu.VMEM((n_chains, 1, nq), F32),
            pltpu.VMEM((n_chains, HEAD_DIM + BF16_ROWS, nq), F32),
            pltpu.VMEM((tq, d), BF16),
        ],
        compiler_params=_params("arbitrary", "arbitrary"),
    )(x, g, w_q, w_o, c, s_lo, s_hi, kb, vt, km)


def _dot_hi(a, b):
    return jnp.dot(a, b, preferred_element_type=F32, precision=HIGHEST)


def _rg_sample_kernel(x_ref, cs_ref, h0_ref, g_ref, win_ref, cw_ref, cb_ref, wa_ref, wi_ref,
                      ba_ref, bi_ref, lam_ref, wout_ref, xo_ref, cso_ref, ho_ref):
    x = x_ref[...]
    proj = _dot_hi(_rms(x, g_ref[...]), win_ref[...])
    gate, u = proj[:, :LRU_WIDTH], proj[:, LRU_WIDTH:]
    uc = cb_ref[...] + cs_ref[0] * cw_ref[0:1, :]
    for j in range(1, CONV_WIDTH - 1):
        uc = uc + cs_ref[j] * cw_ref[j:j + 1, :]
    uc = uc + u * cw_ref[CONV_WIDTH - 1:CONV_WIDTH, :]
    for j in range(CONV_WIDTH - 2):
        cso_ref[j] = cs_ref[j + 1]
    cso_ref[CONV_WIDTH - 2] = u

    sp = _softplus(-lam_ref[...])
    h0 = h0_ref[...]
    hs = []
    for n in range(N_GATE_BLOCKS):
        sl = slice(n * GATE_BLOCK, (n + 1) * GATE_BLOCK)
        ucn = uc[:, sl]
        a, b = _lru_coeffs(_dot_hi(ucn, wa_ref[n]) + ba_ref[:, sl],
                           _dot_hi(ucn, wi_ref[n]) + bi_ref[:, sl], ucn, sp[:, sl])
        hs.append(b + a * h0[:, sl])
    h = jnp.concatenate(hs, axis=1)
    ho_ref[...] = h
    xo_ref[...] = x + _dot_hi(h * jax.nn.gelu(gate), wout_ref[...])


def _rg_sample(x, cs, h0, g, w_in, conv_w, conv_b, wa, wi, ba, bi, lam, w_out, layer):
    n, d = x.shape
    vec = _resident((1, LRU_WIDTH))
    gates = _layer(layer, (N_GATE_BLOCKS, GATE_BLOCK, GATE_BLOCK))
    return pl.pallas_call(
        _rg_sample_kernel,
        name="rg_sample",
        grid=(1,),
        in_specs=[
            _resident((n, d)), _resident((CONV_WIDTH - 1, n, LRU_WIDTH)), _resident((n, LRU_WIDTH)),
            _resident((1, d)), _layer(layer, (d, 2 * LRU_WIDTH)), _resident((CONV_WIDTH, LRU_WIDTH)),
            vec, gates, gates, vec, vec, vec, _layer(layer, (LRU_WIDTH, d)),
        ],
        out_specs=[_whole((n, d)), _whole((CONV_WIDTH - 1, n, LRU_WIDTH)), _whole((n, LRU_WIDTH))],
        out_shape=[
            jax.ShapeDtypeStruct((n, d), F32),
            jax.ShapeDtypeStruct((CONV_WIDTH - 1, n, LRU_WIDTH), F32),
            jax.ShapeDtypeStruct((n, LRU_WIDTH), F32),
        ],
        compiler_params=_params("arbitrary"),
    )(x, cs, h0, g, w_in, conv_w, conv_b, wa, wi, ba, bi, lam, w_out)


def _proj_sample_kernel(x_ref, g_ref, w_ref, c_ref, slo_ref, shi_ref, o_ref, *, n_rope_heads):
    y = _dot_hi(_rms(x_ref[...], g_ref[...]), w_ref[...])
    if n_rope_heads:
        tile = lambda r: jnp.concatenate([r[...]] * n_rope_heads, axis=1)
        width = n_rope_heads * HEAD_DIM
        rot = _rope(y[:, :width], tile(c_ref), tile(slo_ref), tile(shi_ref))
        y = jnp.concatenate([rot, y[:, width:]], axis=1) if width < y.shape[1] else rot
    o_ref[...] = y


def _proj_sample(x, g, w, layer, c, s_lo, s_hi, n_rope_heads):
    n, d = x.shape
    width = w.shape[2]
    tab = _resident((1, HEAD_DIM))
    return pl.pallas_call(
        functools.partial(_proj_sample_kernel, n_rope_heads=n_rope_heads),
        name="proj_sample",
        grid=(1,),
        in_specs=[_resident((n, d)), _resident((1, d)), _layer(layer, (d, width)), tab, tab, tab],
        out_specs=_whole((n, width)),
        out_shape=jax.ShapeDtypeStruct((n, width), F32),
        compiler_params=_params("arbitrary"),
    )(x, g, w, c, s_lo, s_hi)


def _gate_sample_kernel(q_ref, km_ref, sel_ref):
    km = km_ref[...]
    n_seq, nb, _ = km.shape
    blk = lax.broadcasted_iota(jnp.int32, (n_seq, nb, 1), 1).astype(F32)
    lane = lax.broadcasted_iota(jnp.int32, (n_seq, 1, HEAD_DIM), 2)
    for h in range(N_HEADS):
        kvh = h // GROUP
        qh = q_ref[:, :, h * HEAD_DIM:(h + 1) * HEAD_DIM]
        g = jnp.sum(km[:, :, kvh * HEAD_DIM:(kvh + 1) * HEAD_DIM] * qh, axis=-1, keepdims=True)
        out = jnp.zeros((n_seq, 1, HEAD_DIM), F32)
        for r in range(MOBA_TOPK):
            mx = jnp.max(g, axis=1, keepdims=True)
            idx = jnp.min(jnp.where(g == mx, blk, float(nb)), axis=1, keepdims=True)
            g = jnp.where(blk == idx, NEG, g)
            out = jnp.where(lane == r, idx, out)
        sel_ref[h] = out.astype(jnp.int32)


def _gate_sample(q3, km):
    n_seq = q3.shape[0]
    return pl.pallas_call(
        _gate_sample_kernel,
        name="gate_sample",
        out_shape=jax.ShapeDtypeStruct((N_HEADS, n_seq, 1, HEAD_DIM), jnp.int32),
        compiler_params=pltpu.CompilerParams(vmem_limit_bytes=VMEM_LIMIT),
    )(q3, km)


def _attn_sample_body(pt_ref, sel_ref, q_ref, kn_ref, vn_ref, ck_ref, cv_ref, o_ref,
                      kbuf, vbuf, sem, *, n_pages):
    b = pl.program_id(0)
    n_slots = MOBA_TOPK * PAGES_PER_BLOCK
    rows = PAGE_SIZE // ROW_PARITIES

    def copies(seq, buf):
        out = []
        for h in range(N_HEADS):
            kvh = h // GROUP
            for r in range(MOBA_TOPK):
                blk = sel_ref[(seq * N_HEADS + h) * MOBA_TOPK + r]
                for half in range(PAGES_PER_BLOCK):
                    page = pt_ref[seq * n_pages + PAGES_PER_BLOCK * blk + half]
                    slot = r * PAGES_PER_BLOCK + half
                    for par in range(ROW_PARITIES):
                        sub = par * N_KV_HEADS + kvh
                        dst = (buf, h, slot, pl.ds(par * rows, rows))
                        out.append(pltpu.make_async_copy(ck_ref.at[page, :, sub], kbuf.at[dst], sem.at[0, buf]))
                        out.append(pltpu.make_async_copy(cv_ref.at[page, :, sub], vbuf.at[dst], sem.at[1, buf]))
        return out

    cur = b % 2

    def start_all(seq, buf):
        for k, cp in enumerate(copies(seq, buf)):
            cp.start(priority=(k // 2) % 2)

    @pl.when(b == 0)
    def _():
        start_all(b, cur)

    @pl.when(b + 1 < pl.num_programs(0))
    def _():
        start_all(b + 1, 1 - cur)

    for cp in copies(b, cur):
        cp.wait()

    for h in range(N_HEADS):
        kvh = h // GROUP
        qh = q_ref[0, :, h * HEAD_DIM:(h + 1) * HEAD_DIM] * SM_SCALE
        kn = kn_ref[0, :, kvh * HEAD_DIM:(kvh + 1) * HEAD_DIM]
        vn = vn_ref[0, :, kvh * HEAD_DIM:(kvh + 1) * HEAD_DIM]
        kk = kbuf[cur, h].reshape(n_slots * PAGE_SIZE, HEAD_DIM)
        vv = vbuf[cur, h].reshape(n_slots * PAGE_SIZE, HEAD_DIM)
        s = jnp.sum(kk * qh, axis=-1, keepdims=True)
        s_new = jnp.sum(kn * qh, axis=-1, keepdims=True)
        m = jnp.maximum(jnp.max(s, axis=0, keepdims=True), s_new)
        p = jnp.exp(s - m)
        p_new = jnp.exp(s_new - m)
        l = jnp.sum(p, axis=0, keepdims=True) + p_new
        acc = jnp.sum(p * vv, axis=0, keepdims=True) + p_new * vn
        o_ref[0, :, h * HEAD_DIM:(h + 1) * HEAD_DIM] = acc / l


def _oproj_sample_kernel(x_ref, a_ref, w_ref, o_ref):
    o_ref[...] = x_ref[...] + _dot_hi(a_ref[...], w_ref[...])


def _oproj_sample(x, attn, w_o, layer):
    n, d = x.shape
    return pl.pallas_call(
        _oproj_sample_kernel,
        name="oproj_sample",
        grid=(1,),
        in_specs=[_resident((n, d)), _resident((n, d)), _layer(layer, (d, d))],
        out_specs=_whole((n, d)),
        out_shape=jax.ShapeDtypeStruct((n, d), F32),
        compiler_params=_params("arbitrary"),
    )(x, attn, w_o)


def _step(x_prompt, x_sample3, state_conv, state_h, ck, cv, pt_flat, n_pages, w):
    bsz, t, d = x_prompt.shape
    n_seq = x_sample3.shape[0]
    row = lambda a: a.reshape(1, -1)
    bf = lambda a: a.astype(BF16)
    n_rg = w["w_rg_in"].shape[0]
    depth = w["g_mix"].shape[0]
    nb = t // MOBA_BLOCK
    assert n_pages == n_rg * KMEANS_PAGES and (bsz * t) // FFN_TILE == n_seq

    c_p, slo_p, shi_p = _rope_tables(jnp.arange(t, dtype=jnp.int32))
    c_s, slo_s, shi_s = _rope_tables(jnp.full((1,), n_pages * PAGE_SIZE, jnp.int32))
    w_ffn_in_bf, w_ffn_out_bf = bf(w["w_ffn_in"]), bf(w["w_ffn_out"])

    def ffn_prompt(x, l, **side):
        out = _ffn(x.reshape(bsz * t, d), row(w["g_ffn"][l]), w_ffn_in_bf, w_ffn_out_bf,
                   row(w["g_final"]), l, final=(l == depth - 1), tile=FFN_TILE, **side)
        return out[0].reshape(bsz, t, d), out[1]

    def ffn_sample(x, l):
        w_in, w_out = (w["w_ffn_in"], w["w_ffn_out"]) if l == 0 else (w_ffn_in_bf, w_ffn_out_bf)
        return _ffn(x, row(w["g_ffn"][l]), w_in, w_out, row(w["g_final"]), l,
                    final=(l == depth - 1), tile=n_seq)

    xp, xs = x_prompt, x_sample3.reshape(n_seq, d)
    p_convs, p_hs, s_convs, s_hs, kmeans = [], [], [], [], []
    for l in range(n_rg):
        wg = jnp.concatenate([w["rg_gate_a_w"][l], w["rg_gate_i_w"][l]], axis=-1)
        xp, conv, h = _rg_prompt(
            xp, row(w["g_mix"][l]), bf(w["w_rg_in"][l]), w["rg_conv_w"][l], row(w["rg_conv_b"][l]),
            bf(wg), row(w["rg_gate_a_b"][l]), row(w["rg_gate_i_b"][l]), row(w["rg_lambda"][l]),
            bf(w["w_rg_out"][l]))
        p_convs.append(conv)
        p_hs.append(h[:, 0, :])
        xp, km_l = ffn_prompt(xp, l, kmeans=(pt_flat, ck, n_pages, l * KMEANS_PAGES))
        kmeans.append(km_l)

        xs, conv, h = _rg_sample(
            xs, jnp.swapaxes(state_conv[l], 0, 1), state_h[l], row(w["g_mix"][l]), w["w_rg_in"],
            w["rg_conv_w"][l], row(w["rg_conv_b"][l]), w["rg_gate_a_w"], w["rg_gate_i_w"],
            row(w["rg_gate_a_b"][l]), row(w["rg_gate_i_b"][l]), row(w["rg_lambda"][l]), w["w_rg_out"], l)
        s_convs.append(jnp.swapaxes(conv, 0, 1))
        s_hs.append(h)
        xs = ffn_sample(xs, l)

    k, v, kb, vt, km_p = _kv_prompt(xp, row(w["g_kv"]), bf(w["w_kv"]), c_p, slo_p, shi_p)
    kb = kb.reshape(bsz, nb, MOBA_BLOCK, KV_DIM)
    km_p = km_p.reshape(bsz, nb, KV_DIM)
    kv_s = _proj_sample(xs, row(w["g_kv"]), w["w_kv"][None], 0, c_s, slo_s, shi_s, N_KV_HEADS)
    k_new, v_new = kv_s[:, :KV_DIM], kv_s[:, KV_DIM:]
    km_s = jnp.concatenate(kmeans, axis=1)

    for a in range(depth - n_rg):
        l = n_rg + a
        xp = _attn_prompt(xp, row(w["g_mix"][l]), bf(w["w_q"][a]), bf(w["w_o"][a]), c_p, slo_p, shi_p,
                          kb, vt, km_p)
        q3 = _proj_sample(xs, row(w["g_mix"][l]), w["w_q"], a, c_s, slo_s, shi_s,
                          N_HEADS).reshape(n_seq, 1, d)
        sel = _gate_sample(q3, km_s)[:, :, 0, :MOBA_TOPK]
        sel_flat = jnp.transpose(sel, (1, 0, 2)).reshape(-1)
        xp, attn_s = ffn_prompt(xp, l, attn=(pt_flat, sel_flat, q3, k_new.reshape(n_seq, 1, KV_DIM),
                                             v_new.reshape(n_seq, 1, KV_DIM), ck, cv, n_pages))
        xs = _oproj_sample(xs, attn_s.reshape(n_seq, d), w["w_o"], a)
        xs = ffn_sample(xs, l)

    shape_p = (bsz, t, N_KV_HEADS, HEAD_DIM)
    shape_s = (n_seq, 1, N_KV_HEADS, HEAD_DIM)
    return (xp, xs.reshape(n_seq, 1, d), jnp.stack(p_convs), jnp.stack(p_hs),
            k.reshape(shape_p), v.reshape(shape_p), jnp.stack(s_convs), jnp.stack(s_hs),
            k_new.reshape(shape_s), v_new.reshape(shape_s))


def kernel(x_prompt, x_sample, cache_k, cache_v, state_conv, state_h, page_table,
           g_mix, g_ffn, w_rg_in, rg_conv_w, rg_conv_b, rg_gate_a_w, rg_gate_a_b,
           rg_gate_i_w, rg_gate_i_b, rg_lambda, w_rg_out, g_kv, w_kv, w_q, w_o,
           w_ffn_in, w_ffn_out, g_final):
    assert x_sample.shape[1] == 1 and x_prompt.shape[1] % MOBA_BLOCK == 0
    assert MOBA_TOPK <= page_table.shape[1] // PAGES_PER_BLOCK
    w = dict(g_mix=g_mix, g_ffn=g_ffn, w_rg_in=w_rg_in, rg_conv_w=rg_conv_w, rg_conv_b=rg_conv_b,
             rg_gate_a_w=rg_gate_a_w, rg_gate_a_b=rg_gate_a_b, rg_gate_i_w=rg_gate_i_w,
             rg_gate_i_b=rg_gate_i_b, rg_lambda=rg_lambda, w_rg_out=w_rg_out, g_kv=g_kv, w_kv=w_kv,
             w_q=w_q, w_o=w_o, w_ffn_in=w_ffn_in, w_ffn_out=w_ffn_out, g_final=g_final)
    ck = cache_k.reshape((cache_k.shape[0],) + PAGE_VIEW)
    cv = cache_v.reshape((cache_v.shape[0],) + PAGE_VIEW)
    return _step(x_prompt, x_sample, state_conv, state_h, ck, cv, page_table.reshape(-1),
                 page_table.shape[1], w)
```

```python
import functools
import math

import jax
import jax.numpy as jnp
from jax import lax
from jax.experimental import pallas as pl
from jax.experimental.pallas import tpu as pltpu

D_MODEL = 1024
LRU_WIDTH = D_MODEL
N_GATE_BLOCKS = 8
GATE_BLOCK = LRU_WIDTH // N_GATE_BLOCKS
CONV_WIDTH = 4
LRU_C = 8.0
N_HEADS = 8
HEAD_DIM = D_MODEL // N_HEADS
N_KV_HEADS = 4
KV_DIM = N_KV_HEADS * HEAD_DIM
GROUP = N_HEADS // N_KV_HEADS
ROT_DIM = HEAD_DIM // 4
ROPE_THETA = 500000.0
MOBA_BLOCK = 256
MOBA_TOPK = 3
PAGE_SIZE = 128
PAGES_PER_BLOCK = MOBA_BLOCK // PAGE_SIZE
D_FF = -(-8 * D_MODEL // (3 * 256)) * 256
EPS = 1e-6
NEG = -1e30
SM_SCALE = 1.0 / math.sqrt(HEAD_DIM)
LOG2E = math.log2(math.e)

F32 = jnp.float32
BF16 = jnp.bfloat16
HIGHEST = lax.Precision.HIGHEST

SUBLANES = 8
BF16_ROWS = 16
FF_CHUNK = 256
RG_TILE = 512
FFN_TILE = 512
KV_TILE = 512
KMEANS_PAGES = 32
ATTN_UNROLL = 8
ATTN_CHAIN_HEADS = GROUP
ROW_PARITIES = SUBLANES // N_KV_HEADS
PAGE_VIEW = (PAGE_SIZE // ROW_PARITIES, SUBLANES, HEAD_DIM)
VMEM_LIMIT = 56 * 1024 * 1024

NT_DIMS = (((1,), (1,)), ((), ()))


def _params(*sem):
    return pltpu.CompilerParams(dimension_semantics=sem, vmem_limit_bytes=VMEM_LIMIT)


def _resident(shape):
    zeros = (0,) * len(shape)
    return pl.BlockSpec(shape, lambda *_: zeros, pipeline_mode=pl.Buffered(1))


def _whole(shape):
    zeros = (0,) * len(shape)
    return pl.BlockSpec(shape, lambda *_: zeros)


def _layer(layer, shape):
    zeros = (0,) * len(shape)
    return pl.BlockSpec((None,) + shape, lambda *_: (layer,) + zeros, pipeline_mode=pl.Buffered(1))


def _rms(x, g):
    ms = jnp.mean(x * x, axis=-1, keepdims=True)
    return x * lax.rsqrt(ms + EPS) * g


def _softplus(z):
    return jnp.maximum(z, 0.0) + jnp.log1p(jnp.exp(-jnp.abs(z)))


def _rope(x, c, s_lo, s_hi):
    n = x.shape[-1]
    half = ROT_DIM // 2
    return x * c + pltpu.roll(x, half, 1) * s_hi + pltpu.roll(x, n - half, 1) * s_lo


def _rope_tables(pos):
    half = ROT_DIM // 2
    inv = ROPE_THETA ** (-jnp.arange(half, dtype=F32) / half)
    ang = pos.astype(F32)[:, None] * inv[None, :]
    cos, sin = jnp.cos(ang), jnp.sin(ang)
    t = pos.shape[0]
    rest = HEAD_DIM - ROT_DIM
    c = jnp.concatenate([cos, cos, jnp.ones((t, rest), F32)], axis=1)
    s_hi = jnp.concatenate([jnp.zeros((t, half), F32), sin, jnp.zeros((t, rest), F32)], axis=1)
    s_lo = jnp.concatenate([-sin, jnp.zeros((t, half + rest), F32)], axis=1)
    return c, s_lo, s_hi


def _lru_coeffs(z_a, z_i, uc, sp):
    r = jax.nn.sigmoid(z_a)
    i = jax.nn.sigmoid(z_i)
    log_a = (-LRU_C * r) * sp
    a = jnp.exp(log_a)
    b = jnp.sqrt(-jnp.tanh(log_a) * (a * a + 1.0)) * (i * uc)
    return a, b


def _rg_prompt_kernel(x_ref, g_ref, win_ref, cw_ref, cb_ref, wg_ref, ba_ref, bi_ref, lam_ref,
                      wout_ref, xo_ref, conv_ref, hl_ref, ext_ref, a_ref, b_ref, gg_ref, hc_ref):
    tt = RG_TILE
    t = pl.program_id(1)

    @pl.when(t == 0)
    def _():
        ext_ref[0:SUBLANES, :] = jnp.zeros((SUBLANES, LRU_WIDTH), F32)
        hc_ref[...] = jnp.zeros_like(hc_ref)

    x = x_ref[0]
    xn = _rms(x, g_ref[...]).astype(BF16)
    proj = jnp.dot(xn, win_ref[...], preferred_element_type=F32)
    gg_ref[...] = jax.nn.gelu(proj[:, :LRU_WIDTH])
    ext_ref[SUBLANES:SUBLANES + tt, :] = proj[:, LRU_WIDTH:]

    base = SUBLANES - (CONV_WIDTH - 1)
    ext = ext_ref[...]

    def lagged(j):
        lag = CONV_WIDTH - 1 - j
        rolled = pltpu.roll(ext, lag, 0) if lag else ext
        return rolled[SUBLANES:SUBLANES + tt, :]

    uc = cb_ref[...] + lagged(0) * cw_ref[0:1, :]
    for j in range(1, CONV_WIDTH):
        uc = uc + lagged(j) * cw_ref[j:j + 1, :]
    conv_ref[0] = ext_ref[tt + base:tt + SUBLANES, :]
    ext_ref[0:SUBLANES, :] = ext_ref[tt:tt + SUBLANES, :]

    sp = _softplus(-lam_ref[...])
    for n in range(N_GATE_BLOCKS):
        sl = slice(n * GATE_BLOCK, (n + 1) * GATE_BLOCK)
        ucn = uc[:, sl]
        z = jnp.dot(ucn.astype(BF16), wg_ref[n], preferred_element_type=F32)
        a, b = _lru_coeffs(z[:, :GATE_BLOCK] + ba_ref[:, sl], z[:, GATE_BLOCK:] + bi_ref[:, sl],
                           ucn, sp[:, sl])
        a_ref[:, sl] = a
        b_ref[:, sl] = b

    row = lax.broadcasted_iota(jnp.int32, (SUBLANES, LRU_WIDTH), 0)

    def group(gi, h_prev):
        r0 = pl.multiple_of(gi * SUBLANES, SUBLANES)
        a8 = a_ref[pl.ds(r0, SUBLANES), :]
        b8 = b_ref[pl.ds(r0, SUBLANES), :]
        s = 1
        while s < SUBLANES:
            keep = row >= s
            b8 = jnp.where(keep, a8 * pltpu.roll(b8, s, 0) + b8, b8)
            a8 = jnp.where(keep, a8 * pltpu.roll(a8, s, 0), a8)
            s *= 2
        h = a8 * h_prev + b8
        b_ref[pl.ds(r0, SUBLANES), :] = h
        return h[SUBLANES - 1:SUBLANES, :]

    h_last = lax.fori_loop(0, tt // SUBLANES, group, hc_ref[...])
    hc_ref[...] = h_last
    hl_ref[0] = h_last

    y = jnp.dot((b_ref[...] * gg_ref[...]).astype(BF16), wout_ref[...], preferred_element_type=F32)
    xo_ref[0] = x + y


def _rg_prompt(x, g, w_in, conv_w, conv_b, wg, ba, bi, lam, w_out):
    bsz, t, d = x.shape
    tt = RG_TILE
    vec = lambda n: _resident((1, n))
    return pl.pallas_call(
        _rg_prompt_kernel,
        name="rg_prompt",
        grid=(bsz, t // tt),
        in_specs=[
            pl.BlockSpec((1, tt, d), lambda b, i: (b, i, 0)),
            vec(d),
            _resident((d, 2 * LRU_WIDTH)),
            _resident((CONV_WIDTH, LRU_WIDTH)),
            vec(LRU_WIDTH),
            _resident((N_GATE_BLOCKS, GATE_BLOCK, 2 * GATE_BLOCK)),
            vec(LRU_WIDTH), vec(LRU_WIDTH), vec(LRU_WIDTH),
            _resident((LRU_WIDTH, d)),
        ],
        out_specs=[
            pl.BlockSpec((1, tt, d), lambda b, i: (b, i, 0)),
            pl.BlockSpec((1, CONV_WIDTH - 1, LRU_WIDTH), lambda b, i: (b, 0, 0)),
            pl.BlockSpec((1, 1, LRU_WIDTH), lambda b, i: (b, 0, 0)),
        ],
        out_shape=[
            jax.ShapeDtypeStruct((bsz, t, d), F32),
            jax.ShapeDtypeStruct((bsz, CONV_WIDTH - 1, LRU_WIDTH), F32),
            jax.ShapeDtypeStruct((bsz, 1, LRU_WIDTH), F32),
        ],
        scratch_shapes=[
            pltpu.VMEM((tt + SUBLANES, LRU_WIDTH), F32),
            pltpu.VMEM((tt, LRU_WIDTH), F32),
            pltpu.VMEM((tt, LRU_WIDTH), F32),
            pltpu.VMEM((tt, LRU_WIDTH), F32),
            pltpu.VMEM((1, LRU_WIDTH), F32),
        ],
        compiler_params=_params("arbitrary", "arbitrary"),
    )(x, g, w_in, conv_w, conv_b, wg, ba, bi, lam, w_out)


def _ffn_body(x_ref, g_ref, win_ref, wout_ref, gf_ref, o_ref, act_ref, *, final, precision):
    cdt = win_ref.dtype
    x = x_ref[...]
    xn = _rms(x, g_ref[...]).astype(cdt)
    for c in range(D_FF // FF_CHUNK):
        lo = c * FF_CHUNK
        gate = jnp.dot(xn, win_ref[:, lo:lo + FF_CHUNK], preferred_element_type=F32,
                       precision=precision)
        up = jnp.dot(xn, win_ref[:, D_FF + lo:D_FF + lo + FF_CHUNK], preferred_element_type=F32,
                     precision=precision)
        act_ref[:, lo:lo + FF_CHUNK] = (jax.nn.silu(gate) * up).astype(cdt)
    y = x + jnp.dot(act_ref[...], wout_ref[...], preferred_element_type=F32, precision=precision)
    if final:
        y = _rms(y, gf_ref[...])
    o_ref[...] = y


def _ffn_kernel(*refs, final, precision):
    _ffn_body(*refs, final=final, precision=precision)


def _ffn_kmeans_kernel(pt_ref, x_ref, g_ref, win_ref, wout_ref, gf_ref, ck_ref, o_ref, km_ref,
                       act_ref, kbuf, sem, *, final, precision, n_pages, first_page):
    s = pl.program_id(0)
    cur = s % 2

    def copies(seq, buf):
        return [pltpu.make_async_copy(ck_ref.at[pt_ref[seq * n_pages + first_page + p]],
                                      kbuf.at[buf, p], sem.at[buf])
                for p in range(KMEANS_PAGES)]

    @pl.when(s == 0)
    def _():
        for cp in copies(s, cur):
            cp.start()

    @pl.when(s + 1 < pl.num_programs(0))
    def _():
        for cp in copies(s + 1, 1 - cur):
            cp.start()

    for cp in copies(s, cur):
        cp.wait()

    means = []
    for blk in range(KMEANS_PAGES // PAGES_PER_BLOCK):
        t = sum(jnp.sum(kbuf[cur, blk * PAGES_PER_BLOCK + h], axis=0)
                for h in range(PAGES_PER_BLOCK))
        t = sum(t[par * N_KV_HEADS:(par + 1) * N_KV_HEADS] for par in range(ROW_PARITIES))
        means.append(jnp.concatenate([t[kvh:kvh + 1, :] for kvh in range(N_KV_HEADS)], axis=1))
    km_ref[0] = jnp.concatenate(means, axis=0) * (1.0 / MOBA_BLOCK)

    _ffn_body(x_ref, g_ref, win_ref, wout_ref, gf_ref, o_ref, act_ref, final=final,
              precision=precision)


def _ffn_attn_kernel(pt_ref, sel_ref, x_ref, g_ref, win_ref, wout_ref, gf_ref, q_ref, kn_ref,
                     vn_ref, ck_ref, cv_ref, o_ref, ao_ref, act_ref, kbuf, vbuf, sem, *,
                     final, precision, n_pages):
    _attn_sample_body(pt_ref, sel_ref, q_ref, kn_ref, vn_ref, ck_ref, cv_ref, ao_ref, kbuf, vbuf,
                      sem, n_pages=n_pages)
    _ffn_body(x_ref, g_ref, win_ref, wout_ref, gf_ref, o_ref, act_ref, final=final,
              precision=precision)


def _ffn(x, g, w_in, w_out, g_final, layer, *, final, tile, kmeans=None, attn=None):
    n, d = x.shape
    precision = HIGHEST if w_in.dtype == F32 else None
    in_specs = [
        pl.BlockSpec((tile, d), lambda i, *_: (i, 0)),
        _resident((1, d)),
        _layer(layer, (d, 2 * D_FF)),
        _layer(layer, (D_FF, d)),
        _resident((1, d)),
    ]
    out_spec = pl.BlockSpec((tile, d), lambda i, *_: (i, 0))
    out_shape = jax.ShapeDtypeStruct((n, d), F32)
    act = pltpu.VMEM((tile, D_FF), w_in.dtype)
    any_space = pl.BlockSpec(memory_space=pl.ANY)
    if attn is not None:
        pt_flat, sel_flat, q3, k_new3, v_new3, ck, cv, n_pages = attn
        n_seq = n // tile
        assert q3.shape[0] == n_seq
        n_slots = MOBA_TOPK * PAGES_PER_BLOCK
        row = lambda width: pl.BlockSpec((1, 1, width), lambda i, *_: (i, 0, 0))
        gathered = pltpu.VMEM((2, N_HEADS, n_slots, PAGE_SIZE, HEAD_DIM), F32)
        return pl.pallas_call(
            functools.partial(_ffn_attn_kernel, final=final, precision=precision, n_pages=n_pages),
            name="ffn_attn",
            grid_spec=pltpu.PrefetchScalarGridSpec(
                num_scalar_prefetch=2,
                grid=(n_seq,),
                in_specs=in_specs + [row(D_MODEL), row(KV_DIM), row(KV_DIM), any_space, any_space],
                out_specs=[out_spec, row(D_MODEL)],
                scratch_shapes=[act, gathered, gathered, pltpu.SemaphoreType.DMA((2, 2))],
            ),
            out_shape=[out_shape, jax.ShapeDtypeStruct((n_seq, 1, D_MODEL), F32)],
            compiler_params=_params("arbitrary"),
        )(pt_flat, sel_flat, x, g, w_in, w_out, g_final, q3, k_new3, v_new3, ck, cv)
    if kmeans is None:
        return pl.pallas_call(
            functools.partial(_ffn_kernel, final=final, precision=precision),
            name="ffn",
            grid=(n // tile,),
            in_specs=in_specs,
            out_specs=out_spec,
            out_shape=out_shape,
            scratch_shapes=[act],
            compiler_params=_params("arbitrary"),
        )(x, g, w_in, w_out, g_final)

    pt_flat, cache, n_pages, first_page = kmeans
    n_seq = n // tile
    assert pt_flat.shape[0] == n_seq * n_pages
    blocks = KMEANS_PAGES // PAGES_PER_BLOCK
    return pl.pallas_call(
        functools.partial(_ffn_kmeans_kernel, final=final, precision=precision, n_pages=n_pages,
                          first_page=first_page),
        name="ffn_kmeans",
        grid_spec=pltpu.PrefetchScalarGridSpec(
            num_scalar_prefetch=1,
            grid=(n_seq,),
            in_specs=in_specs + [any_space],
            out_specs=[out_spec, pl.BlockSpec((1, blocks, KV_DIM), lambda i, *_: (i, 0, 0))],
            scratch_shapes=[act, pltpu.VMEM((2, KMEANS_PAGES) + PAGE_VIEW, F32),
                            pltpu.SemaphoreType.DMA((2,))],
        ),
        out_shape=[out_shape, jax.ShapeDtypeStruct((n_seq, blocks, KV_DIM), F32)],
        compiler_params=_params("arbitrary"),
    )(pt_flat, x, g, w_in, w_out, g_final, cache)


def _kv_prompt_kernel(x_ref, g_ref, w_ref, c_ref, slo_ref, shi_ref,
                      k_ref, v_ref, kb_ref, vt_ref, km_ref):
    xn = _rms(x_ref[0], g_ref[...]).astype(BF16)
    kv = jnp.dot(xn, w_ref[...], preferred_element_type=F32)
    tile4 = lambda r: jnp.concatenate([r[...]] * N_KV_HEADS, axis=1)
    k = _rope(kv[:, :KV_DIM], tile4(c_ref), tile4(slo_ref), tile4(shi_ref))
    v = kv[:, KV_DIM:]
    for kvh in range(N_KV_HEADS):
        lanes = slice(kvh * HEAD_DIM, (kvh + 1) * HEAD_DIM)
        rows = pl.ds(kvh, KV_TILE, stride=N_KV_HEADS)
        k_ref[0, rows, :] = k[:, lanes]
        v_ref[0, rows, :] = v[:, lanes]
    kb_ref[0] = k.astype(BF16)
    for bb in range(KV_TILE // MOBA_BLOCK):
        rows = slice(bb * MOBA_BLOCK, (bb + 1) * MOBA_BLOCK)
        vt_ref[0, bb] = v[rows, :].T.astype(BF16)
        km_ref[0, bb] = jnp.sum(k[rows, :], axis=0, keepdims=True) * (1.0 / MOBA_BLOCK)


def _kv_prompt(x, g, w_kv, c, s_lo, s_hi):
    bsz, t, d = x.shape
    tt = KV_TILE
    nbt = tt // MOBA_BLOCK
    nb = t // MOBA_BLOCK
    tab = pl.BlockSpec((tt, HEAD_DIM), lambda b, i: (i, 0))
    row_spec = pl.BlockSpec((1, tt, KV_DIM), lambda b, i: (b, i, 0))
    head_rows = pl.BlockSpec((1, tt * N_KV_HEADS, HEAD_DIM), lambda b, i: (b, i, 0))
    return pl.pallas_call(
        _kv_prompt_kernel,
        name="kv_prompt",
        grid=(bsz, t // tt),
        in_specs=[
            pl.BlockSpec((1, tt, d), lambda b, i: (b, i, 0)),
            _resident((1, d)),
            _resident((d, 2 * KV_DIM)),
            tab, tab, tab,
        ],
        out_specs=[
            head_rows, head_rows, row_spec,
            pl.BlockSpec((1, nbt, KV_DIM, MOBA_BLOCK), lambda b, i: (b, i, 0, 0)),
            pl.BlockSpec((1, nbt, 1, KV_DIM), lambda b, i: (b, i, 0, 0)),
        ],
        out_shape=[
            jax.ShapeDtypeStruct((bsz, t * N_KV_HEADS, HEAD_DIM), F32),
            jax.ShapeDtypeStruct((bsz, t * N_KV_HEADS, HEAD_DIM), F32),
            jax.ShapeDtypeStruct((bsz, t, KV_DIM), BF16),
            jax.ShapeDtypeStruct((bsz, nb, KV_DIM, MOBA_BLOCK), BF16),
            jax.ShapeDtypeStruct((bsz, nb, 1, KV_DIM), F32),
        ],
        compiler_params=_params("arbitrary", "arbitrary"),
    )(x, g, w_kv, c, s_lo, s_hi)


def _select_bias(gate, n_allowed, n_valid):
    nb, nq = gate.shape
    rowi = lax.broadcasted_iota(jnp.int32, (nb, nq), 0).astype(F32)
    g = jnp.where(rowi < n_allowed, gate, NEG)
    bias = jnp.where(rowi == n_allowed, 0.0, NEG)
    for r in range(MOBA_TOPK):
        mx = jnp.max(g, axis=0, keepdims=True)
        idx = jnp.min(jnp.where(g == mx, rowi, float(nb)), axis=0, keepdims=True)
        ok = (n_valid > r).astype(F32)
        idx = idx * ok - (1.0 - ok)
        pick = rowi == idx
        bias = jnp.where(pick, 0.0, bias)
        g = jnp.where(pick, NEG, g)
    return bias


def _attn_prompt_kernel(x_ref, g_ref, wq_ref, wo_ref, c_ref, slo_ref, shi_ref, k_ref, vt_ref,
                        km_ref, o_ref, q2s_ref, s0_ref, s1_ref, m_ref, acc_ref, attn_ref):
    tq = MOBA_BLOCK
    nq = ATTN_CHAIN_HEADS * tq
    i = pl.program_id(1)
    x = x_ref[0]
    xn = _rms(x, g_ref[...]).astype(BF16)
    q = jnp.dot(xn, wq_ref[...], preferred_element_type=F32)
    c, s_lo, s_hi = c_ref[...], slo_ref[...], shi_ref[...]
    own = i.astype(F32)

    key_pos = lax.broadcasted_iota(jnp.int32, (tq, nq), 0)
    qry_pos = lax.broadcasted_iota(jnp.int32, (tq, nq), 1) % tq
    causal = key_pos <= qry_pos
    n_chains = N_HEADS // ATTN_CHAIN_HEADS
    chain_heads = [range(ch * ATTN_CHAIN_HEADS, (ch + 1) * ATTN_CHAIN_HEADS)
                   for ch in range(n_chains)]
    kv_lanes = [slice((hs[0] // GROUP) * HEAD_DIM, (hs[0] // GROUP + 1) * HEAD_DIM)
                for hs in chain_heads]

    ones_rows = jnp.where(lax.broadcasted_iota(jnp.int32, (BF16_ROWS, tq), 0) == 0, 1.0, 0.0
                          ).astype(BF16)

    def vt_ones(j, lanes):
        return jnp.concatenate([vt_ref[0, j, lanes, :], ones_rows], axis=0)

    for kvh, lanes in enumerate(kv_lanes):
        q2 = jnp.concatenate(
            [_rope(q[:, h * HEAD_DIM:(h + 1) * HEAD_DIM], c, s_lo, s_hi)
             for h in chain_heads[kvh]], axis=0)
        gate = lax.dot_general(km_ref[0, :, lanes], q2, NT_DIMS, precision=HIGHEST,
                               preferred_element_type=F32)
        bias = _select_bias(gate, own, own)
        bias_t = jnp.concatenate(
            [bias, jnp.zeros((HEAD_DIM - bias.shape[0], nq), F32)], axis=0).T
        q2s = (q2 * (SM_SCALE * LOG2E)).astype(BF16)
        q2s_ref[kvh] = jnp.concatenate([q2s, bias_t.astype(BF16)], axis=1)
        m_ref[kvh] = jnp.full((1, nq), NEG, F32)
        acc_ref[kvh] = jnp.zeros(acc_ref.shape[1:], F32)

    block_lane = lax.broadcasted_iota(jnp.int32, (1, HEAD_DIM), 1)

    def qk(j, s_out):
        one_hot = jnp.broadcast_to(jnp.where(block_lane == j, 1.0, 0.0).astype(BF16),
                                   (tq, HEAD_DIM))
        for kvh, lanes in enumerate(kv_lanes):
            k_aug = jnp.concatenate([k_ref[0, j, :, lanes], one_hot], axis=1)
            s_out[kvh] = lax.dot_general(k_aug, q2s_ref[kvh], NT_DIMS,
                                         preferred_element_type=F32)

    def block(j, s_cur, s_nxt):
        if s_nxt is not None:
            qk(j + 1, s_nxt)
        for kvh, lanes in enumerate(kv_lanes):
            s = s_cur[kvh]
            if s_nxt is None:
                s = jnp.where(causal, s, NEG)
            m = m_ref[kvh]
            m_new = jnp.maximum(m, jnp.max(s, axis=0, keepdims=True))
            alpha = jnp.exp2(m - m_new)
            p = jnp.exp2(s - m_new)
            m_ref[kvh] = m_new
            acc_ref[kvh] = alpha * acc_ref[kvh] + jnp.dot(
                vt_ones(j, lanes), p.astype(BF16), preferred_element_type=F32)

    qk(0, s0_ref)
    bufs = (s0_ref, s1_ref)

    def run(first, count):
        for k in range(count):
            block(first + k, bufs[k % 2], bufs[(k + 1) % 2])

    def long_trip(t, carry):
        run(ATTN_UNROLL * t, ATTN_UNROLL)
        return carry

    n_long = i // ATTN_UNROLL
    lax.fori_loop(0, n_long, long_trip, 0)
    done = n_long * ATTN_UNROLL

    def pair_trip(t, carry):
        run(done + 2 * t, 2)
        return carry

    lax.fori_loop(0, (i - done) // 2, pair_trip, 0)

    @pl.when(i % 2 == 1)
    def _():
        run(i - 1, 1)
        block(i, s1_ref, None)

    @pl.when(i % 2 == 0)
    def _():
        block(i, s0_ref, None)

    for kvh in range(n_chains):
        o = (acc_ref[kvh, :HEAD_DIM, :] / acc_ref[kvh, HEAD_DIM:HEAD_DIM + 1, :]).T
        for j, h in enumerate(chain_heads[kvh]):
            attn_ref[:, h * HEAD_DIM:(h + 1) * HEAD_DIM] = o[j * tq:(j + 1) * tq, :].astype(BF16)

    o_ref[0] = x + jnp.dot(attn_ref[...], wo_ref[...], preferred_element_type=F32)


def _attn_prompt(x, g, w_q, w_o, c, s_lo, s_hi, kb, vt, km):
    bsz, t, d = x.shape
    tq = MOBA_BLOCK
    nb = t // tq
    n_chains, nq = N_HEADS // ATTN_CHAIN_HEADS, ATTN_CHAIN_HEADS * tq
    tab = pl.BlockSpec((tq, HEAD_DIM), lambda b, i: (i, 0))
    per_seq = lambda shape: pl.BlockSpec((1,) + shape, lambda b, i: (b,) + (0,) * len(shape),
                                         pipeline_mode=pl.Buffered(1))
    return pl.pallas_call(
        _attn_prompt_kernel,
        name="attn_prompt",
        grid=(bsz, nb),
        in_specs=[
            pl.BlockSpec((1, tq, d), lambda b, i: (b, i, 0)),
            _resident((1, d)),
            _resident((d, d)),
            _resident((d, d)),
            tab, tab, tab,
            per_seq((nb, tq, KV_DIM)),
            per_seq((nb, KV_DIM, tq)),
            per_seq((nb, KV_DIM)),
        ],
        out_specs=pl.BlockSpec((1, tq, d), lambda b, i: (b, i, 0)),
        out_shape=jax.ShapeDtypeStruct((bsz, t, d), F32),
        scratch_shapes=[
            pltpu.VMEM((n_chains, nq, 2 * HEAD_DIM), BF16),
            pltpu.VMEM((n_chains, tq, nq), F32),
            pltpu.VMEM((n_chains, tq, nq), F32),
            pltpu.VMEM((n_chains, 1, nq), F32),
            pltpu.VMEM((n_chains, HEAD_DIM + BF16_ROWS, nq), F32),
            pltpu.VMEM((tq, d), BF16),
        ],
        compiler_params=_params("arbitrary", "arbitrary"),
    )(x, g, w_q, w_o, c, s_lo, s_hi, kb, vt, km)


def _dot_hi(a, b):
    return jnp.dot(a, b, preferred_element_type=F32, precision=HIGHEST)


def _rg_sample_kernel(x_ref, cs_ref, h0_ref, g_ref, win_ref, cw_ref, cb_ref, wa_ref, wi_ref,
                      ba_ref, bi_ref, lam_ref, wout_ref, xo_ref, cso_ref, ho_ref):
    x = x_ref[...]
    proj = _dot_hi(_rms(x, g_ref[...]), win_ref[...])
    gate, u = proj[:, :LRU_WIDTH], proj[:, LRU_WIDTH:]
    uc = cb_ref[...] + cs_ref[0] * cw_ref[0:1, :]
    for j in range(1, CONV_WIDTH - 1):
        uc = uc + cs_ref[j] * cw_ref[j:j + 1, :]
    uc = uc + u * cw_ref[CONV_WIDTH - 1:CONV_WIDTH, :]
    for j in range(CONV_WIDTH - 2):
        cso_ref[j] = cs_ref[j + 1]
    cso_ref[CONV_WIDTH - 2] = u

    sp = _softplus(-lam_ref[...])
    h0 = h0_ref[...]
    hs = []
    for n in range(N_GATE_BLOCKS):
        sl = slice(n * GATE_BLOCK, (n + 1) * GATE_BLOCK)
        ucn = uc[:, sl]
        a, b = _lru_coeffs(_dot_hi(ucn, wa_ref[n]) + ba_ref[:, sl],
                           _dot_hi(ucn, wi_ref[n]) + bi_ref[:, sl], ucn, sp[:, sl])
        hs.append(b + a * h0[:, sl])
    h = jnp.concatenate(hs, axis=1)
    ho_ref[...] = h
    xo_ref[...] = x + _dot_hi(h * jax.nn.gelu(gate), wout_ref[...])


def _rg_sample(x, cs, h0, g, w_in, conv_w, conv_b, wa, wi, ba, bi, lam, w_out, layer):
    n, d = x.shape
    vec = _resident((1, LRU_WIDTH))
    gates = _layer(layer, (N_GATE_BLOCKS, GATE_BLOCK, GATE_BLOCK))
    return pl.pallas_call(
        _rg_sample_kernel,
        name="rg_sample",
        grid=(1,),
        in_specs=[
            _resident((n, d)), _resident((CONV_WIDTH - 1, n, LRU_WIDTH)), _resident((n, LRU_WIDTH)),
            _resident((1, d)), _layer(layer, (d, 2 * LRU_WIDTH)), _resident((CONV_WIDTH, LRU_WIDTH)),
            vec, gates, gates, vec, vec, vec, _layer(layer, (LRU_WIDTH, d)),
        ],
        out_specs=[_whole((n, d)), _whole((CONV_WIDTH - 1, n, LRU_WIDTH)), _whole((n, LRU_WIDTH))],
        out_shape=[
            jax.ShapeDtypeStruct((n, d), F32),
            jax.ShapeDtypeStruct((CONV_WIDTH - 1, n, LRU_WIDTH), F32),
            jax.ShapeDtypeStruct((n, LRU_WIDTH), F32),
        ],
        compiler_params=_params("arbitrary"),
    )(x, cs, h0, g, w_in, conv_w, conv_b, wa, wi, ba, bi, lam, w_out)


def _proj_sample_kernel(x_ref, g_ref, w_ref, c_ref, slo_ref, shi_ref, o_ref, *, n_rope_heads):
    y = _dot_hi(_rms(x_ref[...], g_ref[...]), w_ref[...])
    if n_rope_heads:
        tile = lambda r: jnp.concatenate([r[...]] * n_rope_heads, axis=1)
        width = n_rope_heads * HEAD_DIM
        rot = _rope(y[:, :width], tile(c_ref), tile(slo_ref), tile(shi_ref))
        y = jnp.concatenate([rot, y[:, width:]], axis=1) if width < y.shape[1] else rot
    o_ref[...] = y


def _proj_sample(x, g, w, layer, c, s_lo, s_hi, n_rope_heads):
    n, d = x.shape
    width = w.shape[2]
    tab = _resident((1, HEAD_DIM))
    return pl.pallas_call(
        functools.partial(_proj_sample_kernel, n_rope_heads=n_rope_heads),
        name="proj_sample",
        grid=(1,),
        in_specs=[_resident((n, d)), _resident((1, d)), _layer(layer, (d, width)), tab, tab, tab],
        out_specs=_whole((n, width)),
        out_shape=jax.ShapeDtypeStruct((n, width), F32),
        compiler_params=_params("arbitrary"),
    )(x, g, w, c, s_lo, s_hi)


def _gate_sample_kernel(q_ref, km_ref, sel_ref):
    km = km_ref[...]
    n_seq, nb, _ = km.shape
    blk = lax.broadcasted_iota(jnp.int32, (n_seq, nb, 1), 1).astype(F32)
    lane = lax.broadcasted_iota(jnp.int32, (n_seq, 1, HEAD_DIM), 2)
    for h in range(N_HEADS):
        kvh = h // GROUP
        qh = q_ref[:, :, h * HEAD_DIM:(h + 1) * HEAD_DIM]
        g = jnp.sum(km[:, :, kvh * HEAD_DIM:(kvh + 1) * HEAD_DIM] * qh, axis=-1, keepdims=True)
        out = jnp.zeros((n_seq, 1, HEAD_DIM), F32)
        for r in range(MOBA_TOPK):
            mx = jnp.max(g, axis=1, keepdims=True)
            idx = jnp.min(jnp.where(g == mx, blk, float(nb)), axis=1, keepdims=True)
            g = jnp.where(blk == idx, NEG, g)
            out = jnp.where(lane == r, idx, out)
        sel_ref[h] = out.astype(jnp.int32)


def _gate_sample(q3, km):
    n_seq = q3.shape[0]
    return pl.pallas_call(
        _gate_sample_kernel,
        name="gate_sample",
        out_shape=jax.ShapeDtypeStruct((N_HEADS, n_seq, 1, HEAD_DIM), jnp.int32),
        compiler_params=pltpu.CompilerParams(vmem_limit_bytes=VMEM_LIMIT),
    )(q3, km)


def _attn_sample_body(pt_ref, sel_ref, q_ref, kn_ref, vn_ref, ck_ref, cv_ref, o_ref,
                      kbuf, vbuf, sem, *, n_pages):
    b = pl.program_id(0)
    n_slots = MOBA_TOPK * PAGES_PER_BLOCK
    rows = PAGE_SIZE // ROW_PARITIES

    def copies(seq, buf):
        out = []
        for h in range(N_HEADS):
            kvh = h // GROUP
            for r in range(MOBA_TOPK):
                blk = sel_ref[(seq * N_HEADS + h) * MOBA_TOPK + r]
                for half in range(PAGES_PER_BLOCK):
                    page = pt_ref[seq * n_pages + PAGES_PER_BLOCK * blk + half]
                    slot = r * PAGES_PER_BLOCK + half
                    for par in range(ROW_PARITIES):
                        sub = par * N_KV_HEADS + kvh
                        dst = (buf, h, slot, pl.ds(par * rows, rows))
                        out.append(pltpu.make_async_copy(ck_ref.at[page, :, sub], kbuf.at[dst], sem.at[0, buf]))
                        out.append(pltpu.make_async_copy(cv_ref.at[page, :, sub], vbuf.at[dst], sem.at[1, buf]))
        return out

    cur = b % 2

    def start_all(seq, buf):
        for k, cp in enumerate(copies(seq, buf)):
            cp.start(priority=(k // 2) % 2)

    @pl.when(b == 0)
    def _():
        start_all(b, cur)

    @pl.when(b + 1 < pl.num_programs(0))
    def _():
        start_all(b + 1, 1 - cur)

    for cp in copies(b, cur):
        cp.wait()

    for h in range(N_HEADS):
        kvh = h // GROUP
        qh = q_ref[0, :, h * HEAD_DIM:(h + 1) * HEAD_DIM] * SM_SCALE
        kn = kn_ref[0, :, kvh * HEAD_DIM:(kvh + 1) * HEAD_DIM]
        vn = vn_ref[0, :, kvh * HEAD_DIM:(kvh + 1) * HEAD_DIM]
        kk = kbuf[cur, h].reshape(n_slots * PAGE_SIZE, HEAD_DIM)
        vv = vbuf[cur, h].reshape(n_slots * PAGE_SIZE, HEAD_DIM)
        s = jnp.sum(kk * qh, axis=-1, keepdims=True)
        s_new = jnp.sum(kn * qh, axis=-1, keepdims=True)
        m = jnp.maximum(jnp.max(s, axis=0, keepdims=True), s_new)
        p = jnp.exp(s - m)
        p_new = jnp.exp(s_new - m)
        l = jnp.sum(p, axis=0, keepdims=True) + p_new
        acc = jnp.sum(p * vv, axis=0, keepdims=True) + p_new * vn
        o_ref[0, :, h * HEAD_DIM:(h + 1) * HEAD_DIM] = acc / l


def _oproj_sample_kernel(x_ref, a_ref, w_ref, o_ref):
    o_ref[...] = x_ref[...] + _dot_hi(a_ref[...], w_ref[...])


def _oproj_sample(x, attn, w_o, layer):
    n, d = x.shape
    return pl.pallas_call(
        _oproj_sample_kernel,
        name="oproj_sample",
        grid=(1,),
        in_specs=[_resident((n, d)), _resident((n, d)), _layer(layer, (d, d))],
        out_specs=_whole((n, d)),
        out_shape=jax.ShapeDtypeStruct((n, d), F32),
        compiler_params=_params("arbitrary"),
    )(x, attn, w_o)


def _step(x_prompt, x_sample3, state_conv, state_h, ck, cv, pt_flat, n_pages, w):
    bsz, t, d = x_prompt.shape
    n_seq = x_sample3.shape[0]
    row = lambda a: a.reshape(1, -1)
    bf = lambda a: a.astype(BF16)
    n_rg = w["w_rg_in"].shape[0]
    depth = w["g_mix"].shape[0]
    nb = t // MOBA_BLOCK
    assert n_pages == n_rg * KMEANS_PAGES and (bsz * t) // FFN_TILE == n_seq

    c_p, slo_p, shi_p = _rope_tables(jnp.arange(t, dtype=jnp.int32))
    c_s, slo_s, shi_s = _rope_tables(jnp.full((1,), n_pages * PAGE_SIZE, jnp.int32))
    w_ffn_in_bf, w_ffn_out_bf = bf(w["w_ffn_in"]), bf(w["w_ffn_out"])

    def ffn_prompt(x, l, **side):
        out = _ffn(x.reshape(bsz * t, d), row(w["g_ffn"][l]), w_ffn_in_bf, w_ffn_out_bf,
                   row(w["g_final"]), l, final=(l == depth - 1), tile=FFN_TILE, **side)
        return out[0].reshape(bsz, t, d), out[1]

    def ffn_sample(x, l):
        return _ffn(x, row(w["g_ffn"][l]), w_ffn_in_bf, w_ffn_out_bf, row(w["g_final"]), l,
                    final=(l == depth - 1), tile=n_seq)

    xp, xs = x_prompt, x_sample3.reshape(n_seq, d)
    p_convs, p_hs, s_convs, s_hs, kmeans = [], [], [], [], []
    for l in range(n_rg):
        wg = jnp.concatenate([w["rg_gate_a_w"][l], w["rg_gate_i_w"][l]], axis=-1)
        xp, conv, h = _rg_prompt(
            xp, row(w["g_mix"][l]), bf(w["w_rg_in"][l]), w["rg_conv_w"][l], row(w["rg_conv_b"][l]),
            bf(wg), row(w["rg_gate_a_b"][l]), row(w["rg_gate_i_b"][l]), row(w["rg_lambda"][l]),
            bf(w["w_rg_out"][l]))
        p_convs.append(conv)
        p_hs.append(h[:, 0, :])
        xp, km_l = ffn_prompt(xp, l, kmeans=(pt_flat, ck, n_pages, l * KMEANS_PAGES))
        kmeans.append(km_l)

        xs, conv, h = _rg_sample(
            xs, jnp.swapaxes(state_conv[l], 0, 1), state_h[l], row(w["g_mix"][l]), w["w_rg_in"],
            w["rg_conv_w"][l], row(w["rg_conv_b"][l]), w["rg_gate_a_w"], w["rg_gate_i_w"],
            row(w["rg_gate_a_b"][l]), row(w["rg_gate_i_b"][l]), row(w["rg_lambda"][l]), w["w_rg_out"], l)
        s_convs.append(jnp.swapaxes(conv, 0, 1))
        s_hs.append(h)
        xs = ffn_sample(xs, l)

    k, v, kb, vt, km_p = _kv_prompt(xp, row(w["g_kv"]), bf(w["w_kv"]), c_p, slo_p, shi_p)
    kb = kb.reshape(bsz, nb, MOBA_BLOCK, KV_DIM)
    km_p = km_p.reshape(bsz, nb, KV_DIM)
    kv_s = _proj_sample(xs, row(w["g_kv"]), w["w_kv"][None], 0, c_s, slo_s, shi_s, N_KV_HEADS)
    k_new, v_new = kv_s[:, :KV_DIM], kv_s[:, KV_DIM:]
    km_s = jnp.concatenate(kmeans, axis=1)

    for a in range(depth - n_rg):
        l = n_rg + a
        xp = _attn_prompt(xp, row(w["g_mix"][l]), bf(w["w_q"][a]), bf(w["w_o"][a]), c_p, slo_p, shi_p,
                          kb, vt, km_p)
        q3 = _proj_sample(xs, row(w["g_mix"][l]), w["w_q"], a, c_s, slo_s, shi_s,
                          N_HEADS).reshape(n_seq, 1, d)
        sel = _gate_sample(q3, km_s)[:, :, 0, :MOBA_TOPK]
        sel_flat = jnp.transpose(sel, (1, 0, 2)).reshape(-1)
        xp, attn_s = ffn_prompt(xp, l, attn=(pt_flat, sel_flat, q3, k_new.reshape(n_seq, 1, KV_DIM),
                                             v_new.reshape(n_seq, 1, KV_DIM), ck, cv, n_pages))
        xs = _oproj_sample(xs, attn_s.reshape(n_seq, d), w["w_o"], a)
        xs = ffn_sample(xs, l)

    shape_p = (bsz, t, N_KV_HEADS, HEAD_DIM)
    shape_s = (n_seq, 1, N_KV_HEADS, HEAD_DIM)
    return (xp, xs.reshape(n_seq, 1, d), jnp.stack(p_convs), jnp.stack(p_hs),
            k.reshape(shape_p), v.reshape(shape_p), jnp.stack(s_convs), jnp.stack(s_hs),
            k_new.reshape(shape_s), v_new.reshape(shape_s))


def kernel(x_prompt, x_sample, cache_k, cache_v, state_conv, state_h, page_table,
           g_mix, g_ffn, w_rg_in, rg_conv_w, rg_conv_b, rg_gate_a_w, rg_gate_a_b,
           rg_gate_i_w, rg_gate_i_b, rg_lambda, w_rg_out, g_kv, w_kv, w_q, w_o,
           w_ffn_in, w_ffn_out, g_final):
    assert x_sample.shape[1] == 1 and x_prompt.shape[1] % MOBA_BLOCK == 0
    assert MOBA_TOPK <= page_table.shape[1] // PAGES_PER_BLOCK
    w = dict(g_mix=g_mix, g_ffn=g_ffn, w_rg_in=w_rg_in, rg_conv_w=rg_conv_w, rg_conv_b=rg_conv_b,
             rg_gate_a_w=rg_gate_a_w, rg_gate_a_b=rg_gate_a_b, rg_gate_i_w=rg_gate_i_w,
             rg_gate_i_b=rg_gate_i_b, rg_lambda=rg_lambda, w_rg_out=w_rg_out, g_kv=g_kv, w_kv=w_kv,
             w_q=w_q, w_o=w_o, w_ffn_in=w_ffn_in, w_ffn_out=w_ffn_out, g_final=g_final)
    ck = cache_k.reshape((cache_k.shape[0],) + PAGE_VIEW)
    cv = cache_v.reshape((cache_v.shape[0],) + PAGE_VIEW)
    return _step(x_prompt, x_sample, state_conv, state_h, ck, cv, page_table.reshape(-1),
                 page_table.shape[1], w)
```
